```python
import jax, jax.numpy as jnp
from jax import lax
import numpy as np

D_MODEL = 1024
BATCH = 32
SEQ = 2048
DEPTH = 2
DEC_BATCH = 8
DEC_SEQ = 16
PAST_LEN = 4096

CHUNK = 64
N_META = 16
Q_BLOCK = 128
EPS = 1e-6
MLA_HEADS = 8
QK_NOPE = 64
QK_ROPE = 32
V_HEAD = 64
Q_LORA = 256
KV_LORA = 128
ROPE_THETA = 10000.0
HG_HEADS = 4
HG_DK = 128
HG_DV = 128
D_FF = 2816
CONV_W = 3

MLA_WIDTH = MLA_HEADS * V_HEAD
HG_KW = HG_HEADS * HG_DK
HG_WIDTH = HG_HEADS * HG_DV
MIX_WIDTH = MLA_WIDTH + HG_WIDTH
IN_WIDTH = Q_LORA + KV_LORA + QK_ROPE + 2 * HG_KW + 2 * HG_WIDTH

kernel_name = 'hymba_mla_hgrn2_convglu_stream_step'


def rmsnorm(x, w):
    xf = x.astype(jnp.float32)
    y = xf * lax.rsqrt(jnp.mean(xf * xf, axis=-1, keepdims=True) + EPS)
    return (y * w.astype(jnp.float32)).astype(x.dtype)


def rope(x, pos):
    half = QK_ROPE // 2
    inv = ROPE_THETA ** (-jnp.arange(half, dtype=jnp.float32) / half)
    ang = pos.astype(jnp.float32)[:, None] * inv
    ang = ang.reshape((ang.shape[0],) + (1,) * (x.ndim - 3) + (half,))
    cos, sin = jnp.cos(ang), jnp.sin(ang)
    xf = x.astype(jnp.float32)
    x1, x2 = xf[..., :half], xf[..., half:]
    return jnp.concatenate([x1 * cos - x2 * sin, x1 * sin + x2 * cos], axis=-1).astype(x.dtype)


def split_in(z):
    sizes = (Q_LORA, KV_LORA, QK_ROPE, HG_KW, HG_KW, HG_WIDTH, HG_WIDTH)
    outs, off = [], 0
    for s in sizes:
        outs.append(z[..., off:off + s])
        off += s
    return outs


def mla_core(q_lat, q_pe, kv_lat, k_pe, q_pos, k_pos):
    scale = (QK_NOPE + QK_ROPE) ** -0.5
    s = (jnp.einsum('bthr,bsr->bhts', q_lat, kv_lat)
         + jnp.einsum('bthp,bsp->bhts', q_pe, k_pe)).astype(jnp.float32) * scale
    allowed = (q_pos // CHUNK)[:, None] >= (k_pos // CHUNK)[None, :]
    s = jnp.where(allowed[None, None], s, -jnp.inf)
    p = jax.nn.softmax(s, axis=-1).astype(kv_lat.dtype)
    return jnp.einsum('bhts,bsr->bthr', p, kv_lat)


def mla_prompt(q_lat, q_pe, kv_lat, k_pe, pos):
    L = q_lat.shape[1]
    outs = []
    for start in range(0, L, Q_BLOCK):
        stop = min(start + Q_BLOCK, L)
        last_chunk = (stop - 1 - N_META) // CHUNK
        kmax = min(L, N_META + (last_chunk + 1) * CHUNK)
        outs.append(mla_core(q_lat[:, start:stop], q_pe[:, start:stop], kv_lat[:, :kmax],
                             k_pe[:, :kmax], pos[start:stop], pos[:kmax]))
    return jnp.concatenate(outs, axis=1)


def hgrn_chunk(S, q, gl, k, v):
    C = q.shape[1]
    G = jnp.cumsum(gl, axis=1)
    G_last = G[:, -1]
    o_inter = jnp.einsum('bthk,bhkv->bthv', q * jnp.exp(G), S)
    tri = jnp.tril(jnp.ones((C, C), dtype=bool))
    diff = G[:, :, None] - G[:, None, :]
    decay = jnp.exp(jnp.where(tri[None, :, :, None, None], diff, -jnp.inf))
    A = jnp.einsum('bthk,btshk,bshk->bhts', q, decay, k)
    o = o_inter + jnp.einsum('bhts,bshv->bthv', A, v)
    S_new = jnp.exp(G_last)[..., None] * S + jnp.einsum('bshk,bshv->bhkv', k * jnp.exp(G_last[:, None] - G), v)
    return S_new, o


def hgrn2(hq, hf, hi, hg, lb, gain, S0, is_prompt):
    B, T, _ = hq.shape
    f32 = jnp.float32
    q = jax.nn.silu(hq.astype(f32)).reshape(B, T, HG_HEADS, HG_DK)
    zf = hf.astype(f32).reshape(B, T, HG_HEADS, HG_DK)
    lbh = lb.reshape(HG_HEADS, HG_DK)
    f = lbh + (1.0 - lbh) * jax.nn.sigmoid(zf)
    gl = jnp.log(f)
    k = (1.0 - lbh) * jax.nn.sigmoid(-zf)
    v = hi.astype(f32).reshape(B, T, HG_HEADS, HG_DV)
    S0 = S0.astype(f32)
    if is_prompt:
        n_pad = (-T) % CHUNK
        n_chunks = (T + n_pad) // CHUNK

        def to_chunks(a):
            a = jnp.pad(a, ((0, 0), (n_pad, 0), (0, 0), (0, 0)))
            return a.reshape(B, n_chunks, CHUNK, HG_HEADS, a.shape[-1]).swapaxes(0, 1)

        def step(S, xs):
            return hgrn_chunk(S, *xs)

        S_new, o = lax.scan(step, S0, (to_chunks(q), to_chunks(gl), to_chunks(k), to_chunks(v)))
        o = o.swapaxes(0, 1).reshape(B, n_chunks * CHUNK, HG_HEADS, HG_DV)[:, n_pad:]
    else:
        S_new, o = hgrn_chunk(S0, q, gl, k, v)
    o = rmsnorm(o, gain) * jax.nn.silu(hg.astype(f32).reshape(B, T, HG_HEADS, HG_DV))
    return o.reshape(B, T, HG_WIDTH).astype(hq.dtype), S_new


def token_mixers(h, pos, p, lb, kv_past, pe_past, past_pos, S0, is_prompt):
    B, T, _ = h.shape
    c_q, c_kv, k_pe, hq, hf, hi, hg = split_in(h @ p['w_in'])
    q = (rmsnorm(c_q, p['q_norm']) @ p['w_uq']).reshape(B, T, MLA_HEADS, QK_NOPE + QK_ROPE)
    q_nope = q[..., :QK_NOPE]
    q_pe = rope(q[..., QK_NOPE:], pos)
    kv_lat = rmsnorm(c_kv, p['kv_norm'])
    k_pe = rope(k_pe, pos)
    q_lat = jnp.einsum('bthd,rhd->bthr', q_nope, p['w_uk'])
    if is_prompt:
        o_lat = mla_prompt(q_lat, q_pe, kv_lat, k_pe, pos)
    else:
        o_lat = mla_core(q_lat, q_pe,
                         jnp.concatenate([kv_past.astype(kv_lat.dtype), kv_lat], axis=1),
                         jnp.concatenate([pe_past.astype(k_pe.dtype), k_pe], axis=1),
                         pos, jnp.concatenate([past_pos, pos]))
    o_a = jnp.einsum('bthr,rhd->bthd', o_lat, p['w_uv']).reshape(B, T, MLA_WIDTH)
    o_b, S_new = hgrn2(hq, hf, hi, hg, lb, p['hg_out_norm'], S0, is_prompt)
    y = jnp.concatenate([o_a, o_b], axis=-1) @ p['w_out']
    return y, kv_lat, k_pe, S_new


def conv_ffn(h, p, buf):
    u = h @ p['w_ffn_up']
    a, v = u[..., :D_FF], u[..., D_FF:]
    T = a.shape[1]
    ext = jnp.concatenate([buf.astype(a.dtype), a], axis=1)
    w = p['ffn_conv_w']
    c = p['ffn_conv_b'] + sum(ext[:, j:j + T] * w[j] for j in range(CONV_W))
    out = (jax.nn.silu(c) * v) @ p['w_ffn_down']
    return out, ext[:, -(CONV_W - 1):]


def layer(x, pos, p, lb, kv_past, pe_past, past_pos, S0, conv_buf, is_prompt):
    h = rmsnorm(x, p['norm_mix_pre'])
    mix, kv_lat, k_pe, S_new = token_mixers(h, pos, p, lb, kv_past, pe_past, past_pos, S0, is_prompt)
    x = x + rmsnorm(mix, p['norm_mix_post'])
    h = rmsnorm(x, p['norm_ffn_pre'])
    f, buf = conv_ffn(h, p, conv_buf)
    x = x + rmsnorm(f, p['norm_ffn_post'])
    return x, kv_lat, k_pe, S_new, buf


def setup_inputs(seed: int = 0) -> dict:
    key = jax.random.key(seed)
    ks = jax.random.split(key, 26)
    f32 = jnp.float32

    def nrm(k, shape, scale=1.0):
        return jax.random.normal(k, shape, f32) * scale

    def gain(k, shape):
        return 1.0 + 0.05 * jax.random.normal(k, shape, f32)

    return {
        'x_prompt': nrm(ks[0], (BATCH, SEQ, D_MODEL)),
        'x_sample': nrm(ks[1], (DEC_BATCH, DEC_SEQ, D_MODEL)),
        'cache_kv_latent': nrm(ks[2], (DEPTH, DEC_BATCH, PAST_LEN, KV_LORA)),
        'cache_k_rope': nrm(ks[3], (DEPTH, DEC_BATCH, PAST_LEN, QK_ROPE)),
        'state_hgrn': nrm(ks[4], (DEPTH, DEC_BATCH, HG_HEADS, HG_DK, HG_DV), 0.5),
        'state_ffn_conv': nrm(ks[5], (DEPTH, DEC_BATCH, CONV_W - 1, D_FF)),
        'meta_tokens': nrm(ks[6], (N_META, D_MODEL)),
        'w_in': nrm(ks[7], (DEPTH, D_MODEL, IN_WIDTH), D_MODEL ** -0.5),
        'norm_mix_pre': gain(ks[8], (DEPTH, D_MODEL)),
        'norm_mix_post': gain(ks[9], (DEPTH, D_MODEL)),
        'q_norm': gain(ks[10], (DEPTH, Q_LORA)),
        'kv_norm': gain(ks[11], (DEPTH, KV_LORA)),
        'w_uq': nrm(ks[12], (DEPTH, Q_LORA, MLA_HEADS * (QK_NOPE + QK_ROPE)), Q_LORA ** -0.5),
        'w_uk': nrm(ks[13], (DEPTH, KV_LORA, MLA_HEADS, QK_NOPE), KV_LORA ** -0.5),
        'w_uv': nrm(ks[14], (DEPTH, KV_LORA, MLA_HEADS, V_HEAD), KV_LORA ** -0.5),
        'hg_lower_bounds': 1.0 + 0.1 * jax.random.normal(ks[15], (DEPTH, HG_KW), f32),
        'hg_out_norm': gain(ks[16], (DEPTH, HG_DV)),
        'w_out': nrm(ks[17], (DEPTH, MIX_WIDTH, D_MODEL), MIX_WIDTH ** -0.5),
        'norm_ffn_pre': gain(ks[18], (DEPTH, D_MODEL)),
        'norm_ffn_post': gain(ks[19], (DEPTH, D_MODEL)),
        'w_ffn_up': nrm(ks[20], (DEPTH, D_MODEL, 2 * D_FF), D_MODEL ** -0.5),
        'ffn_conv_w': nrm(ks[21], (DEPTH, CONV_W, D_FF), CONV_W ** -0.5),
        'ffn_conv_b': nrm(ks[22], (DEPTH, D_FF), 0.01),
        'w_ffn_down': nrm(ks[23], (DEPTH, D_FF, D_MODEL), D_FF ** -0.5),
        'final_norm': gain(ks[24], (D_MODEL,)),
    }


def reference(x_prompt, x_sample, cache_kv_latent, cache_k_rope, state_hgrn, state_ffn_conv,
              meta_tokens, w_in, norm_mix_pre, norm_mix_post, q_norm, kv_norm, w_uq, w_uk, w_uv,
              hg_lower_bounds, hg_out_norm, w_out, norm_ffn_pre, norm_ffn_post, w_ffn_up,
              ffn_conv_w, ffn_conv_b, w_ffn_down, final_norm):
    lb_sm = jax.nn.softmax(hg_lower_bounds.astype(jnp.float32), axis=0)
    lb_all = jnp.cumsum(lb_sm, axis=0) - lb_sm[0]

    B = x_prompt.shape[0]
    Bs, Ts = x_sample.shape[0], x_sample.shape[1]
    meta = jnp.broadcast_to(meta_tokens.astype(x_prompt.dtype)[None], (B, N_META, D_MODEL))
    xp = jnp.concatenate([meta, x_prompt], axis=1)
    pos_p = jnp.arange(xp.shape[1], dtype=jnp.int32) - N_META
    past_pos = jnp.arange(cache_kv_latent.shape[2], dtype=jnp.int32)
    pos_s = cache_kv_latent.shape[2] + jnp.arange(Ts, dtype=jnp.int32)
    xs = x_sample

    p_kv, p_pe, p_S, p_buf = [], [], [], []
    s_kv, s_pe, s_S, s_buf = [], [], [], []
    for l in range(DEPTH):
        p = {'w_in': w_in[l], 'norm_mix_pre': norm_mix_pre[l], 'norm_mix_post': norm_mix_post[l],
             'q_norm': q_norm[l], 'kv_norm': kv_norm[l], 'w_uq': w_uq[l], 'w_uk': w_uk[l],
             'w_uv': w_uv[l], 'hg_out_norm': hg_out_norm[l], 'w_out': w_out[l],
             'norm_ffn_pre': norm_ffn_pre[l], 'norm_ffn_post': norm_ffn_post[l],
             'w_ffn_up': w_ffn_up[l], 'ffn_conv_w': ffn_conv_w[l], 'ffn_conv_b': ffn_conv_b[l],
             'w_ffn_down': w_ffn_down[l]}
        S0_p = jnp.zeros((B, HG_HEADS, HG_DK, HG_DV), jnp.float32)
        buf0_p = jnp.zeros((B, CONV_W - 1, D_FF), xp.dtype)
        xp, kv, pe, S, buf = layer(xp, pos_p, p, lb_all[l], None, None, None, S0_p, buf0_p, True)
        p_kv.append(kv); p_pe.append(pe); p_S.append(S); p_buf.append(buf)
        xs, kv, pe, S, buf = layer(xs, pos_s, p, lb_all[l], cache_kv_latent[l], cache_k_rope[l],
                                   past_pos, state_hgrn[l], state_ffn_conv[l], False)
        s_kv.append(kv); s_pe.append(pe); s_S.append(S); s_buf.append(buf)

    y_prompt = rmsnorm(xp[:, N_META:], final_norm)
    y_sample = rmsnorm(xs, final_norm)
    return (y_prompt, y_sample,
            jnp.stack(p_kv), jnp.stack(p_pe), jnp.stack(p_S), jnp.stack(p_buf),
            jnp.stack(s_kv), jnp.stack(s_pe), jnp.stack(s_S), jnp.stack(s_buf))
```

```python
import functools

import jax
import jax.numpy as jnp
from jax import lax
from jax.experimental import pallas as pl
from jax.experimental.pallas import tpu as pltpu

F32 = jnp.float32
BF16 = jnp.bfloat16

CHUNK = 64
N_META = 16
PAD_FRONT = CHUNK - N_META
EPS = 1e-6
MLA_HEADS = 8
QK_NOPE = 64
QK_ROPE = 32
V_HEAD = 64
Q_LORA = 256
KV_LORA = 128
ROPE_THETA = 10000.0
HG_HEADS = 4
HG_DK = 128
HG_DV = 128
CONV_W = 3

HG_KW = HG_HEADS * HG_DK
HG_WIDTH = HG_HEADS * HG_DV
MLA_WIDTH = MLA_HEADS * V_HEAD
LANES = 128
KEY_BLOCK = 256
QCAT = KV_LORA + LANES
SOFTMAX_SCALE = (QK_NOPE + QK_ROPE) ** -0.5
HGRN_FAST_RANGE = 80.0
VMEM_LIMIT = 56 * 1024 * 1024


def _row_tile(n, target):
    if n <= target:
        return n
    best = None
    for t in range(16, target + 1, 16):
        if n % t == 0:
            best = t
    assert best is not None, (n, target)
    return best


def _const_spec(shape):
    nd = len(shape)
    return pl.BlockSpec(shape, lambda *_: (0,) * nd, pipeline_mode=pl.Buffered(1))


def _rms(x, w):
    return x * lax.rsqrt(jnp.mean(x * x, axis=-1, keepdims=True) + EPS) * w


def _silu(x):
    return x * jax.nn.sigmoid(x)


def _mm(a, b):
    return jnp.dot(a.astype(BF16), b.astype(BF16), preferred_element_type=F32)


def _mm_nt(a, b):
    return lax.dot_general(a.astype(BF16), b.astype(BF16), (((1,), (1,)), ((), ())),
                           preferred_element_type=F32)


def _mm_tn(a, b):
    return lax.dot_general(a.astype(BF16), b.astype(BF16), (((0,), (0,)), ((), ())),
                           preferred_element_type=F32)


def _in_proj_body(x_ref, cos_ref, sn_ref, sp_ref, npre_ref, win_ref, qn_ref, kvn_ref, wuq_ref, wuk_ref,
                  qcat_ref, kcat_ref, kv_ref, pe_ref, zh_ref):
    cos, sn, sp = cos_ref[...], sn_ref[...], sp_ref[...]

    def rope(v):
        return v * cos + pltpu.roll(v, LANES - QK_ROPE // 2, 1) * sn + pltpu.roll(v, QK_ROPE // 2, 1) * sp

    h = _rms(x_ref[0], npre_ref[...])
    z = _mm(h, win_ref[...])
    zh_ref[0] = z[:, Q_LORA + KV_LORA + LANES:]
    kv_lat = _rms(z[:, Q_LORA:Q_LORA + KV_LORA], kvn_ref[...])
    k_pe = rope(z[:, Q_LORA + KV_LORA:Q_LORA + KV_LORA + LANES])
    kv_ref[0] = kv_lat
    pe_ref[0] = k_pe[:, :QK_ROPE]
    kcat_ref[0, :, :KV_LORA] = kv_lat.astype(BF16)
    kcat_ref[0, :, KV_LORA:] = k_pe.astype(BF16)

    cqn = _rms(z[:, :Q_LORA], qn_ref[...])
    q = _mm(cqn, wuq_ref[...]) * SOFTMAX_SCALE
    nope_w = MLA_HEADS * QK_NOPE
    for j in range(MLA_HEADS // 2):
        ql = _mm(q[:, 2 * QK_NOPE * j:2 * QK_NOPE * (j + 1)], wuk_ref[j])
        for hh in range(2):
            hd = 2 * j + hh
            qcat_ref[0, hd, :, :KV_LORA] = ql[:, KV_LORA * hh:KV_LORA * (hh + 1)].astype(BF16)
            qcat_ref[0, hd, :, KV_LORA:] = rope(q[:, nope_w + LANES * hd:nope_w + LANES * (hd + 1)]).astype(BF16)


def _in_proj(x, tabs, w):
    bx, t, d = x.shape
    tm = _row_tile(t, 384)
    nt = t // tm
    row = lambda shape: pl.BlockSpec((1, tm) + shape, lambda b, i: (b, i) + (0,) * len(shape))
    tab = pl.BlockSpec((tm, LANES), lambda b, i: (i, 0))
    zw = 2 * HG_KW + 2 * HG_WIDTH
    out_shape = (
        jax.ShapeDtypeStruct((bx, MLA_HEADS, t, QCAT), BF16),
        jax.ShapeDtypeStruct((bx, t, QCAT), BF16),
        jax.ShapeDtypeStruct((bx, t, KV_LORA), F32),
        jax.ShapeDtypeStruct((bx, t, QK_ROPE), F32),
        jax.ShapeDtypeStruct((bx, t, zw), F32),
    )
    out_specs = (
        pl.BlockSpec((1, MLA_HEADS, tm, QCAT), lambda b, i: (b, 0, i, 0)),
        row((QCAT,)), row((KV_LORA,)), row((QK_ROPE,)), row((zw,)),
    )
    weights = (w['norm_mix_pre'], w['w_in'], w['q_norm'], w['kv_norm'], w['w_uq'], w['w_uk'])
    return pl.pallas_call(
        _in_proj_body,
        grid=(bx, nt),
        in_specs=[row((d,)), tab, tab, tab] + [_const_spec(a.shape) for a in weights],
        out_specs=out_specs,
        out_shape=out_shape,
        compiler_params=pltpu.CompilerParams(dimension_semantics=("parallel", "parallel"),
                                             vmem_limit_bytes=VMEM_LIMIT),
        name="in_proj",
    )(x, *tabs, *weights)


def _mla_prompt_body(q_ref, k_ref, o_ref, *, lp):
    c = pl.program_id(1)
    rows = MLA_HEADS * CHUNK
    q = q_ref[0].reshape(rows, QCAT)
    hi = (c + 1) * CHUNK
    nkb = (hi + KEY_BLOCK - 1) // KEY_BLOCK

    def step(kb, carry):
        m, l, acc = carry
        lo = kb * KEY_BLOCK
        start = pl.multiple_of(jnp.minimum(lo, lp - KEY_BLOCK), CHUNK)
        kblk = k_ref[0, pl.ds(start, KEY_BLOCK), :]
        s = _mm_nt(q, kblk)
        key = start + lax.broadcasted_iota(jnp.int32, (1, KEY_BLOCK), 1)
        s = jnp.where(key >= jnp.maximum(lo, PAD_FRONT), jnp.where(key < hi, s, -jnp.inf), -jnp.inf)
        m_new = jnp.maximum(m, jnp.max(s, axis=-1, keepdims=True))
        alpha = jnp.exp(m - m_new)
        p = jnp.exp(s - m_new)
        l = alpha * l + jnp.sum(p, axis=-1, keepdims=True)
        acc = alpha * acc + _mm(p, kblk[:, :KV_LORA])
        return m_new, l, acc

    init = (jnp.full((rows, 1), -jnp.inf, F32), jnp.zeros((rows, 1), F32), jnp.zeros((rows, KV_LORA), F32))
    _, l, acc = lax.fori_loop(0, nkb, step, init)
    o = acc / l
    for hd in range(MLA_HEADS):
        o_ref[0, :, KV_LORA * hd:KV_LORA * (hd + 1)] = o[CHUNK * hd:CHUNK * (hd + 1)].astype(BF16)


def _mla_prompt(qcat, kcat):
    b, _, lp, _ = qcat.shape
    assert lp % CHUNK == 0 and lp >= KEY_BLOCK
    return pl.pallas_call(
        functools.partial(_mla_prompt_body, lp=lp),
        grid=(b, lp // CHUNK),
        in_specs=[pl.BlockSpec((1, MLA_HEADS, CHUNK, QCAT), lambda i, c: (i, 0, c, 0)),
                  pl.BlockSpec((1, lp, QCAT), lambda i, c: (i, 0, 0))],
        out_specs=pl.BlockSpec((1, CHUNK, MLA_HEADS * KV_LORA), lambda i, c: (i, c, 0)),
        out_shape=jax.ShapeDtypeStruct((b, lp, MLA_HEADS * KV_LORA), BF16),
        compiler_params=pltpu.CompilerParams(dimension_semantics=("parallel", "parallel"),
                                             vmem_limit_bytes=VMEM_LIMIT),
        name="mla_prompt",
    )(qcat, kcat)


def _mla_sample_body(q_ref, k_ref, ckv_ref, cpe_ref, o_ref, *, ts):
    rows = MLA_HEADS * ts
    q = q_ref[0].reshape(rows, QCAT)
    ckv = ckv_ref[0].astype(BF16)
    knew = k_ref[0]
    s_c = _mm_nt(q[:, :KV_LORA], ckv) + _mm_nt(q[:, KV_LORA:KV_LORA + QK_ROPE], cpe_ref[0])
    s_n = _mm_nt(q, knew)
    m = jnp.maximum(jnp.max(s_c, axis=-1, keepdims=True), jnp.max(s_n, axis=-1, keepdims=True))
    p_c = jnp.exp(s_c - m)
    p_n = jnp.exp(s_n - m)
    l = jnp.sum(p_c, axis=-1, keepdims=True) + jnp.sum(p_n, axis=-1, keepdims=True)
    o = (_mm(p_c, ckv) + _mm(p_n, knew[:, :KV_LORA])) / l
    for hd in range(MLA_HEADS):
        o_ref[0, :, KV_LORA * hd:KV_LORA * (hd + 1)] = o[ts * hd:ts * (hd + 1)].astype(BF16)


def _mla_sample(qcat, kcat, cache_kv, cache_pe):
    b, _, ts, _ = qcat.shape
    past = cache_kv.shape[1]
    return pl.pallas_call(
        functools.partial(_mla_sample_body, ts=ts),
        grid=(b,),
        in_specs=[pl.BlockSpec((1, MLA_HEADS, ts, QCAT), lambda i: (i, 0, 0, 0)),
                  pl.BlockSpec((1, ts, QCAT), lambda i: (i, 0, 0)),
                  pl.BlockSpec((1, past, KV_LORA), lambda i: (i, 0, 0)),
                  pl.BlockSpec((1, past, QK_ROPE), lambda i: (i, 0, 0))],
        out_specs=pl.BlockSpec((1, ts, MLA_HEADS * KV_LORA), lambda i: (i, 0, 0)),
        out_shape=jax.ShapeDtypeStruct((b, ts, MLA_HEADS * KV_LORA), BF16),
        compiler_params=pltpu.CompilerParams(dimension_semantics=("parallel",),
                                             vmem_limit_bytes=VMEM_LIMIT),
        name="mla_sample",
    )(qcat, kcat, cache_kv, cache_pe)


def _hgrn_body(zh_ref, s0_ref, lb_ref, gain_ref, ob_ref, sout_ref, st_ref, g_scr, q_scr, k_scr, v_scr, o_scr,
               *, cs, pad_front):
    c = pl.program_id(1)
    nc = pl.num_programs(1)

    @pl.when(c == 0)
    def _():
        for hd in range(HG_HEADS):
            st_ref[hd] = s0_ref[0, hd].T

    hq = zh_ref[0, :, :HG_KW]
    hf = zh_ref[0, :, HG_KW:2 * HG_KW]
    v = zh_ref[0, :, 2 * HG_KW:2 * HG_KW + HG_WIDTH]
    hg = zh_ref[0, :, 2 * HG_KW + HG_WIDTH:]
    lb = lb_ref[...]
    e = jnp.exp(-jnp.abs(hf))
    r = 1.0 / (1.0 + e)
    pos = hf >= 0.0
    f = lb + (1.0 - lb) * jnp.where(pos, r, e * r)
    gl = jnp.log(f)
    k = (1.0 - lb) * jnp.where(pos, e * r, r)
    q = _silu(hq)
    if pad_front:
        live = (c * cs + lax.broadcasted_iota(jnp.int32, (cs, 1), 0)) >= pad_front
        gl = jnp.where(live, gl, 0.0)
        k = jnp.where(live, k, 0.0)
        q = jnp.where(live, q, 0.0)
    t_idx = lax.broadcasted_iota(jnp.int32, (cs, cs), 0)
    s_idx = lax.broadcasted_iota(jnp.int32, (cs, cs), 1)
    causal = t_idx >= s_idx
    g = jnp.dot(causal.astype(F32), gl, precision=lax.Precision.HIGHEST, preferred_element_type=F32)
    g_tot = g[cs - 1:cs, :]
    qt = q * jnp.exp(g)
    khat = k * jnp.exp(g_tot - g)
    in_range = jnp.max(-g_tot) <= HGRN_FAST_RANGE

    @pl.when(in_range)
    def _():
        kt = k * jnp.exp(-g)
        for hd in range(HG_HEADS):
            sl = slice(HG_DK * hd, HG_DK * (hd + 1))
            a = jnp.where(causal, _mm_nt(qt[:, sl], kt[:, sl]), 0.0)
            o_scr[:, sl] = _mm(a, v[:, sl])

    @pl.when(jnp.logical_not(in_range))
    def _():
        g_scr[...] = g
        q_scr[...] = q
        k_scr[...] = k
        v_scr[...] = v
        o_scr[...] = jnp.zeros_like(o_scr)
        t_col = lax.broadcasted_iota(jnp.int32, (cs, 1), 0)

        def cols(i, carry):
            base = pl.multiple_of(i * 8, 8)
            g8, k8, v8 = (r[pl.ds(base, 8), :] for r in (g_scr, k_scr, v_scr))
            for j in range(8):
                dec = jnp.exp(jnp.where(t_col >= base + j, g_scr[...] - g8[j:j + 1], -jnp.inf))
                pr = q_scr[...] * dec * k8[j:j + 1]
                for hd in range(HG_HEADS):
                    sl = slice(HG_DK * hd, HG_DK * (hd + 1))
                    o_scr[:, sl] += jnp.sum(pr[:, sl], axis=-1, keepdims=True) * v8[j:j + 1, sl]
            return carry

        lax.fori_loop(0, cs // 8, cols, 0)

    gain = gain_ref[...]
    for hd in range(HG_HEADS):
        sl = slice(HG_DK * hd, HG_DK * (hd + 1))
        st = st_ref[hd]
        o = o_scr[:, sl] + _mm_nt(qt[:, sl], st)
        st_ref[hd] = jnp.exp(g_tot[:, sl]) * st + _mm_tn(v[:, sl], khat[:, sl])
        ob_ref[0, :, sl] = (_rms(o, gain) * _silu(hg[:, sl])).astype(BF16)

    @pl.when(c == nc - 1)
    def _():
        for hd in range(HG_HEADS):
            sout_ref[0, hd] = st_ref[hd].T


def _hgrn(zh, s0, lb, gain, *, cs, pad_front):
    b, t, zw = zh.shape
    assert t % cs == 0
    return pl.pallas_call(
        functools.partial(_hgrn_body, cs=cs, pad_front=pad_front),
        grid=(b, t // cs),
        in_specs=[pl.BlockSpec((1, cs, zw), lambda i, c: (i, c, 0)),
                  pl.BlockSpec((1, HG_HEADS, HG_DK, HG_DV), lambda i, c: (i, 0, 0, 0)),
                  _const_spec(lb.shape), _const_spec(gain.shape)],
        out_specs=(pl.BlockSpec((1, cs, HG_WIDTH), lambda i, c: (i, c, 0)),
                   pl.BlockSpec((1, HG_HEADS, HG_DK, HG_DV), lambda i, c: (i, 0, 0, 0))),
        out_shape=(jax.ShapeDtypeStruct((b, t, HG_WIDTH), BF16),
                   jax.ShapeDtypeStruct((b, HG_HEADS, HG_DK, HG_DV), F32)),
        scratch_shapes=[pltpu.VMEM((HG_HEADS, HG_DV, HG_DK), F32)]
        + [pltpu.VMEM((cs, HG_KW), F32) for _ in range(5)],
        compiler_params=pltpu.CompilerParams(dimension_semantics=("parallel", "arbitrary"),
                                             vmem_limit_bytes=VMEM_LIMIT),
        name="hgrn",
    )(zh, s0, lb, gain)


def _out_ffn_body(x_ref, ol_ref, ob_ref, buf_ref, wuv_ref, wout_ref, npost_ref, nfpre_ref, wup_ref, cw_ref,
                  cb_ref, wdn_ref, nfpost_ref, fin_ref, xo_ref, bufo_ref, a_scr, *, tm, pad_front, last):
    t = pl.program_id(1)
    nt = pl.num_programs(1)
    d_ff = cw_ref.shape[1]
    keep = CONV_W - 1

    @pl.when(t == 0)
    def _():
        a_scr[0:8, :] = jnp.zeros((8, d_ff), F32)
        a_scr[8 - keep:8, :] = buf_ref[0]

    ol = ol_ref[0]
    o_a = [_mm(ol[:, 2 * KV_LORA * j:2 * KV_LORA * (j + 1)], wuv_ref[j]) for j in range(MLA_HEADS // 2)]
    mixed = jnp.concatenate([o.astype(BF16) for o in o_a] + [ob_ref[0]], axis=-1)
    x1 = x_ref[0] + _rms(_mm(mixed, wout_ref[...]), npost_ref[...])
    u = _mm(_rms(x1, nfpre_ref[...]), wup_ref[...])
    a = u[:, :d_ff]
    if pad_front:
        live = (t * tm + lax.broadcasted_iota(jnp.int32, (tm, 1), 0)) >= pad_front
        a = jnp.where(live, a, 0.0)
    a_scr[8:8 + tm, :] = a
    cw = cw_ref[...]
    conv = cb_ref[...] + a * cw[keep:keep + 1]
    for j in range(keep):
        conv = conv + a_scr[8 - keep + j:8 - keep + j + tm, :] * cw[j:j + 1]
    f = _mm(_silu(conv) * u[:, d_ff:], wdn_ref[...])
    x2 = x1 + _rms(f, nfpost_ref[...])
    xo_ref[0] = _rms(x2, fin_ref[...]) if last else x2
    a_scr[0:8, :] = a_scr[tm:tm + 8, :]

    @pl.when(t == nt - 1)
    def _():
        bufo_ref[0] = a_scr[8 - keep:8, :]


def _out_ffn(x, ol, ob, buf, w, final_norm, *, pad_front, last):
    b, t, d = x.shape
    d_ff = buf.shape[-1]
    tm = _row_tile(t, 192)
    assert tm >= 8
    row = lambda width: pl.BlockSpec((1, tm, width), lambda i, j: (i, j, 0))
    per_b = pl.BlockSpec((1, CONV_W - 1, d_ff), lambda i, j: (i, 0, 0))
    weights = (w['w_uv'], w['w_out'], w['norm_mix_post'], w['norm_ffn_pre'], w['w_ffn_up'], w['ffn_conv_w'],
               w['ffn_conv_b'], w['w_ffn_down'], w['norm_ffn_post'], final_norm)
    return pl.pallas_call(
        functools.partial(_out_ffn_body, tm=tm, pad_front=pad_front, last=last),
        grid=(b, t // tm),
        in_specs=[row(d), row(MLA_HEADS * KV_LORA), row(HG_WIDTH), per_b] + [_const_spec(a.shape) for a in weights],
        out_specs=(row(d), per_b),
        out_shape=(jax.ShapeDtypeStruct((b, t, d), F32), jax.ShapeDtypeStruct((b, CONV_W - 1, d_ff), F32)),
        scratch_shapes=[pltpu.VMEM((tm + 8, d_ff), F32)],
        compiler_params=pltpu.CompilerParams(dimension_semantics=("parallel", "arbitrary"),
                                             vmem_limit_bytes=VMEM_LIMIT),
        name="out_ffn",
    )(x, ol, ob, buf, *weights)


def _rope_tables(pos):
    half = QK_ROPE // 2
    inv = ROPE_THETA ** (-jnp.arange(half, dtype=F32) / half)
    ang = pos.astype(F32)[:, None] * inv
    cos, sin = jnp.cos(ang), jnp.sin(ang)
    zero = jnp.zeros_like(cos)
    pad = jnp.zeros((pos.shape[0], LANES - QK_ROPE), F32)
    return (jnp.concatenate([cos, cos, pad], axis=1),
            jnp.concatenate([-sin, zero, pad], axis=1),
            jnp.concatenate([zero, sin, pad], axis=1))


def _layer_weights(l, w_in, norm_mix_pre, norm_mix_post, q_norm, kv_norm, w_uq, w_uk, w_uv, hg_out_norm, w_out,
                   norm_ffn_pre, norm_ffn_post, w_ffn_up, ffn_conv_w, ffn_conv_b, w_ffn_down):
    d = w_in.shape[1]
    head_w = Q_LORA + KV_LORA + QK_ROPE
    w_in_r = jnp.concatenate([w_in[l][:, :head_w], jnp.zeros((d, LANES - QK_ROPE), F32), w_in[l][:, head_w:]], axis=1)
    uq = w_uq[l].reshape(Q_LORA, MLA_HEADS, QK_NOPE + QK_ROPE)
    uq_rope = jnp.pad(uq[:, :, QK_NOPE:], ((0, 0), (0, 0), (0, LANES - QK_ROPE)))
    w_uq_r = jnp.concatenate([uq[:, :, :QK_NOPE].reshape(Q_LORA, -1), uq_rope.reshape(Q_LORA, -1)], axis=1)
    uk = jnp.transpose(w_uk[l], (1, 2, 0))
    uk_pair = jnp.zeros((MLA_HEADS // 2, 2 * QK_NOPE, 2 * KV_LORA), F32)
    uk_pair = uk_pair.at[:, :QK_NOPE, :KV_LORA].set(uk[0::2]).at[:, QK_NOPE:, KV_LORA:].set(uk[1::2])
    uv = jnp.transpose(w_uv[l], (1, 0, 2))
    uv_pair = jnp.zeros((MLA_HEADS // 2, 2 * KV_LORA, 2 * V_HEAD), F32)
    uv_pair = uv_pair.at[:, :KV_LORA, :V_HEAD].set(uv[0::2]).at[:, KV_LORA:, V_HEAD:].set(uv[1::2])
    row = lambda a: a.reshape(1, -1)
    return {
        'w_in': w_in_r.astype(BF16), 'norm_mix_pre': row(norm_mix_pre[l]), 'norm_mix_post': row(norm_mix_post[l]),
        'q_norm': row(q_norm[l]), 'kv_norm': row(kv_norm[l]), 'w_uq': w_uq_r.astype(BF16),
        'w_uk': uk_pair.astype(BF16), 'w_uv': uv_pair.astype(BF16),
        'hg_gain': row(hg_out_norm[l]), 'w_out': w_out[l].astype(BF16),
        'norm_ffn_pre': row(norm_ffn_pre[l]), 'norm_ffn_post': row(norm_ffn_post[l]),
        'w_ffn_up': w_ffn_up[l].astype(BF16), 'ffn_conv_w': ffn_conv_w[l], 'ffn_conv_b': row(ffn_conv_b[l]),
        'w_ffn_down': w_ffn_down[l].astype(BF16),
    }


def kernel(x_prompt, x_sample, cache_kv_latent, cache_k_rope, state_hgrn, state_ffn_conv, meta_tokens, w_in, norm_mix_pre, norm_mix_post, q_norm, kv_norm, w_uq, w_uk, w_uv, hg_lower_bounds, hg_out_norm, w_out, norm_ffn_pre, norm_ffn_post, w_ffn_up, ffn_conv_w, ffn_conv_b, w_ffn_down, final_norm):
    depth = w_in.shape[0]
    b, seq, d = x_prompt.shape
    bs, ts, _ = x_sample.shape
    past = cache_kv_latent.shape[2]
    d_ff = ffn_conv_w.shape[-1]
    lp = PAD_FRONT + N_META + seq

    lb_sm = jax.nn.softmax(hg_lower_bounds.astype(F32), axis=0)
    lb_all = jnp.cumsum(lb_sm, axis=0) - lb_sm[0]

    meta = jnp.broadcast_to(meta_tokens[None], (b, N_META, d))
    xp = jnp.concatenate([jnp.zeros((b, PAD_FRONT, d), F32), meta, x_prompt], axis=1)
    xs = x_sample
    tabs_p = _rope_tables(jnp.arange(lp, dtype=jnp.int32) - CHUNK)
    tabs_s = tuple(jnp.tile(a, (bs, 1)) for a in _rope_tables(past + jnp.arange(ts, dtype=jnp.int32)))
    fin = final_norm.reshape(1, -1)
    s0_p = jnp.zeros((b, HG_HEADS, HG_DK, HG_DV), F32)
    buf0_p = jnp.zeros((b, CONV_W - 1, d_ff), F32)

    outs = [[] for _ in range(8)]
    for l in range(depth):
        w = _layer_weights(l, w_in, norm_mix_pre, norm_mix_post, q_norm, kv_norm, w_uq, w_uk, w_uv, hg_out_norm,
                           w_out, norm_ffn_pre, norm_ffn_post, w_ffn_up, ffn_conv_w, ffn_conv_b, w_ffn_down)
        lb = lb_all[l].reshape(1, -1)
        gain = w['hg_gain']
        last = l == depth - 1

        qcat, kcat, kv, pe, zh = _in_proj(xp, tabs_p, w)
        ol = _mla_prompt(qcat, kcat)
        ob, s_new = _hgrn(zh, s0_p, lb, gain, cs=CHUNK, pad_front=PAD_FRONT)
        xp, buf = _out_ffn(xp, ol, ob, buf0_p, w, fin, pad_front=PAD_FRONT, last=last)
        for lst, a in zip(outs[:4], (kv[:, PAD_FRONT:], pe[:, PAD_FRONT:], s_new, buf)):
            lst.append(a)

        qcat, kcat, kv, pe, zh = _in_proj(xs.reshape(1, bs * ts, d), tabs_s, w)
        qcat = jnp.swapaxes(qcat.reshape(MLA_HEADS, bs, ts, QCAT), 0, 1)
        kcat, kv, pe, zh = (a.reshape(bs, ts, -1) for a in (kcat, kv, pe, zh))
        ol = _mla_sample(qcat, kcat, cache_kv_latent[l], cache_k_rope[l])
        ob, s_new = _hgrn(zh, state_hgrn[l], lb, gain, cs=ts, pad_front=0)
        xs, buf = _out_ffn(xs, ol, ob, state_ffn_conv[l], w, fin, pad_front=0, last=last)
        for lst, a in zip(outs[4:], (kv, pe, s_new, buf)):
            lst.append(a)

    return (xp[:, PAD_FRONT + N_META:], xs) + tuple(jnp.stack(o) for o in outs)
```

```python
import functools

import jax
import jax.numpy as jnp
from jax import lax
from jax.experimental import pallas as pl
from jax.experimental.pallas import tpu as pltpu

F32 = jnp.float32
BF16 = jnp.bfloat16

CHUNK = 64
N_META = 16
PAD_FRONT = CHUNK - N_META
EPS = 1e-6
MLA_HEADS = 8
QK_NOPE = 64
QK_ROPE = 32
V_HEAD = 64
Q_LORA = 256
KV_LORA = 128
ROPE_THETA = 10000.0
HG_HEADS = 4
HG_DK = 128
HG_DV = 128
CONV_W = 3

HG_KW = HG_HEADS * HG_DK
HG_WIDTH = HG_HEADS * HG_DV
MLA_WIDTH = MLA_HEADS * V_HEAD
LANES = 128
KEY_BLOCK = 256
QCAT = KV_LORA + LANES
VT_ROWS = KV_LORA + 16
IN_PROJ_ROWS = 384
Q_SCALE = (QK_NOPE + QK_ROPE) ** -0.5 * 1.4426950408889634
HGRN_FAST_RANGE = 80.0
VMEM_LIMIT = 56 * 1024 * 1024


def _row_tile(n, target):
    if n <= target:
        return n
    best = None
    for t in range(16, target + 1, 16):
        if n % t == 0:
            best = t
    assert best is not None, (n, target)
    return best


def _const_spec(shape):
    nd = len(shape)
    return pl.BlockSpec(shape, lambda *_: (0,) * nd, pipeline_mode=pl.Buffered(1))


def _rms(x, w):
    return x * lax.rsqrt(jnp.mean(x * x, axis=-1, keepdims=True) + EPS) * w


def _silu(x):
    return x * jax.nn.sigmoid(x)


def _mm(a, b):
    return jnp.dot(a.astype(BF16), b.astype(BF16), preferred_element_type=F32)


def _mm_nt(a, b):
    return lax.dot_general(a.astype(BF16), b.astype(BF16), (((1,), (1,)), ((), ())),
                           preferred_element_type=F32)


def _mm_tn(a, b):
    return lax.dot_general(a.astype(BF16), b.astype(BF16), (((0,), (0,)), ((), ())),
                           preferred_element_type=F32)


def _in_proj_body(x_ref, cos_ref, sn_ref, sp_ref, npre_ref, win_ref, qn_ref, kvn_ref, wuq_ref, wuk_ref,
                  qcat_ref, kcat_ref, kvt_ref, kv_ref, pe_ref, zh_ref, *, tm, n_valid):
    cos, sn, sp = cos_ref[...], sn_ref[...], sp_ref[...]

    def rope(v):
        return v * cos + pltpu.roll(v, LANES - QK_ROPE // 2, 1) * sn + pltpu.roll(v, QK_ROPE // 2, 1) * sp

    live = (pl.program_id(1) * tm + lax.broadcasted_iota(jnp.int32, (tm, 1), 0)) < n_valid
    h = _rms(jnp.where(live, x_ref[0], 0.0), npre_ref[...])
    z = _mm(h, win_ref[...])
    zh_ref[0] = z[:, Q_LORA + KV_LORA + LANES:]
    kv_lat = _rms(z[:, Q_LORA:Q_LORA + KV_LORA], kvn_ref[...])
    k_pe = rope(z[:, Q_LORA + KV_LORA:Q_LORA + KV_LORA + LANES])
    kv_ref[0] = kv_lat
    pe_ref[0] = k_pe[:, :QK_ROPE]
    kcat_ref[0, :, :KV_LORA] = kv_lat.astype(BF16)
    kcat_ref[0, :, KV_LORA:] = k_pe.astype(BF16)
    kvt_ref[0, :KV_LORA, :] = kv_lat.T.astype(BF16)
    kvt_ref[0, KV_LORA:, :] = jnp.ones((VT_ROWS - KV_LORA, tm), BF16)

    cqn = _rms(z[:, :Q_LORA], qn_ref[...])
    q = _mm(cqn, wuq_ref[...]) * Q_SCALE
    nope_w = MLA_HEADS * QK_NOPE
    for j in range(MLA_HEADS // 2):
        ql = _mm(q[:, 2 * QK_NOPE * j:2 * QK_NOPE * (j + 1)], wuk_ref[j])
        for hh in range(2):
            hd = 2 * j + hh
            qcat_ref[0, hd, :, :KV_LORA] = ql[:, KV_LORA * hh:KV_LORA * (hh + 1)].astype(BF16)
            qcat_ref[0, hd, :, KV_LORA:] = rope(q[:, nope_w + LANES * hd:nope_w + LANES * (hd + 1)]).astype(BF16)


def _in_proj(x, tabs, w, *, n_valid):
    bx, t, d = x.shape
    tm = _row_tile(t, IN_PROJ_ROWS)
    assert tm % LANES == 0 or tm == t
    nt = t // tm
    row = lambda shape: pl.BlockSpec((1, tm) + shape, lambda b, i: (b, i) + (0,) * len(shape))
    tab = pl.BlockSpec((tm, LANES), lambda b, i: (i, 0))
    zw = 2 * HG_KW + 2 * HG_WIDTH
    out_shape = (
        jax.ShapeDtypeStruct((bx, MLA_HEADS, t, QCAT), BF16),
        jax.ShapeDtypeStruct((bx, t, QCAT), BF16),
        jax.ShapeDtypeStruct((bx, VT_ROWS, t), BF16),
        jax.ShapeDtypeStruct((bx, t, KV_LORA), F32),
        jax.ShapeDtypeStruct((bx, t, QK_ROPE), F32),
        jax.ShapeDtypeStruct((bx, t, zw), F32),
    )
    out_specs = (
        pl.BlockSpec((1, MLA_HEADS, tm, QCAT), lambda b, i: (b, 0, i, 0)),
        row((QCAT,)),
        pl.BlockSpec((1, VT_ROWS, tm), lambda b, i: (b, 0, i)),
        row((KV_LORA,)), row((QK_ROPE,)), row((zw,)),
    )
    weights = (w['norm_mix_pre'], w['w_in'], w['q_norm'], w['kv_norm'], w['w_uq'], w['w_uk'])
    return pl.pallas_call(
        functools.partial(_in_proj_body, tm=tm, n_valid=n_valid),
        grid=(bx, nt),
        in_specs=[row((d,)), tab, tab, tab] + [_const_spec(a.shape) for a in weights],
        out_specs=out_specs,
        out_shape=out_shape,
        compiler_params=pltpu.CompilerParams(dimension_semantics=("parallel", "parallel"),
                                             vmem_limit_bytes=VMEM_LIMIT),
        name="in_proj",
    )(x, *tabs, *weights)


def _mla_prompt_body(q_ref, k_ref, vt_ref, o_ref, m_scr, acc_scr, bias_scr):
    j = pl.program_id(1)
    base = j * KEY_BLOCK
    m_scr[...] = jnp.full(m_scr.shape, -jnp.inf, F32)
    acc_scr[...] = jnp.zeros(acc_scr.shape, F32)
    row = lax.broadcasted_iota(jnp.int32, (KEY_BLOCK, KEY_BLOCK), 0)
    col = lax.broadcasted_iota(jnp.int32, (KEY_BLOCK, KEY_BLOCK), 1)
    real = row >= PAD_FRONT
    bias_scr[0] = jnp.where(real, 0.0, -jnp.inf)
    visible = ((base + row) // CHUNK) <= ((base + col) // CHUNK)
    bias_scr[1] = jnp.where(visible, jnp.where(base + row >= PAD_FRONT, 0.0, -jnp.inf), -jnp.inf)

    def block(start, bias_idx):
        kblk = k_ref[0, pl.ds(start, KEY_BLOCK), :]
        vt = vt_ref[0, :, pl.ds(start, KEY_BLOCK)]
        m_all, acc_all = [], []
        s_next = _mm_nt(kblk, q_ref[0, 0])
        for hd in range(MLA_HEADS):
            s = s_next
            if hd + 1 < MLA_HEADS:
                s_next = _mm_nt(kblk, q_ref[0, hd + 1])
            if bias_idx is not None:
                s = s + bias_scr[bias_idx]
            m_old = m_scr[hd]
            m_new = jnp.maximum(m_old, jnp.max(s, axis=0, keepdims=True))
            p = jnp.exp2(s - m_new)
            acc_all.append(jnp.exp2(m_old - m_new) * acc_scr[hd] + _mm(vt, p))
            m_all.append(m_new)
        acc_scr[...] = jnp.stack(acc_all)
        m_scr[...] = jnp.stack(m_all)

    @pl.when(j > 0)
    def _():
        block(0, 0)

    def interior(kb, carry):
        block(pl.multiple_of(kb * KEY_BLOCK, KEY_BLOCK), None)
        return carry

    lax.fori_loop(1, j, interior, 0)
    block(pl.multiple_of(base, KEY_BLOCK), 1)

    for hd in range(MLA_HEADS):
        acc = acc_scr[hd]
        o_t = acc[:KV_LORA] * (1.0 / acc[KV_LORA:KV_LORA + 1])
        o_ref[0, :, KV_LORA * hd:KV_LORA * (hd + 1)] = o_t.T.astype(BF16)


def _mla_prompt(qcat, kcat, kvt, *, lp):
    b, _, t, _ = qcat.shape
    assert t % KEY_BLOCK == 0 and lp <= t
    return pl.pallas_call(
        _mla_prompt_body,
        grid=(b, pl.cdiv(lp, KEY_BLOCK)),
        in_specs=[pl.BlockSpec((1, MLA_HEADS, KEY_BLOCK, QCAT), lambda i, j: (i, 0, j, 0)),
                  pl.BlockSpec((1, t, QCAT), lambda i, j: (i, 0, 0)),
                  pl.BlockSpec((1, VT_ROWS, t), lambda i, j: (i, 0, 0))],
        out_specs=pl.BlockSpec((1, KEY_BLOCK, MLA_HEADS * KV_LORA), lambda i, j: (i, j, 0)),
        out_shape=jax.ShapeDtypeStruct((b, t, MLA_HEADS * KV_LORA), BF16),
        scratch_shapes=[pltpu.VMEM((MLA_HEADS, 1, KEY_BLOCK), F32),
                        pltpu.VMEM((MLA_HEADS, VT_ROWS, KEY_BLOCK), F32),
                        pltpu.VMEM((2, KEY_BLOCK, KEY_BLOCK), F32)],
        compiler_params=pltpu.CompilerParams(dimension_semantics=("parallel", "parallel"),
                                             vmem_limit_bytes=VMEM_LIMIT),
        name="mla_prompt",
    )(qcat, kcat, kvt)


def _mla_sample_body(q_ref, k_ref, ckv_ref, cpe_ref, o_ref, *, ts):
    rows = MLA_HEADS * ts
    q = q_ref[0].reshape(rows, QCAT)
    ckv = ckv_ref[0].astype(BF16)
    knew = k_ref[0]
    s_c = _mm_nt(q[:, :KV_LORA], ckv) + _mm_nt(q[:, KV_LORA:KV_LORA + QK_ROPE], cpe_ref[0])
    s_n = _mm_nt(q, knew)
    m = jnp.maximum(jnp.max(s_c, axis=-1, keepdims=True), jnp.max(s_n, axis=-1, keepdims=True))
    p_c = jnp.exp2(s_c - m)
    p_n = jnp.exp2(s_n - m)
    l = jnp.sum(p_c, axis=-1, keepdims=True) + jnp.sum(p_n, axis=-1, keepdims=True)
    o = (_mm(p_c, ckv) + _mm(p_n, knew[:, :KV_LORA])) / l
    for hd in range(MLA_HEADS):
        o_ref[0, :, KV_LORA * hd:KV_LORA * (hd + 1)] = o[ts * hd:ts * (hd + 1)].astype(BF16)


def _mla_sample(qcat, kcat, cache_kv, cache_pe):
    b, _, ts, _ = qcat.shape
    past = cache_kv.shape[1]
    return pl.pallas_call(
        functools.partial(_mla_sample_body, ts=ts),
        grid=(b,),
        in_specs=[pl.BlockSpec((1, MLA_HEADS, ts, QCAT), lambda i: (i, 0, 0, 0)),
                  pl.BlockSpec((1, ts, QCAT), lambda i: (i, 0, 0)),
                  pl.BlockSpec((1, past, KV_LORA), lambda i: (i, 0, 0)),
                  pl.BlockSpec((1, past, QK_ROPE), lambda i: (i, 0, 0))],
        out_specs=pl.BlockSpec((1, ts, MLA_HEADS * KV_LORA), lambda i: (i, 0, 0)),
        out_shape=jax.ShapeDtypeStruct((b, ts, MLA_HEADS * KV_LORA), BF16),
        compiler_params=pltpu.CompilerParams(dimension_semantics=("parallel",),
                                             vmem_limit_bytes=VMEM_LIMIT),
        name="mla_sample",
    )(qcat, kcat, cache_kv, cache_pe)


def _hgrn_body(zh_ref, s0_ref, lb_ref, gain_ref, ob_ref, sout_ref, st_ref, g_scr, q_scr, k_scr, v_scr, o_scr,
               *, cs, pad_front):
    c = pl.program_id(1)
    nc = pl.num_programs(1)

    @pl.when(c == 0)
    def _():
        for hd in range(HG_HEADS):
            st_ref[hd] = s0_ref[0, hd].T

    hq = zh_ref[0, :, :HG_KW]
    hf = zh_ref[0, :, HG_KW:2 * HG_KW]
    v = zh_ref[0, :, 2 * HG_KW:2 * HG_KW + HG_WIDTH]
    hg = zh_ref[0, :, 2 * HG_KW + HG_WIDTH:]
    lb = lb_ref[...]
    e = jnp.exp(-jnp.abs(hf))
    r = 1.0 / (1.0 + e)
    pos = hf >= 0.0
    f = lb + (1.0 - lb) * jnp.where(pos, r, e * r)
    gl = jnp.log(f)
    k = (1.0 - lb) * jnp.where(pos, e * r, r)
    q = _silu(hq)
    if pad_front:
        live = (c * cs + lax.broadcasted_iota(jnp.int32, (cs, 1), 0)) >= pad_front
        gl = jnp.where(live, gl, 0.0)
        k = jnp.where(live, k, 0.0)
        q = jnp.where(live, q, 0.0)
    t_idx = lax.broadcasted_iota(jnp.int32, (cs, cs), 0)
    s_idx = lax.broadcasted_iota(jnp.int32, (cs, cs), 1)
    causal = t_idx >= s_idx
    g = jnp.dot(causal.astype(F32), gl, precision=lax.Precision.HIGHEST, preferred_element_type=F32)
    g_tot = g[cs - 1:cs, :]
    qt = q * jnp.exp(g)
    khat = k * jnp.exp(g_tot - g)
    in_range = jnp.max(-g_tot) <= HGRN_FAST_RANGE

    @pl.when(in_range)
    def _():
        kt = k * jnp.exp(-g)
        for hd in range(HG_HEADS):
            sl = slice(HG_DK * hd, HG_DK * (hd + 1))
            a = jnp.where(causal, _mm_nt(qt[:, sl], kt[:, sl]), 0.0)
            o_scr[:, sl] = _mm(a, v[:, sl])

    @pl.when(jnp.logical_not(in_range))
    def _():
        g_scr[...] = g
        q_scr[...] = q
        k_scr[...] = k
        v_scr[...] = v
        o_scr[...] = jnp.zeros_like(o_scr)
        t_col = lax.broadcasted_iota(jnp.int32, (cs, 1), 0)

        def cols(i, carry):
            base = pl.multiple_of(i * 8, 8)
            g8, k8, v8 = (r[pl.ds(base, 8), :] for r in (g_scr, k_scr, v_scr))
            for j in range(8):
                dec = jnp.exp(jnp.where(t_col >= base + j, g_scr[...] - g8[j:j + 1], -jnp.inf))
                pr = q_scr[...] * dec * k8[j:j + 1]
                for hd in range(HG_HEADS):
                    sl = slice(HG_DK * hd, HG_DK * (hd + 1))
                    o_scr[:, sl] += jnp.sum(pr[:, sl], axis=-1, keepdims=True) * v8[j:j + 1, sl]
            return carry

        lax.fori_loop(0, cs // 8, cols, 0)

    gain = gain_ref[...]
    for hd in range(HG_HEADS):
        sl = slice(HG_DK * hd, HG_DK * (hd + 1))
        st = st_ref[hd]
        o = o_scr[:, sl] + _mm_nt(qt[:, sl], st)
        st_ref[hd] = jnp.exp(g_tot[:, sl]) * st + _mm_tn(v[:, sl], khat[:, sl])
        ob_ref[0, :, sl] = (_rms(o, gain) * _silu(hg[:, sl])).astype(BF16)

    @pl.when(c == nc - 1)
    def _():
        for hd in range(HG_HEADS):
            sout_ref[0, hd] = st_ref[hd].T


def _hgrn(zh, s0, lb, gain, *, n_rows, cs, pad_front):
    b, t, zw = zh.shape
    assert n_rows % cs == 0 and n_rows <= t
    return pl.pallas_call(
        functools.partial(_hgrn_body, cs=cs, pad_front=pad_front),
        grid=(b, n_rows // cs),
        in_specs=[pl.BlockSpec((1, cs, zw), lambda i, c: (i, c, 0)),
                  pl.BlockSpec((1, HG_HEADS, HG_DK, HG_DV), lambda i, c: (i, 0, 0, 0)),
                  _const_spec(lb.shape), _const_spec(gain.shape)],
        out_specs=(pl.BlockSpec((1, cs, HG_WIDTH), lambda i, c: (i, c, 0)),
                   pl.BlockSpec((1, HG_HEADS, HG_DK, HG_DV), lambda i, c: (i, 0, 0, 0))),
        out_shape=(jax.ShapeDtypeStruct((b, t, HG_WIDTH), BF16),
                   jax.ShapeDtypeStruct((b, HG_HEADS, HG_DK, HG_DV), F32)),
        scratch_shapes=[pltpu.VMEM((HG_HEADS, HG_DV, HG_DK), F32)]
        + [pltpu.VMEM((cs, HG_KW), F32) for _ in range(5)],
        compiler_params=pltpu.CompilerParams(dimension_semantics=("parallel", "arbitrary"),
                                             vmem_limit_bytes=VMEM_LIMIT),
        name="hgrn",
    )(zh, s0, lb, gain)


def _out_ffn_body(x_ref, ol_ref, ob_ref, buf_ref, wuv_ref, wout_ref, npost_ref, nfpre_ref, wup_ref, cw_ref,
                  cb_ref, wdn_ref, nfpost_ref, fin_ref, xo_ref, bufo_ref, a_scr, *, tm, pad_front, last):
    t = pl.program_id(1)
    nt = pl.num_programs(1)
    d_ff = cw_ref.shape[1]
    keep = CONV_W - 1

    @pl.when(t == 0)
    def _():
        a_scr[0:8, :] = jnp.zeros((8, d_ff), F32)
        a_scr[8 - keep:8, :] = buf_ref[0]

    ol = ol_ref[0]
    o_a = [_mm(ol[:, 2 * KV_LORA * j:2 * KV_LORA * (j + 1)], wuv_ref[j]) for j in range(MLA_HEADS // 2)]
    mixed = jnp.concatenate([o.astype(BF16) for o in o_a] + [ob_ref[0]], axis=-1)
    x1 = x_ref[0] + _rms(_mm(mixed, wout_ref[...]), npost_ref[...])
    u = _mm(_rms(x1, nfpre_ref[...]), wup_ref[...])
    a = u[:, :d_ff]
    if pad_front:
        live = (t * tm + lax.broadcasted_iota(jnp.int32, (tm, 1), 0)) >= pad_front
        a = jnp.where(live, a, 0.0)
    a_scr[8:8 + tm, :] = a
    cw = cw_ref[...]
    conv = cb_ref[...] + a * cw[keep:keep + 1]
    for j in range(keep):
        conv = conv + a_scr[8 - keep + j:8 - keep + j + tm, :] * cw[j:j + 1]
    f = _mm(_silu(conv) * u[:, d_ff:], wdn_ref[...])
    x2 = x1 + _rms(f, nfpost_ref[...])
    xo_ref[0] = _rms(x2, fin_ref[...]) if last else x2
    a_scr[0:8, :] = a_scr[tm:tm + 8, :]

    @pl.when(t == nt - 1)
    def _():
        bufo_ref[0] = a_scr[8 - keep:8, :]


def _out_ffn(x, ol, ob, buf, w, final_norm, *, n_rows, pad_front, last):
    b, t, d = x.shape
    d_ff = buf.shape[-1]
    tm = _row_tile(n_rows, 192)
    assert tm >= 8 and n_rows <= t
    row = lambda width: pl.BlockSpec((1, tm, width), lambda i, j: (i, j, 0))
    per_b = pl.BlockSpec((1, CONV_W - 1, d_ff), lambda i, j: (i, 0, 0))
    weights = (w['w_uv'], w['w_out'], w['norm_mix_post'], w['norm_ffn_pre'], w['w_ffn_up'], w['ffn_conv_w'],
               w['ffn_conv_b'], w['w_ffn_down'], w['norm_ffn_post'], final_norm)
    return pl.pallas_call(
        functools.partial(_out_ffn_body, tm=tm, pad_front=pad_front, last=last),
        grid=(b, n_rows // tm),
        in_specs=[row(d), row(MLA_HEADS * KV_LORA), row(HG_WIDTH), per_b] + [_const_spec(a.shape) for a in weights],
        out_specs=(row(d), per_b),
        out_shape=(jax.ShapeDtypeStruct((b, t, d), F32), jax.ShapeDtypeStruct((b, CONV_W - 1, d_ff), F32)),
        scratch_shapes=[pltpu.VMEM((tm + 8, d_ff), F32)],
        compiler_params=pltpu.CompilerParams(dimension_semantics=("parallel", "arbitrary"),
                                             vmem_limit_bytes=VMEM_LIMIT),
        name="out_ffn",
    )(x, ol, ob, buf, *weights)


def _rope_tables(pos):
    half = QK_ROPE // 2
    inv = ROPE_THETA ** (-jnp.arange(half, dtype=F32) / half)
    ang = pos.astype(F32)[:, None] * inv
    cos, sin = jnp.cos(ang), jnp.sin(ang)
    zero = jnp.zeros_like(cos)
    pad = jnp.zeros((pos.shape[0], LANES - QK_ROPE), F32)
    return (jnp.concatenate([cos, cos, pad], axis=1),
            jnp.concatenate([-sin, zero, pad], axis=1),
            jnp.concatenate([zero, sin, pad], axis=1))


def _layer_weights(l, w_in, norm_mix_pre, norm_mix_post, q_norm, kv_norm, w_uq, w_uk, w_uv, hg_out_norm, w_out,
                   norm_ffn_pre, norm_ffn_post, w_ffn_up, ffn_conv_w, ffn_conv_b, w_ffn_down):
    d = w_in.shape[1]
    head_w = Q_LORA + KV_LORA + QK_ROPE
    w_in_r = jnp.concatenate([w_in[l][:, :head_w], jnp.zeros((d, LANES - QK_ROPE), F32), w_in[l][:, head_w:]], axis=1)
    uq = w_uq[l].reshape(Q_LORA, MLA_HEADS, QK_NOPE + QK_ROPE)
    uq_rope = jnp.pad(uq[:, :, QK_NOPE:], ((0, 0), (0, 0), (0, LANES - QK_ROPE)))
    w_uq_r = jnp.concatenate([uq[:, :, :QK_NOPE].reshape(Q_LORA, -1), uq_rope.reshape(Q_LORA, -1)], axis=1)
    uk = jnp.transpose(w_uk[l], (1, 2, 0))
    uk_pair = jnp.zeros((MLA_HEADS // 2, 2 * QK_NOPE, 2 * KV_LORA), F32)
    uk_pair = uk_pair.at[:, :QK_NOPE, :KV_LORA].set(uk[0::2]).at[:, QK_NOPE:, KV_LORA:].set(uk[1::2])
    uv = jnp.transpose(w_uv[l], (1, 0, 2))
    uv_pair = jnp.zeros((MLA_HEADS // 2, 2 * KV_LORA, 2 * V_HEAD), F32)
    uv_pair = uv_pair.at[:, :KV_LORA, :V_HEAD].set(uv[0::2]).at[:, KV_LORA:, V_HEAD:].set(uv[1::2])
    row = lambda a: a.reshape(1, -1)
    return {
        'w_in': w_in_r.astype(BF16), 'norm_mix_pre': row(norm_mix_pre[l]), 'norm_mix_post': row(norm_mix_post[l]),
        'q_norm': row(q_norm[l]), 'kv_norm': row(kv_norm[l]), 'w_uq': w_uq_r.astype(BF16),
        'w_uk': uk_pair.astype(BF16), 'w_uv': uv_pair.astype(BF16),
        'hg_gain': row(hg_out_norm[l]), 'w_out': w_out[l].astype(BF16),
        'norm_ffn_pre': row(norm_ffn_pre[l]), 'norm_ffn_post': row(norm_ffn_post[l]),
        'w_ffn_up': w_ffn_up[l].astype(BF16), 'ffn_conv_w': ffn_conv_w[l], 'ffn_conv_b': row(ffn_conv_b[l]),
        'w_ffn_down': w_ffn_down[l].astype(BF16),
    }


def kernel(x_prompt, x_sample, cache_kv_latent, cache_k_rope, state_hgrn, state_ffn_conv, meta_tokens, w_in, norm_mix_pre, norm_mix_post, q_norm, kv_norm, w_uq, w_uk, w_uv, hg_lower_bounds, hg_out_norm, w_out, norm_ffn_pre, norm_ffn_post, w_ffn_up, ffn_conv_w, ffn_conv_b, w_ffn_down, final_norm):
    depth = w_in.shape[0]
    b, seq, d = x_prompt.shape
    bs, ts, _ = x_sample.shape
    past = cache_kv_latent.shape[2]
    d_ff = ffn_conv_w.shape[-1]
    lp = PAD_FRONT + N_META + seq

    lb_sm = jax.nn.softmax(hg_lower_bounds.astype(F32), axis=0)
    lb_all = jnp.cumsum(lb_sm, axis=0) - lb_sm[0]

    unit = IN_PROJ_ROWS * KEY_BLOCK // 128
    lp_alloc = -(-lp // unit) * unit
    meta = jnp.broadcast_to(meta_tokens[None], (b, N_META, d))
    xp = jnp.concatenate([jnp.zeros((b, PAD_FRONT, d), F32), meta, x_prompt,
                          jnp.zeros((b, lp_alloc - lp, d), F32)], axis=1)
    xs = x_sample
    tabs_p = _rope_tables(jnp.arange(lp_alloc, dtype=jnp.int32) - CHUNK)
    tabs_s = tuple(jnp.tile(a, (bs, 1)) for a in _rope_tables(past + jnp.arange(ts, dtype=jnp.int32)))
    fin = final_norm.reshape(1, -1)
    s0_p = jnp.zeros((b, HG_HEADS, HG_DK, HG_DV), F32)
    buf0_p = jnp.zeros((b, CONV_W - 1, d_ff), F32)

    outs = [[] for _ in range(8)]
    for l in range(depth):
        w = _layer_weights(l, w_in, norm_mix_pre, norm_mix_post, q_norm, kv_norm, w_uq, w_uk, w_uv, hg_out_norm,
                           w_out, norm_ffn_pre, norm_ffn_post, w_ffn_up, ffn_conv_w, ffn_conv_b, w_ffn_down)
        lb = lb_all[l].reshape(1, -1)
        gain = w['hg_gain']
        last = l == depth - 1

        qcat, kcat, kvt, kv, pe, zh = _in_proj(xp, tabs_p, w, n_valid=lp)
        ol = _mla_prompt(qcat, kcat, kvt, lp=lp)
        ob, s_new = _hgrn(zh, s0_p, lb, gain, n_rows=lp, cs=CHUNK, pad_front=PAD_FRONT)
        xp, buf = _out_ffn(xp, ol, ob, buf0_p, w, fin, n_rows=lp, pad_front=PAD_FRONT, last=last)
        for lst, a in zip(outs[:4], (kv[:, PAD_FRONT:lp], pe[:, PAD_FRONT:lp], s_new, buf)):
            lst.append(a)

        qcat, kcat, _, kv, pe, zh = _in_proj(xs.reshape(1, bs * ts, d), tabs_s, w, n_valid=bs * ts)
        qcat = jnp.swapaxes(qcat.reshape(MLA_HEADS, bs, ts, QCAT), 0, 1)
        kcat, kv, pe, zh = (a.reshape(bs, ts, -1) for a in (kcat, kv, pe, zh))
        ol = _mla_sample(qcat, kcat, cache_kv_latent[l], cache_k_rope[l])
        ob, s_new = _hgrn(zh, state_hgrn[l], lb, gain, n_rows=ts, cs=ts, pad_front=0)
        xs, buf = _out_ffn(xs, ol, ob, state_ffn_conv[l], w, fin, n_rows=ts, pad_front=0, last=last)
        for lst, a in zip(outs[4:], (kv, pe, s_new, buf)):
            lst.append(a)

    return (xp[:, PAD_FRONT + N_META:lp], xs) + tuple(jnp.stack(o) for o in outs)
```

```python
import functools

import jax
import jax.numpy as jnp
from jax import lax
from jax.experimental import pallas as pl
from jax.experimental.pallas import tpu as pltpu

F32 = jnp.float32
BF16 = jnp.bfloat16

CHUNK = 64
N_META = 16
PAD_FRONT = CHUNK - N_META
EPS = 1e-6
MLA_HEADS = 8
QK_NOPE = 64
QK_ROPE = 32
V_HEAD = 64
Q_LORA = 256
KV_LORA = 128
ROPE_THETA = 10000.0
HG_HEADS = 4
HG_DK = 128
HG_DV = 128
CONV_W = 3

HG_KW = HG_HEADS * HG_DK
HG_WIDTH = HG_HEADS * HG_DV
MLA_WIDTH = MLA_HEADS * V_HEAD
LANES = 128
KEY_BLOCK = 256
QCAT = KV_LORA + LANES
VT_ROWS = KV_LORA + 16
IN_PROJ_ROWS = 384
Q_SCALE = (QK_NOPE + QK_ROPE) ** -0.5 * 1.4426950408889634
HGRN_FAST_RANGE = 80.0
HGRN_STREAMS = 4
VMEM_LIMIT = 56 * 1024 * 1024


def _row_tile(n, target):
    if n <= target:
        return n
    best = None
    for t in range(16, target + 1, 16):
        if n % t == 0:
            best = t
    assert best is not None, (n, target)
    return best


def _const_spec(shape):
    nd = len(shape)
    return pl.BlockSpec(shape, lambda *_: (0,) * nd, pipeline_mode=pl.Buffered(1))


def _rms(x, w):
    return x * lax.rsqrt(jnp.mean(x * x, axis=-1, keepdims=True) + EPS) * w


def _silu(x):
    return x * jax.nn.sigmoid(x)


def _mm(a, b):
    return jnp.dot(a.astype(BF16), b.astype(BF16), preferred_element_type=F32)


def _mm_nt(a, b):
    return lax.dot_general(a.astype(BF16), b.astype(BF16), (((1,), (1,)), ((), ())),
                           preferred_element_type=F32)


def _mm_tn(a, b):
    return lax.dot_general(a.astype(BF16), b.astype(BF16), (((0,), (0,)), ((), ())),
                           preferred_element_type=F32)


def _in_proj_body(x_ref, cos_ref, sn_ref, sp_ref, npre_ref, win_ref, qn_ref, kvn_ref, wuq_ref, wuk_ref,
                  qcat_ref, kcat_ref, kvt_ref, kv_ref, pe_ref, zh_ref, *, tm, n_valid):
    cos, sn, sp = cos_ref[...], sn_ref[...], sp_ref[...]

    def rope(v):
        return v * cos + pltpu.roll(v, LANES - QK_ROPE // 2, 1) * sn + pltpu.roll(v, QK_ROPE // 2, 1) * sp

    live = (pl.program_id(1) * tm + lax.broadcasted_iota(jnp.int32, (tm, 1), 0)) < n_valid
    h = _rms(jnp.where(live, x_ref[0], 0.0), npre_ref[...])
    z = _mm(h, win_ref[...])
    zh_ref[0] = z[:, Q_LORA + KV_LORA + LANES:]
    kv_lat = _rms(z[:, Q_LORA:Q_LORA + KV_LORA], kvn_ref[...])
    k_pe = rope(z[:, Q_LORA + KV_LORA:Q_LORA + KV_LORA + LANES])
    kv_ref[0] = kv_lat
    pe_ref[0] = k_pe[:, :QK_ROPE]
    kcat_ref[0, :, :KV_LORA] = kv_lat.astype(BF16)
    kcat_ref[0, :, KV_LORA:] = k_pe.astype(BF16)
    kvt_ref[0, :KV_LORA, :] = kv_lat.T.astype(BF16)
    kvt_ref[0, KV_LORA:, :] = jnp.ones((VT_ROWS - KV_LORA, tm), BF16)

    cqn = _rms(z[:, :Q_LORA], qn_ref[...])
    q = _mm(cqn, wuq_ref[...]) * Q_SCALE
    nope_w = MLA_HEADS * QK_NOPE
    for j in range(MLA_HEADS // 2):
        ql = _mm(q[:, 2 * QK_NOPE * j:2 * QK_NOPE * (j + 1)], wuk_ref[j])
        for hh in range(2):
            hd = 2 * j + hh
            qcat_ref[0, hd, :, :KV_LORA] = ql[:, KV_LORA * hh:KV_LORA * (hh + 1)].astype(BF16)
            qcat_ref[0, hd, :, KV_LORA:] = rope(q[:, nope_w + LANES * hd:nope_w + LANES * (hd + 1)]).astype(BF16)


def _in_proj(x, tabs, w, *, n_valid):
    bx, t, d = x.shape
    tm = _row_tile(t, IN_PROJ_ROWS)
    assert tm % LANES == 0 or tm == t
    nt = t // tm
    row = lambda shape: pl.BlockSpec((1, tm) + shape, lambda b, i: (b, i) + (0,) * len(shape))
    tab = pl.BlockSpec((tm, LANES), lambda b, i: (i, 0))
    zw = 2 * HG_KW + 2 * HG_WIDTH
    out_shape = (
        jax.ShapeDtypeStruct((bx, MLA_HEADS, t, QCAT), BF16),
        jax.ShapeDtypeStruct((bx, t, QCAT), BF16),
        jax.ShapeDtypeStruct((bx, VT_ROWS, t), BF16),
        jax.ShapeDtypeStruct((bx, t, KV_LORA), F32),
        jax.ShapeDtypeStruct((bx, t, QK_ROPE), F32),
        jax.ShapeDtypeStruct((bx, t, zw), F32),
    )
    out_specs = (
        pl.BlockSpec((1, MLA_HEADS, tm, QCAT), lambda b, i: (b, 0, i, 0)),
        row((QCAT,)),
        pl.BlockSpec((1, VT_ROWS, tm), lambda b, i: (b, 0, i)),
        row((KV_LORA,)), row((QK_ROPE,)), row((zw,)),
    )
    weights = (w['norm_mix_pre'], w['w_in'], w['q_norm'], w['kv_norm'], w['w_uq'], w['w_uk'])
    return pl.pallas_call(
        functools.partial(_in_proj_body, tm=tm, n_valid=n_valid),
        grid=(bx, nt),
        in_specs=[row((d,)), tab, tab, tab] + [_const_spec(a.shape) for a in weights],
        out_specs=out_specs,
        out_shape=out_shape,
        compiler_params=pltpu.CompilerParams(dimension_semantics=("parallel", "parallel"),
                                             vmem_limit_bytes=VMEM_LIMIT),
        name="in_proj",
    )(x, *tabs, *weights)


def _mla_prompt_body(q_ref, k_ref, vt_ref, o_ref, m_scr, acc_scr, bias_scr):
    j = pl.program_id(1)
    base = j * KEY_BLOCK
    m_scr[...] = jnp.full(m_scr.shape, -jnp.inf, F32)
    acc_scr[...] = jnp.zeros(acc_scr.shape, F32)
    row = lax.broadcasted_iota(jnp.int32, (KEY_BLOCK, KEY_BLOCK), 0)
    col = lax.broadcasted_iota(jnp.int32, (KEY_BLOCK, KEY_BLOCK), 1)
    real = row >= PAD_FRONT
    bias_scr[0] = jnp.where(real, 0.0, -jnp.inf)
    visible = ((base + row) // CHUNK) <= ((base + col) // CHUNK)
    bias_scr[1] = jnp.where(visible, jnp.where(base + row >= PAD_FRONT, 0.0, -jnp.inf), -jnp.inf)
    bias_scr[2] = jnp.zeros((KEY_BLOCK, KEY_BLOCK), F32)

    def keys_of(kb):
        start = kb * KEY_BLOCK
        return pl.ds(start if isinstance(kb, int) else pl.multiple_of(start, KEY_BLOCK), KEY_BLOCK)

    def scores(kb):
        return _mm_nt(k_ref[0, keys_of(kb), :], q_ref[0].reshape(MLA_HEADS * KEY_BLOCK, QCAT))

    def absorb(s, kb, bias_idx):
        if bias_idx is not None:
            s = s + jnp.concatenate([bias_scr[bias_idx]] * MLA_HEADS, axis=1)
        m_old = m_scr[...]
        m_new = jnp.maximum(m_old, jnp.max(s, axis=0, keepdims=True))
        p = jnp.exp2(s - m_new)
        acc_scr[...] = jnp.exp2(m_old - m_new) * acc_scr[...] + _mm(vt_ref[0, :, keys_of(kb)], p)
        m_scr[...] = m_new

    def pair(a, b, bias_a, bias_b):
        s_a, s_b = scores(a), scores(b)
        absorb(s_a, a, bias_a)
        absorb(s_b, b, bias_b)

    first_mask, diag_mask, no_mask = 0, 1, 2

    @pl.when(j == 0)
    def _():
        absorb(scores(0), 0, diag_mask)

    @pl.when(j > 0)
    def _():
        pair(j, j - 1, diag_mask, jnp.where(j == 1, first_mask, no_mask))
        n_inner = jnp.maximum(j - 2, 0) // 2

        def inner(p, carry):
            pair(j - 2 * p, j - 2 * p - 1, None, None)
            return carry

        lax.fori_loop(1, n_inner + 1, inner, 0)
        left = j - 1 - 2 * n_inner

        @pl.when(left == 1)
        def _():
            absorb(scores(0), 0, first_mask)

        @pl.when(left == 2)
        def _():
            pair(1, 0, None, first_mask)

    for hd in range(MLA_HEADS):
        acc = acc_scr[:, KEY_BLOCK * hd:KEY_BLOCK * (hd + 1)]
        o_t = acc[:KV_LORA] * (1.0 / acc[KV_LORA:KV_LORA + 1])
        o_ref[0, :, KV_LORA * hd:KV_LORA * (hd + 1)] = o_t.T.astype(BF16)


def _mla_prompt(qcat, kcat, kvt, *, lp):
    b, _, t, _ = qcat.shape
    assert t % KEY_BLOCK == 0 and lp <= t
    return pl.pallas_call(
        _mla_prompt_body,
        grid=(b, pl.cdiv(lp, KEY_BLOCK)),
        in_specs=[pl.BlockSpec((1, MLA_HEADS, KEY_BLOCK, QCAT), lambda i, j: (i, 0, j, 0)),
                  pl.BlockSpec((1, t, QCAT), lambda i, j: (i, 0, 0)),
                  pl.BlockSpec((1, VT_ROWS, t), lambda i, j: (i, 0, 0))],
        out_specs=pl.BlockSpec((1, KEY_BLOCK, MLA_HEADS * KV_LORA), lambda i, j: (i, j, 0)),
        out_shape=jax.ShapeDtypeStruct((b, t, MLA_HEADS * KV_LORA), BF16),
        scratch_shapes=[pltpu.VMEM((1, MLA_HEADS * KEY_BLOCK), F32),
                        pltpu.VMEM((VT_ROWS, MLA_HEADS * KEY_BLOCK), F32),
                        pltpu.VMEM((3, KEY_BLOCK, KEY_BLOCK), F32)],
        compiler_params=pltpu.CompilerParams(dimension_semantics=("parallel", "parallel"),
                                             vmem_limit_bytes=VMEM_LIMIT),
        name="mla_prompt",
    )(qcat, kcat, kvt)


def _mla_sample_body(q_ref, k_ref, ckv_ref, cpe_ref, o_ref, *, ts):
    rows = MLA_HEADS * ts
    q = q_ref[0].reshape(rows, QCAT)
    ckv = ckv_ref[0].astype(BF16)
    knew = k_ref[0]
    s_c = _mm_nt(q[:, :KV_LORA], ckv) + _mm_nt(q[:, KV_LORA:KV_LORA + QK_ROPE], cpe_ref[0])
    s_n = _mm_nt(q, knew)
    m = jnp.maximum(jnp.max(s_c, axis=-1, keepdims=True), jnp.max(s_n, axis=-1, keepdims=True))
    p_c = jnp.exp2(s_c - m)
    p_n = jnp.exp2(s_n - m)
    l = jnp.sum(p_c, axis=-1, keepdims=True) + jnp.sum(p_n, axis=-1, keepdims=True)
    o = (_mm(p_c, ckv) + _mm(p_n, knew[:, :KV_LORA])) / l
    for hd in range(MLA_HEADS):
        o_ref[0, :, KV_LORA * hd:KV_LORA * (hd + 1)] = o[ts * hd:ts * (hd + 1)].astype(BF16)


def _mla_sample(qcat, kcat, cache_kv, cache_pe):
    b, _, ts, _ = qcat.shape
    past = cache_kv.shape[1]
    return pl.pallas_call(
        functools.partial(_mla_sample_body, ts=ts),
        grid=(b,),
        in_specs=[pl.BlockSpec((1, MLA_HEADS, ts, QCAT), lambda i: (i, 0, 0, 0)),
                  pl.BlockSpec((1, ts, QCAT), lambda i: (i, 0, 0)),
                  pl.BlockSpec((1, past, KV_LORA), lambda i: (i, 0, 0)),
                  pl.BlockSpec((1, past, QK_ROPE), lambda i: (i, 0, 0))],
        out_specs=pl.BlockSpec((1, ts, MLA_HEADS * KV_LORA), lambda i: (i, 0, 0)),
        out_shape=jax.ShapeDtypeStruct((b, ts, MLA_HEADS * KV_LORA), BF16),
        compiler_params=pltpu.CompilerParams(dimension_semantics=("parallel",),
                                             vmem_limit_bytes=VMEM_LIMIT),
        name="mla_sample",
    )(qcat, kcat, cache_kv, cache_pe)


def _hgrn_body(zh_ref, s0_ref, lb_ref, gain_ref, ob_ref, sout_ref, st_ref, g_scr, q_scr, k_scr, v_scr, o_scr,
               *, bb, cs, pad_front):
    c = pl.program_id(1)
    nc = pl.num_programs(1)
    streams = range(bb)
    heads = [slice(HG_DK * hd, HG_DK * (hd + 1)) for hd in range(HG_HEADS)]

    @pl.when(c == 0)
    def _():
        for bi in streams:
            for hd in range(HG_HEADS):
                st_ref[bi, hd] = s0_ref[bi, hd].T

    hq = zh_ref[:, :, :HG_KW]
    hf = zh_ref[:, :, HG_KW:2 * HG_KW]
    v = zh_ref[:, :, 2 * HG_KW:2 * HG_KW + HG_WIDTH]
    hg = zh_ref[:, :, 2 * HG_KW + HG_WIDTH:]
    lb = lb_ref[...]
    e = jnp.exp(-jnp.abs(hf))
    r = 1.0 / (1.0 + e)
    pos = hf >= 0.0
    f = lb + (1.0 - lb) * jnp.where(pos, r, e * r)
    gl = jnp.log(f)
    k = (1.0 - lb) * jnp.where(pos, e * r, r)
    q = _silu(hq)
    if pad_front:
        live = (c * cs + lax.broadcasted_iota(jnp.int32, (cs, 1), 0)) >= pad_front
        gl = jnp.where(live, gl, 0.0)
        k = jnp.where(live, k, 0.0)
        q = jnp.where(live, q, 0.0)
    t_idx = lax.broadcasted_iota(jnp.int32, (cs, cs), 0)
    s_idx = lax.broadcasted_iota(jnp.int32, (cs, cs), 1)
    causal = t_idx >= s_idx
    tri = causal.astype(F32)
    g = jnp.stack([jnp.dot(tri, gl[bi], precision=lax.Precision.HIGHEST, preferred_element_type=F32)
                   for bi in streams])
    g_tot = g[:, cs - 1:cs, :]
    qt = q * jnp.exp(g)
    khat = k * jnp.exp(g_tot - g)
    kt = k * jnp.exp(-g)
    a = [[jnp.where(causal, _mm_nt(qt[bi][:, sl], kt[bi][:, sl]), 0.0) for sl in heads] for bi in streams]
    for bi in streams:
        o_scr[bi] = jnp.concatenate([_mm(a[bi][hd], v[bi][:, sl]) for hd, sl in enumerate(heads)], axis=-1)

    for bi in streams:
        @pl.when(jnp.max(-g_tot[bi]) > HGRN_FAST_RANGE)
        def _(bi=bi):
            g_scr[...] = g[bi]
            q_scr[...] = q[bi]
            k_scr[...] = k[bi]
            v_scr[...] = v[bi]
            o_scr[bi] = jnp.zeros((cs, HG_WIDTH), F32)
            t_col = lax.broadcasted_iota(jnp.int32, (cs, 1), 0)

            def cols(i, carry):
                base = pl.multiple_of(i * 8, 8)
                g8, k8, v8 = (ref[pl.ds(base, 8), :] for ref in (g_scr, k_scr, v_scr))
                for j in range(8):
                    dec = jnp.exp(jnp.where(t_col >= base + j, g_scr[...] - g8[j:j + 1], -jnp.inf))
                    pr = q_scr[...] * dec * k8[j:j + 1]
                    for sl in heads:
                        o_scr[bi, :, sl] += jnp.sum(pr[:, sl], axis=-1, keepdims=True) * v8[j:j + 1, sl]
                return carry

            lax.fori_loop(0, cs // 8, cols, 0)

    gain = gain_ref[...]
    o_inter = [[_mm_nt(qt[bi][:, sl], st_ref[bi, hd]) for hd, sl in enumerate(heads)] for bi in streams]
    kv_new = [[_mm_tn(v[bi][:, sl], khat[bi][:, sl]) for sl in heads] for bi in streams]
    decay = jnp.exp(g_tot)
    for bi in streams:
        for hd, sl in enumerate(heads):
            st_ref[bi, hd] = decay[bi][:, sl] * st_ref[bi, hd] + kv_new[bi][hd]
        o = [_rms(o_scr[bi, :, sl] + o_inter[bi][hd], gain) for hd, sl in enumerate(heads)]
        ob_ref[bi] = (jnp.concatenate(o, axis=-1) * _silu(hg[bi])).astype(BF16)

    @pl.when(c == nc - 1)
    def _():
        for bi in streams:
            for hd in range(HG_HEADS):
                sout_ref[bi, hd] = st_ref[bi, hd].T


def _hgrn(zh, s0, lb, gain, *, n_rows, cs, pad_front):
    b, t, zw = zh.shape
    assert n_rows % cs == 0 and n_rows <= t
    bb = HGRN_STREAMS if b % HGRN_STREAMS == 0 else 1
    return pl.pallas_call(
        functools.partial(_hgrn_body, bb=bb, cs=cs, pad_front=pad_front),
        grid=(b // bb, n_rows // cs),
        in_specs=[pl.BlockSpec((bb, cs, zw), lambda i, c: (i, c, 0)),
                  pl.BlockSpec((bb, HG_HEADS, HG_DK, HG_DV), lambda i, c: (i, 0, 0, 0)),
                  _const_spec(lb.shape), _const_spec(gain.shape)],
        out_specs=(pl.BlockSpec((bb, cs, HG_WIDTH), lambda i, c: (i, c, 0)),
                   pl.BlockSpec((bb, HG_HEADS, HG_DK, HG_DV), lambda i, c: (i, 0, 0, 0))),
        out_shape=(jax.ShapeDtypeStruct((b, t, HG_WIDTH), BF16),
                   jax.ShapeDtypeStruct((b, HG_HEADS, HG_DK, HG_DV), F32)),
        scratch_shapes=[pltpu.VMEM((bb, HG_HEADS, HG_DV, HG_DK), F32)]
        + [pltpu.VMEM((cs, HG_KW), F32) for _ in range(4)] + [pltpu.VMEM((bb, cs, HG_KW), F32)],
        compiler_params=pltpu.CompilerParams(dimension_semantics=("parallel", "arbitrary"),
                                             vmem_limit_bytes=VMEM_LIMIT),
        name="hgrn",
    )(zh, s0, lb, gain)


def _out_ffn_body(x_ref, ol_ref, ob_ref, buf_ref, wuv_ref, wout_ref, npost_ref, nfpre_ref, wup_ref, cw_ref,
                  cb_ref, wdn_ref, nfpost_ref, fin_ref, xo_ref, bufo_ref, a_scr, *, tm, pad_front, last):
    t = pl.program_id(1)
    nt = pl.num_programs(1)
    d_ff = cw_ref.shape[1]
    keep = CONV_W - 1

    @pl.when(t == 0)
    def _():
        a_scr[0:8, :] = jnp.zeros((8, d_ff), F32)
        a_scr[8 - keep:8, :] = buf_ref[0]

    ol = ol_ref[0]
    o_a = [_mm(ol[:, 2 * KV_LORA * j:2 * KV_LORA * (j + 1)], wuv_ref[j]) for j in range(MLA_HEADS // 2)]
    mixed = jnp.concatenate([o.astype(BF16) for o in o_a] + [ob_ref[0]], axis=-1)
    x1 = x_ref[0] + _rms(_mm(mixed, wout_ref[...]), npost_ref[...])
    u = _mm(_rms(x1, nfpre_ref[...]), wup_ref[...])
    a = u[:, :d_ff]
    if pad_front:
        live = (t * tm + lax.broadcasted_iota(jnp.int32, (tm, 1), 0)) >= pad_front
        a = jnp.where(live, a, 0.0)
    a_scr[8:8 + tm, :] = a
    cw = cw_ref[...]
    conv = cb_ref[...] + a * cw[keep:keep + 1]
    for j in range(keep):
        conv = conv + a_scr[8 - keep + j:8 - keep + j + tm, :] * cw[j:j + 1]
    f = _mm(_silu(conv) * u[:, d_ff:], wdn_ref[...])
    x2 = x1 + _rms(f, nfpost_ref[...])
    xo_ref[0] = _rms(x2, fin_ref[...]) if last else x2
    a_scr[0:8, :] = a_scr[tm:tm + 8, :]

    @pl.when(t == nt - 1)
    def _():
        bufo_ref[0] = a_scr[8 - keep:8, :]


def _out_ffn(x, ol, ob, buf, w, final_norm, *, n_rows, pad_front, last):
    b, t, d = x.shape
    d_ff = buf.shape[-1]
    tm = _row_tile(n_rows, 192)
    assert tm >= 8 and n_rows <= t
    row = lambda width: pl.BlockSpec((1, tm, width), lambda i, j: (i, j, 0))
    per_b = pl.BlockSpec((1, CONV_W - 1, d_ff), lambda i, j: (i, 0, 0))
    weights = (w['w_uv'], w['w_out'], w['norm_mix_post'], w['norm_ffn_pre'], w['w_ffn_up'], w['ffn_conv_w'],
               w['ffn_conv_b'], w['w_ffn_down'], w['norm_ffn_post'], final_norm)
    return pl.pallas_call(
        functools.partial(_out_ffn_body, tm=tm, pad_front=pad_front, last=last),
        grid=(b, n_rows // tm),
        in_specs=[row(d), row(MLA_HEADS * KV_LORA), row(HG_WIDTH), per_b] + [_const_spec(a.shape) for a in weights],
        out_specs=(row(d), per_b),
        out_shape=(jax.ShapeDtypeStruct((b, t, d), F32), jax.ShapeDtypeStruct((b, CONV_W - 1, d_ff), F32)),
        scratch_shapes=[pltpu.VMEM((tm + 8, d_ff), F32)],
        compiler_params=pltpu.CompilerParams(dimension_semantics=("parallel", "arbitrary"),
                                             vmem_limit_bytes=VMEM_LIMIT),
        name="out_ffn",
    )(x, ol, ob, buf, *weights)


def _rope_tables(pos):
    half = QK_ROPE // 2
    inv = ROPE_THETA ** (-jnp.arange(half, dtype=F32) / half)
    ang = pos.astype(F32)[:, None] * inv
    cos, sin = jnp.cos(ang), jnp.sin(ang)
    zero = jnp.zeros_like(cos)
    pad = jnp.zeros((pos.shape[0], LANES - QK_ROPE), F32)
    return (jnp.concatenate([cos, cos, pad], axis=1),
            jnp.concatenate([-sin, zero, pad], axis=1),
            jnp.concatenate([zero, sin, pad], axis=1))


def _layer_weights(l, w_in, norm_mix_pre, norm_mix_post, q_norm, kv_norm, w_uq, w_uk, w_uv, hg_out_norm, w_out,
                   norm_ffn_pre, norm_ffn_post, w_ffn_up, ffn_conv_w, ffn_conv_b, w_ffn_down):
    d = w_in.shape[1]
    head_w = Q_LORA + KV_LORA + QK_ROPE
    w_in_r = jnp.concatenate([w_in[l][:, :head_w], jnp.zeros((d, LANES - QK_ROPE), F32), w_in[l][:, head_w:]], axis=1)
    uq = w_uq[l].reshape(Q_LORA, MLA_HEADS, QK_NOPE + QK_ROPE)
    uq_rope = jnp.pad(uq[:, :, QK_NOPE:], ((0, 0), (0, 0), (0, LANES - QK_ROPE)))
    w_uq_r = jnp.concatenate([uq[:, :, :QK_NOPE].reshape(Q_LORA, -1), uq_rope.reshape(Q_LORA, -1)], axis=1)
    uk = jnp.transpose(w_uk[l], (1, 2, 0))
    uk_pair = jnp.zeros((MLA_HEADS // 2, 2 * QK_NOPE, 2 * KV_LORA), F32)
    uk_pair = uk_pair.at[:, :QK_NOPE, :KV_LORA].set(uk[0::2]).at[:, QK_NOPE:, KV_LORA:].set(uk[1::2])
    uv = jnp.transpose(w_uv[l], (1, 0, 2))
    uv_pair = jnp.zeros((MLA_HEADS // 2, 2 * KV_LORA, 2 * V_HEAD), F32)
    uv_pair = uv_pair.at[:, :KV_LORA, :V_HEAD].set(uv[0::2]).at[:, KV_LORA:, V_HEAD:].set(uv[1::2])
    row = lambda a: a.reshape(1, -1)
    return {
        'w_in': w_in_r.astype(BF16), 'norm_mix_pre': row(norm_mix_pre[l]), 'norm_mix_post': row(norm_mix_post[l]),
        'q_norm': row(q_norm[l]), 'kv_norm': row(kv_norm[l]), 'w_uq': w_uq_r.astype(BF16),
        'w_uk': uk_pair.astype(BF16), 'w_uv': uv_pair.astype(BF16),
        'hg_gain': row(hg_out_norm[l]), 'w_out': w_out[l].astype(BF16),
        'norm_ffn_pre': row(norm_ffn_pre[l]), 'norm_ffn_post': row(norm_ffn_post[l]),
        'w_ffn_up': w_ffn_up[l].astype(BF16), 'ffn_conv_w': ffn_conv_w[l], 'ffn_conv_b': row(ffn_conv_b[l]),
        'w_ffn_down': w_ffn_down[l].astype(BF16),
    }


def kernel(x_prompt, x_sample, cache_kv_latent, cache_k_rope, state_hgrn, state_ffn_conv, meta_tokens, w_in, norm_mix_pre, norm_mix_post, q_norm, kv_norm, w_uq, w_uk, w_uv, hg_lower_bounds, hg_out_norm, w_out, norm_ffn_pre, norm_ffn_post, w_ffn_up, ffn_conv_w, ffn_conv_b, w_ffn_down, final_norm):
    depth = w_in.shape[0]
    b, seq, d = x_prompt.shape
    bs, ts, _ = x_sample.shape
    past = cache_kv_latent.shape[2]
    d_ff = ffn_conv_w.shape[-1]
    lp = PAD_FRONT + N_META + seq

    lb_sm = jax.nn.softmax(hg_lower_bounds.astype(F32), axis=0)
    lb_all = jnp.cumsum(lb_sm, axis=0) - lb_sm[0]

    unit = IN_PROJ_ROWS * KEY_BLOCK // 128
    lp_alloc = -(-lp // unit) * unit
    meta = jnp.broadcast_to(meta_tokens[None], (b, N_META, d))
    xp = jnp.concatenate([jnp.zeros((b, PAD_FRONT, d), F32), meta, x_prompt,
                          jnp.zeros((b, lp_alloc - lp, d), F32)], axis=1)
    xs = x_sample
    tabs_p = _rope_tables(jnp.arange(lp_alloc, dtype=jnp.int32) - CHUNK)
    tabs_s = tuple(jnp.tile(a, (bs, 1)) for a in _rope_tables(past + jnp.arange(ts, dtype=jnp.int32)))
    fin = final_norm.reshape(1, -1)
    s0_p = jnp.zeros((b, HG_HEADS, HG_DK, HG_DV), F32)
    buf0_p = jnp.zeros((b, CONV_W - 1, d_ff), F32)

    outs = [[] for _ in range(8)]
    for l in range(depth):
        w = _layer_weights(l, w_in, norm_mix_pre, norm_mix_post, q_norm, kv_norm, w_uq, w_uk, w_uv, hg_out_norm,
                           w_out, norm_ffn_pre, norm_ffn_post, w_ffn_up, ffn_conv_w, ffn_conv_b, w_ffn_down)
        lb = lb_all[l].reshape(1, -1)
        gain = w['hg_gain']
        last = l == depth - 1

        qcat, kcat, kvt, kv, pe, zh = _in_proj(xp, tabs_p, w, n_valid=lp)
        ol = _mla_prompt(qcat, kcat, kvt, lp=lp)
        ob, s_new = _hgrn(zh, s0_p, lb, gain, n_rows=lp, cs=CHUNK, pad_front=PAD_FRONT)
        xp, buf = _out_ffn(xp, ol, ob, buf0_p, w, fin, n_rows=lp, pad_front=PAD_FRONT, last=last)
        for lst, a in zip(outs[:4], (kv[:, PAD_FRONT:lp], pe[:, PAD_FRONT:lp], s_new, buf)):
            lst.append(a)

        qcat, kcat, _, kv, pe, zh = _in_proj(xs.reshape(1, bs * ts, d), tabs_s, w, n_valid=bs * ts)
        qcat = jnp.swapaxes(qcat.reshape(MLA_HEADS, bs, ts, QCAT), 0, 1)
        kcat, kv, pe, zh = (a.reshape(bs, ts, -1) for a in (kcat, kv, pe, zh))
        ol = _mla_sample(qcat, kcat, cache_kv_latent[l], cache_k_rope[l])
        ob, s_new = _hgrn(zh, state_hgrn[l], lb, gain, n_rows=ts, cs=ts, pad_front=0)
        xs, buf = _out_ffn(xs, ol, ob, state_ffn_conv[l], w, fin, n_rows=ts, pad_front=0, last=last)
        for lst, a in zip(outs[4:], (kv, pe, s_new, buf)):
            lst.append(a)

    return (xp[:, PAD_FRONT + N_META:lp], xs) + tuple(jnp.stack(o) for o in outs)
```

```python
import functools

import jax
import jax.numpy as jnp
from jax import lax
from jax.experimental import pallas as pl
from jax.experimental.pallas import tpu as pltpu

F32 = jnp.float32
BF16 = jnp.bfloat16

CHUNK = 64
N_META = 16
PAD_FRONT = CHUNK - N_META
EPS = 1e-6
MLA_HEADS = 8
QK_NOPE = 64
QK_ROPE = 32
V_HEAD = 64
Q_LORA = 256
KV_LORA = 128
ROPE_THETA = 10000.0
HG_HEADS = 4
HG_DK = 128
HG_DV = 128
CONV_W = 3

HG_KW = HG_HEADS * HG_DK
HG_WIDTH = HG_HEADS * HG_DV
MLA_WIDTH = MLA_HEADS * V_HEAD
LANES = 128
KEY_BLOCK = 256
QCAT = KV_LORA + LANES
VT_ROWS = KV_LORA + 16
IN_PROJ_ROWS = 384
IN_PROJ_STREAMS = 2
Q_SCALE = (QK_NOPE + QK_ROPE) ** -0.5 * 1.4426950408889634
HGRN_FAST_RANGE = 80.0
FFN_STREAMS = 2
HGRN_STREAMS = 4
VMEM_LIMIT = 56 * 1024 * 1024


def _row_tile(n, target):
    if n <= target:
        return n
    best = None
    for t in range(16, target + 1, 16):
        if n % t == 0:
            best = t
    assert best is not None, (n, target)
    return best


def _const_spec(shape):
    nd = len(shape)
    return pl.BlockSpec(shape, lambda *_: (0,) * nd, pipeline_mode=pl.Buffered(1))


def _rms(x, w):
    return x * lax.rsqrt(jnp.mean(x * x, axis=-1, keepdims=True) + EPS) * w


def _silu(x):
    return x * jax.nn.sigmoid(x)


def _mm(a, b):
    return jnp.dot(a.astype(BF16), b.astype(BF16), preferred_element_type=F32)


def _mm_nt(a, b):
    return lax.dot_general(a.astype(BF16), b.astype(BF16), (((1,), (1,)), ((), ())),
                           preferred_element_type=F32)


def _mm_tn(a, b):
    return lax.dot_general(a.astype(BF16), b.astype(BF16), (((0,), (0,)), ((), ())),
                           preferred_element_type=F32)


def _in_proj_body(x_ref, cos_ref, sn_ref, sp_ref, cosq_ref, sinq_ref, npre_ref, win_ref, qn_ref, kvn_ref, wuq_ref,
                  wuk_ref, spread_ref, qcat_ref, kcat_ref, kvt_ref, kv_ref, pe_ref, zh_ref, *, bb, tm, n_valid):
    cos, sn, sp = cos_ref[...], sn_ref[...], sp_ref[...]

    def rope(v):
        return v * cos + pltpu.roll(v, LANES - QK_ROPE // 2, 1) * sn + pltpu.roll(v, QK_ROPE // 2, 1) * sp

    live = (pl.program_id(1) * tm + lax.broadcasted_iota(jnp.int32, (tm, 1), 0)) < n_valid
    h = [_rms(jnp.where(live, x_ref[bi], 0.0), npre_ref[...]) for bi in range(bb)]
    zs = [_mm(h[bi], win_ref[...]) for bi in range(bb)]
    nope_w = MLA_HEADS * QK_NOPE
    for bi, z in enumerate(zs):
        zh_ref[bi] = z[:, Q_LORA + KV_LORA + LANES:]
        kv_lat = _rms(z[:, Q_LORA:Q_LORA + KV_LORA], kvn_ref[...])
        k_pe = rope(z[:, Q_LORA + KV_LORA:Q_LORA + KV_LORA + LANES])
        kv_ref[bi] = kv_lat
        pe_ref[bi] = k_pe[:, :QK_ROPE]
        kcat_ref[bi, :, :KV_LORA] = kv_lat.astype(BF16)
        kcat_ref[bi, :, KV_LORA:] = k_pe.astype(BF16)
        kvt_ref[bi, :KV_LORA, :] = kv_lat.T.astype(BF16)
        kvt_ref[bi, KV_LORA:, :] = jnp.ones((VT_ROWS - KV_LORA, tm), BF16)

        cqn = _rms(z[:, :Q_LORA], qn_ref[...])
        q = _mm(cqn, wuq_ref[...]) * Q_SCALE
        x1, x2 = q[:, nope_w:nope_w + LANES], q[:, nope_w + LANES:]
        cq, sq = cosq_ref[...], sinq_ref[...]
        roped = jnp.concatenate([x1 * cq - x2 * sq, x1 * sq + x2 * cq], axis=1)
        pe_heads = _mm(roped, spread_ref[...])
        for j in range(MLA_HEADS // 2):
            ql = _mm(q[:, 2 * QK_NOPE * j:2 * QK_NOPE * (j + 1)], wuk_ref[j])
            for hh in range(2):
                hd = 2 * j + hh
                qcat_ref[bi, hd, :, :KV_LORA] = ql[:, KV_LORA * hh:KV_LORA * (hh + 1)].astype(BF16)
                qcat_ref[bi, hd, :, KV_LORA:] = pe_heads[:, LANES * hd:LANES * (hd + 1)].astype(BF16)


def _in_proj(x, tabs, w, *, n_valid):
    bx, t, d = x.shape
    tm = _row_tile(t, IN_PROJ_ROWS)
    assert tm % LANES == 0 or tm == t
    nt = t // tm
    bb = IN_PROJ_STREAMS if bx % IN_PROJ_STREAMS == 0 else 1
    row = lambda shape: pl.BlockSpec((bb, tm) + shape, lambda b, i: (b, i) + (0,) * len(shape))
    tab = pl.BlockSpec((tm, LANES), lambda b, i: (i, 0))
    zw = 2 * HG_KW + 2 * HG_WIDTH
    out_shape = (
        jax.ShapeDtypeStruct((bx, MLA_HEADS, t, QCAT), BF16),
        jax.ShapeDtypeStruct((bx, t, QCAT), BF16),
        jax.ShapeDtypeStruct((bx, VT_ROWS, t), BF16),
        jax.ShapeDtypeStruct((bx, t, KV_LORA), F32),
        jax.ShapeDtypeStruct((bx, t, QK_ROPE), F32),
        jax.ShapeDtypeStruct((bx, t, zw), F32),
    )
    out_specs = (
        pl.BlockSpec((bb, MLA_HEADS, tm, QCAT), lambda b, i: (b, 0, i, 0)),
        row((QCAT,)),
        pl.BlockSpec((bb, VT_ROWS, tm), lambda b, i: (b, 0, i)),
        row((KV_LORA,)), row((QK_ROPE,)), row((zw,)),
    )
    weights = (w['norm_mix_pre'], w['w_in'], w['q_norm'], w['kv_norm'], w['w_uq'], w['w_uk'], w['pe_spread'])
    return pl.pallas_call(
        functools.partial(_in_proj_body, bb=bb, tm=tm, n_valid=n_valid),
        grid=(bx // bb, nt),
        in_specs=[row((d,))] + [tab] * len(tabs) + [_const_spec(a.shape) for a in weights],
        out_specs=out_specs,
        out_shape=out_shape,
        compiler_params=pltpu.CompilerParams(dimension_semantics=("parallel", "parallel"),
                                             vmem_limit_bytes=VMEM_LIMIT),
        name="in_proj",
    )(x, *tabs, *weights)


def _mla_prompt_body(q_ref, k_ref, vt_ref, o_ref, m_scr, acc_scr, bias_scr):
    j = pl.program_id(1)
    base = j * KEY_BLOCK
    m_scr[...] = jnp.full(m_scr.shape, -jnp.inf, F32)
    acc_scr[...] = jnp.zeros(acc_scr.shape, F32)
    row = lax.broadcasted_iota(jnp.int32, (KEY_BLOCK, KEY_BLOCK), 0)
    col = lax.broadcasted_iota(jnp.int32, (KEY_BLOCK, KEY_BLOCK), 1)
    real = row >= PAD_FRONT
    bias_scr[0] = jnp.where(real, 0.0, -jnp.inf)
    visible = ((base + row) // CHUNK) <= ((base + col) // CHUNK)
    bias_scr[1] = jnp.where(visible, jnp.where(base + row >= PAD_FRONT, 0.0, -jnp.inf), -jnp.inf)
    bias_scr[2] = jnp.zeros((KEY_BLOCK, KEY_BLOCK), F32)

    def keys_of(kb):
        start = kb * KEY_BLOCK
        return pl.ds(start if isinstance(kb, int) else pl.multiple_of(start, KEY_BLOCK), KEY_BLOCK)

    def scores(kb):
        return _mm_nt(k_ref[0, keys_of(kb), :], q_ref[0].reshape(MLA_HEADS * KEY_BLOCK, QCAT))

    def absorb(s, kb, bias_idx):
        if bias_idx is not None:
            s = s + jnp.concatenate([bias_scr[bias_idx]] * MLA_HEADS, axis=1)
        m_old = m_scr[...]
        m_new = jnp.maximum(m_old, jnp.max(s, axis=0, keepdims=True))
        p = jnp.exp2(s - m_new)
        acc_scr[...] = jnp.exp2(m_old - m_new) * acc_scr[...] + _mm(vt_ref[0, :, keys_of(kb)], p)
        m_scr[...] = m_new

    def pair(a, b, bias_a, bias_b):
        s_a, s_b = scores(a), scores(b)
        absorb(s_a, a, bias_a)
        absorb(s_b, b, bias_b)

    first_mask, diag_mask, no_mask = 0, 1, 2

    @pl.when(j == 0)
    def _():
        absorb(scores(0), 0, diag_mask)

    @pl.when(j > 0)
    def _():
        pair(j, j - 1, diag_mask, jnp.where(j == 1, first_mask, no_mask))
        n_inner = jnp.maximum(j - 2, 0) // 2

        def inner(p, carry):
            pair(j - 2 * p, j - 2 * p - 1, None, None)
            return carry

        lax.fori_loop(1, n_inner + 1, inner, 0)
        left = j - 1 - 2 * n_inner

        @pl.when(left == 1)
        def _():
            absorb(scores(0), 0, first_mask)

        @pl.when(left == 2)
        def _():
            pair(1, 0, None, first_mask)

    for hd in range(MLA_HEADS):
        acc = acc_scr[:, KEY_BLOCK * hd:KEY_BLOCK * (hd + 1)]
        o_t = acc[:KV_LORA] * (1.0 / acc[KV_LORA:KV_LORA + 1])
        o_ref[0, :, KV_LORA * hd:KV_LORA * (hd + 1)] = o_t.T.astype(BF16)


def _mla_prompt(qcat, kcat, kvt, *, lp):
    b, _, t, _ = qcat.shape
    assert t % KEY_BLOCK == 0 and lp <= t
    return pl.pallas_call(
        _mla_prompt_body,
        grid=(b, pl.cdiv(lp, KEY_BLOCK)),
        in_specs=[pl.BlockSpec((1, MLA_HEADS, KEY_BLOCK, QCAT), lambda i, j: (i, 0, j, 0)),
                  pl.BlockSpec((1, t, QCAT), lambda i, j: (i, 0, 0)),
                  pl.BlockSpec((1, VT_ROWS, t), lambda i, j: (i, 0, 0))],
        out_specs=pl.BlockSpec((1, KEY_BLOCK, MLA_HEADS * KV_LORA), lambda i, j: (i, j, 0)),
        out_shape=jax.ShapeDtypeStruct((b, t, MLA_HEADS * KV_LORA), BF16),
        scratch_shapes=[pltpu.VMEM((1, MLA_HEADS * KEY_BLOCK), F32),
                        pltpu.VMEM((VT_ROWS, MLA_HEADS * KEY_BLOCK), F32),
                        pltpu.VMEM((3, KEY_BLOCK, KEY_BLOCK), F32)],
        compiler_params=pltpu.CompilerParams(dimension_semantics=("parallel", "parallel"),
                                             vmem_limit_bytes=VMEM_LIMIT),
        name="mla_prompt",
    )(qcat, kcat, kvt)


def _mla_sample_body(q_ref, k_ref, ckv_ref, cpe_ref, o_ref, *, ts):
    rows = MLA_HEADS * ts
    q = q_ref[0].reshape(rows, QCAT)
    ckv = ckv_ref[0].astype(BF16)
    knew = k_ref[0]
    s_c = _mm_nt(q[:, :KV_LORA], ckv) + _mm_nt(q[:, KV_LORA:KV_LORA + QK_ROPE], cpe_ref[0])
    s_n = _mm_nt(q, knew)
    m = jnp.maximum(jnp.max(s_c, axis=-1, keepdims=True), jnp.max(s_n, axis=-1, keepdims=True))
    p_c = jnp.exp2(s_c - m)
    p_n = jnp.exp2(s_n - m)
    l = jnp.sum(p_c, axis=-1, keepdims=True) + jnp.sum(p_n, axis=-1, keepdims=True)
    o = (_mm(p_c, ckv) + _mm(p_n, knew[:, :KV_LORA])) / l
    for hd in range(MLA_HEADS):
        o_ref[0, :, KV_LORA * hd:KV_LORA * (hd + 1)] = o[ts * hd:ts * (hd + 1)].astype(BF16)


def _mla_sample(qcat, kcat, cache_kv, cache_pe):
    b, _, ts, _ = qcat.shape
    past = cache_kv.shape[1]
    return pl.pallas_call(
        functools.partial(_mla_sample_body, ts=ts),
        grid=(b,),
        in_specs=[pl.BlockSpec((1, MLA_HEADS, ts, QCAT), lambda i: (i, 0, 0, 0)),
                  pl.BlockSpec((1, ts, QCAT), lambda i: (i, 0, 0)),
                  pl.BlockSpec((1, past, KV_LORA), lambda i: (i, 0, 0)),
                  pl.BlockSpec((1, past, QK_ROPE), lambda i: (i, 0, 0))],
        out_specs=pl.BlockSpec((1, ts, MLA_HEADS * KV_LORA), lambda i: (i, 0, 0)),
        out_shape=jax.ShapeDtypeStruct((b, ts, MLA_HEADS * KV_LORA), BF16),
        compiler_params=pltpu.CompilerParams(dimension_semantics=("parallel",),
                                             vmem_limit_bytes=VMEM_LIMIT),
        name="mla_sample",
    )(qcat, kcat, cache_kv, cache_pe)


def _hgrn_body(zh_ref, s0_ref, lb_ref, gain_ref, ob_ref, sout_ref, st_ref, g_scr, q_scr, k_scr, v_scr, o_scr,
               *, bb, cs, pad_front):
    c = pl.program_id(1)
    nc = pl.num_programs(1)
    streams = range(bb)
    heads = [slice(HG_DK * hd, HG_DK * (hd + 1)) for hd in range(HG_HEADS)]

    @pl.when(c == 0)
    def _():
        for bi in streams:
            for hd in range(HG_HEADS):
                st_ref[bi, hd] = s0_ref[bi, hd].T

    hq = zh_ref[:, :, :HG_KW]
    hf = zh_ref[:, :, HG_KW:2 * HG_KW]
    v = zh_ref[:, :, 2 * HG_KW:2 * HG_KW + HG_WIDTH]
    hg = zh_ref[:, :, 2 * HG_KW + HG_WIDTH:]
    lb = lb_ref[...]
    e = jnp.exp(-jnp.abs(hf))
    r = 1.0 / (1.0 + e)
    pos = hf >= 0.0
    f = lb + (1.0 - lb) * jnp.where(pos, r, e * r)
    gl = jnp.log(f)
    k = (1.0 - lb) * jnp.where(pos, e * r, r)
    q = _silu(hq)
    if pad_front:
        live = (c * cs + lax.broadcasted_iota(jnp.int32, (cs, 1), 0)) >= pad_front
        gl = jnp.where(live, gl, 0.0)
        k = jnp.where(live, k, 0.0)
        q = jnp.where(live, q, 0.0)
    t_idx = lax.broadcasted_iota(jnp.int32, (cs, cs), 0)
    s_idx = lax.broadcasted_iota(jnp.int32, (cs, cs), 1)
    causal = t_idx >= s_idx
    tri = causal.astype(F32)
    g = jnp.stack([jnp.dot(tri, gl[bi], precision=lax.Precision.HIGHEST, preferred_element_type=F32)
                   for bi in streams])
    g_tot = g[:, cs - 1:cs, :]
    qt = q * jnp.exp(g)
    khat = k * jnp.exp(g_tot - g)
    kt = k * jnp.exp(-g)
    a = [[jnp.where(causal, _mm_nt(qt[bi][:, sl], kt[bi][:, sl]), 0.0) for sl in heads] for bi in streams]
    for bi in streams:
        o_scr[bi] = jnp.concatenate([_mm(a[bi][hd], v[bi][:, sl]) for hd, sl in enumerate(heads)], axis=-1)

    for bi in streams:
        @pl.when(jnp.max(-g_tot[bi]) > HGRN_FAST_RANGE)
        def _(bi=bi):
            g_scr[...] = g[bi]
            q_scr[...] = q[bi]
            k_scr[...] = k[bi]
            v_scr[...] = v[bi]
            o_scr[bi] = jnp.zeros((cs, HG_WIDTH), F32)
            t_col = lax.broadcasted_iota(jnp.int32, (cs, 1), 0)

            def cols(i, carry):
                base = pl.multiple_of(i * 8, 8)
                g8, k8, v8 = (ref[pl.ds(base, 8), :] for ref in (g_scr, k_scr, v_scr))
                for j in range(8):
                    dec = jnp.exp(jnp.where(t_col >= base + j, g_scr[...] - g8[j:j + 1], -jnp.inf))
                    pr = q_scr[...] * dec * k8[j:j + 1]
                    for sl in heads:
                        o_scr[bi, :, sl] += jnp.sum(pr[:, sl], axis=-1, keepdims=True) * v8[j:j + 1, sl]
                return carry

            lax.fori_loop(0, cs // 8, cols, 0)

    gain = gain_ref[...]
    o_inter = [[_mm_nt(qt[bi][:, sl], st_ref[bi, hd]) for hd, sl in enumerate(heads)] for bi in streams]
    kv_new = [[_mm_tn(v[bi][:, sl], khat[bi][:, sl]) for sl in heads] for bi in streams]
    decay = jnp.exp(g_tot)
    for bi in streams:
        for hd, sl in enumerate(heads):
            st_ref[bi, hd] = decay[bi][:, sl] * st_ref[bi, hd] + kv_new[bi][hd]
        o = [_rms(o_scr[bi, :, sl] + o_inter[bi][hd], gain) for hd, sl in enumerate(heads)]
        ob_ref[bi] = (jnp.concatenate(o, axis=-1) * _silu(hg[bi])).astype(BF16)

    @pl.when(c == nc - 1)
    def _():
        for bi in streams:
            for hd in range(HG_HEADS):
                sout_ref[bi, hd] = st_ref[bi, hd].T


def _hgrn(zh, s0, lb, gain, *, n_rows, cs, pad_front):
    b, t, zw = zh.shape
    assert n_rows % cs == 0 and n_rows <= t
    bb = HGRN_STREAMS if b % HGRN_STREAMS == 0 else 1
    return pl.pallas_call(
        functools.partial(_hgrn_body, bb=bb, cs=cs, pad_front=pad_front),
        grid=(b // bb, n_rows // cs),
        in_specs=[pl.BlockSpec((bb, cs, zw), lambda i, c: (i, c, 0)),
                  pl.BlockSpec((bb, HG_HEADS, HG_DK, HG_DV), lambda i, c: (i, 0, 0, 0)),
                  _const_spec(lb.shape), _const_spec(gain.shape)],
        out_specs=(pl.BlockSpec((bb, cs, HG_WIDTH), lambda i, c: (i, c, 0)),
                   pl.BlockSpec((bb, HG_HEADS, HG_DK, HG_DV), lambda i, c: (i, 0, 0, 0))),
        out_shape=(jax.ShapeDtypeStruct((b, t, HG_WIDTH), BF16),
                   jax.ShapeDtypeStruct((b, HG_HEADS, HG_DK, HG_DV), F32)),
        scratch_shapes=[pltpu.VMEM((bb, HG_HEADS, HG_DV, HG_DK), F32)]
        + [pltpu.VMEM((cs, HG_KW), F32) for _ in range(4)] + [pltpu.VMEM((bb, cs, HG_KW), F32)],
        compiler_params=pltpu.CompilerParams(dimension_semantics=("parallel", "arbitrary"),
                                             vmem_limit_bytes=VMEM_LIMIT),
        name="hgrn",
    )(zh, s0, lb, gain)


def _out_ffn_body(x_ref, ol_ref, ob_ref, buf_ref, wuv_ref, wout_ref, npost_ref, nfpre_ref, wup_ref, cw_ref,
                  cb_ref, wdn_ref, nfpost_ref, fin_ref, xo_ref, bufo_ref, a_scr, *, bb, tm, pad_front, last):
    t = pl.program_id(1)
    nt = pl.num_programs(1)
    d_ff = cw_ref.shape[1]
    keep = CONV_W - 1
    streams = range(bb)

    @pl.when(t == 0)
    def _():
        for bi in streams:
            a_scr[bi, 0:8, :] = jnp.zeros((8, d_ff), F32)
            a_scr[bi, 8 - keep:8, :] = buf_ref[bi]

    mixed = []
    for bi in streams:
        ol = ol_ref[bi]
        o_a = [_mm(ol[:, 2 * KV_LORA * j:2 * KV_LORA * (j + 1)], wuv_ref[j]) for j in range(MLA_HEADS // 2)]
        mixed.append(jnp.concatenate([o.astype(BF16) for o in o_a] + [ob_ref[bi]], axis=-1))
    y = [_mm(mixed[bi], wout_ref[...]) for bi in streams]
    x1 = [x_ref[bi] + _rms(y[bi], npost_ref[...]) for bi in streams]
    u = [_mm(_rms(x1[bi], nfpre_ref[...]), wup_ref[...]) for bi in streams]
    cw = cw_ref[...]
    if pad_front:
        live = (t * tm + lax.broadcasted_iota(jnp.int32, (tm, 1), 0)) >= pad_front
    f = []
    for bi in streams:
        a = u[bi][:, :d_ff]
        if pad_front:
            a = jnp.where(live, a, 0.0)
        a_scr[bi, 8:8 + tm, :] = a
        conv = cb_ref[...] + a * cw[keep:keep + 1]
        for j in range(keep):
            conv = conv + a_scr[bi, 8 - keep + j:8 - keep + j + tm, :] * cw[j:j + 1]
        f.append(_mm(_silu(conv) * u[bi][:, d_ff:], wdn_ref[...]))
    for bi in streams:
        x2 = x1[bi] + _rms(f[bi], nfpost_ref[...])
        xo_ref[bi] = _rms(x2, fin_ref[...]) if last else x2
        a_scr[bi, 0:8, :] = a_scr[bi, tm:tm + 8, :]

    @pl.when(t == nt - 1)
    def _():
        for bi in streams:
            bufo_ref[bi] = a_scr[bi, 8 - keep:8, :]


def _out_ffn(x, ol, ob, buf, w, final_norm, *, n_rows, pad_front, last):
    b, t, d = x.shape
    d_ff = buf.shape[-1]
    tm = _row_tile(n_rows, 192)
    assert tm >= 8 and n_rows <= t
    bb = FFN_STREAMS if b % FFN_STREAMS == 0 else 1
    row = lambda width: pl.BlockSpec((bb, tm, width), lambda i, j: (i, j, 0))
    per_b = pl.BlockSpec((bb, CONV_W - 1, d_ff), lambda i, j: (i, 0, 0))
    weights = (w['w_uv'], w['w_out'], w['norm_mix_post'], w['norm_ffn_pre'], w['w_ffn_up'], w['ffn_conv_w'],
               w['ffn_conv_b'], w['w_ffn_down'], w['norm_ffn_post'], final_norm)
    return pl.pallas_call(
        functools.partial(_out_ffn_body, bb=bb, tm=tm, pad_front=pad_front, last=last),
        grid=(b // bb, n_rows // tm),
        in_specs=[row(d), row(MLA_HEADS * KV_LORA), row(HG_WIDTH), per_b] + [_const_spec(a.shape) for a in weights],
        out_specs=(row(d), per_b),
        out_shape=(jax.ShapeDtypeStruct((b, t, d), F32), jax.ShapeDtypeStruct((b, CONV_W - 1, d_ff), F32)),
        scratch_shapes=[pltpu.VMEM((bb, tm + 8, d_ff), F32)],
        compiler_params=pltpu.CompilerParams(dimension_semantics=("parallel", "arbitrary"),
                                             vmem_limit_bytes=VMEM_LIMIT),
        name="out_ffn",
    )(x, ol, ob, buf, *weights)


def _rope_tables(pos):
    half = QK_ROPE // 2
    inv = ROPE_THETA ** (-jnp.arange(half, dtype=F32) / half)
    ang = pos.astype(F32)[:, None] * inv
    cos, sin = jnp.cos(ang), jnp.sin(ang)
    zero = jnp.zeros_like(cos)
    pad = jnp.zeros((pos.shape[0], LANES - QK_ROPE), F32)
    return (jnp.concatenate([cos, cos, pad], axis=1),
            jnp.concatenate([-sin, zero, pad], axis=1),
            jnp.concatenate([zero, sin, pad], axis=1),
            jnp.tile(cos, (1, MLA_HEADS)), jnp.tile(sin, (1, MLA_HEADS)))


def _layer_weights(l, w_in, norm_mix_pre, norm_mix_post, q_norm, kv_norm, w_uq, w_uk, w_uv, hg_out_norm, w_out,
                   norm_ffn_pre, norm_ffn_post, w_ffn_up, ffn_conv_w, ffn_conv_b, w_ffn_down):
    d = w_in.shape[1]
    head_w = Q_LORA + KV_LORA + QK_ROPE
    w_in_r = jnp.concatenate([w_in[l][:, :head_w], jnp.zeros((d, LANES - QK_ROPE), F32), w_in[l][:, head_w:]], axis=1)
    uq = w_uq[l].reshape(Q_LORA, MLA_HEADS, QK_NOPE + QK_ROPE)
    half = QK_ROPE // 2
    w_uq_r = jnp.concatenate([uq[:, :, :QK_NOPE].reshape(Q_LORA, -1),
                              uq[:, :, QK_NOPE:QK_NOPE + half].reshape(Q_LORA, -1),
                              uq[:, :, QK_NOPE + half:].reshape(Q_LORA, -1)], axis=1)
    src = jnp.arange(2 * MLA_HEADS * half)
    dst = (src % (MLA_HEADS * half)) // half * LANES + src // (MLA_HEADS * half) * half + src % half
    pe_spread = jnp.zeros((2 * MLA_HEADS * half, MLA_HEADS * LANES), F32).at[src, dst].set(1.0)
    uk = jnp.transpose(w_uk[l], (1, 2, 0))
    uk_pair = jnp.zeros((MLA_HEADS // 2, 2 * QK_NOPE, 2 * KV_LORA), F32)
    uk_pair = uk_pair.at[:, :QK_NOPE, :KV_LORA].set(uk[0::2]).at[:, QK_NOPE:, KV_LORA:].set(uk[1::2])
    uv = jnp.transpose(w_uv[l], (1, 0, 2))
    uv_pair = jnp.zeros((MLA_HEADS // 2, 2 * KV_LORA, 2 * V_HEAD), F32)
    uv_pair = uv_pair.at[:, :KV_LORA, :V_HEAD].set(uv[0::2]).at[:, KV_LORA:, V_HEAD:].set(uv[1::2])
    row = lambda a: a.reshape(1, -1)
    return {
        'w_in': w_in_r.astype(BF16), 'norm_mix_pre': row(norm_mix_pre[l]), 'norm_mix_post': row(norm_mix_post[l]),
        'q_norm': row(q_norm[l]), 'kv_norm': row(kv_norm[l]), 'w_uq': w_uq_r.astype(BF16),
        'w_uk': uk_pair.astype(BF16), 'w_uv': uv_pair.astype(BF16), 'pe_spread': pe_spread.astype(BF16),
        'hg_gain': row(hg_out_norm[l]), 'w_out': w_out[l].astype(BF16),
        'norm_ffn_pre': row(norm_ffn_pre[l]), 'norm_ffn_post': row(norm_ffn_post[l]),
        'w_ffn_up': w_ffn_up[l].astype(BF16), 'ffn_conv_w': ffn_conv_w[l], 'ffn_conv_b': row(ffn_conv_b[l]),
        'w_ffn_down': w_ffn_down[l].astype(BF16),
    }


def kernel(x_prompt, x_sample, cache_kv_latent, cache_k_rope, state_hgrn, state_ffn_conv, meta_tokens, w_in, norm_mix_pre, norm_mix_post, q_norm, kv_norm, w_uq, w_uk, w_uv, hg_lower_bounds, hg_out_norm, w_out, norm_ffn_pre, norm_ffn_post, w_ffn_up, ffn_conv_w, ffn_conv_b, w_ffn_down, final_norm):
    depth = w_in.shape[0]
    b, seq, d = x_prompt.shape
    bs, ts, _ = x_sample.shape
    past = cache_kv_latent.shape[2]
    d_ff = ffn_conv_w.shape[-1]
    lp = PAD_FRONT + N_META + seq

    lb_sm = jax.nn.softmax(hg_lower_bounds.astype(F32), axis=0)
    lb_all = jnp.cumsum(lb_sm, axis=0) - lb_sm[0]

    unit = IN_PROJ_ROWS * KEY_BLOCK // 128
    lp_alloc = -(-lp // unit) * unit
    meta = jnp.broadcast_to(meta_tokens[None], (b, N_META, d))
    xp = jnp.concatenate([jnp.zeros((b, PAD_FRONT, d), F32), meta, x_prompt,
                          jnp.zeros((b, lp_alloc - lp, d), F32)], axis=1)
    xs = x_sample
    tabs_p = _rope_tables(jnp.arange(lp_alloc, dtype=jnp.int32) - CHUNK)
    tabs_s = tuple(jnp.tile(a, (bs, 1)) for a in _rope_tables(past + jnp.arange(ts, dtype=jnp.int32)))
    fin = final_norm.reshape(1, -1)
    s0_p = jnp.zeros((b, HG_HEADS, HG_DK, HG_DV), F32)
    buf0_p = jnp.zeros((b, CONV_W - 1, d_ff), F32)

    outs = [[] for _ in range(8)]
    for l in range(depth):
        w = _layer_weights(l, w_in, norm_mix_pre, norm_mix_post, q_norm, kv_norm, w_uq, w_uk, w_uv, hg_out_norm,
                           w_out, norm_ffn_pre, norm_ffn_post, w_ffn_up, ffn_conv_w, ffn_conv_b, w_ffn_down)
        lb = lb_all[l].reshape(1, -1)
        gain = w['hg_gain']
        last = l == depth - 1

        qcat, kcat, kvt, kv, pe, zh = _in_proj(xp, tabs_p, w, n_valid=lp)
        ol = _mla_prompt(qcat, kcat, kvt, lp=lp)
        ob, s_new = _hgrn(zh, s0_p, lb, gain, n_rows=lp, cs=CHUNK, pad_front=PAD_FRONT)
        xp, buf = _out_ffn(xp, ol, ob, buf0_p, w, fin, n_rows=lp, pad_front=PAD_FRONT, last=last)
        for lst, a in zip(outs[:4], (kv[:, PAD_FRONT:lp], pe[:, PAD_FRONT:lp], s_new, buf)):
            lst.append(a)

        qcat, kcat, _, kv, pe, zh = _in_proj(xs.reshape(1, bs * ts, d), tabs_s, w, n_valid=bs * ts)
        qcat = jnp.swapaxes(qcat.reshape(MLA_HEADS, bs, ts, QCAT), 0, 1)
        kcat, kv, pe, zh = (a.reshape(bs, ts, -1) for a in (kcat, kv, pe, zh))
        ol = _mla_sample(qcat, kcat, cache_kv_latent[l], cache_k_rope[l])
        ob, s_new = _hgrn(zh, state_hgrn[l], lb, gain, n_rows=ts, cs=ts, pad_front=0)
        xs, buf = _out_ffn(xs, ol, ob, state_ffn_conv[l], w, fin, n_rows=ts, pad_front=0, last=last)
        for lst, a in zip(outs[4:], (kv, pe, s_new, buf)):
            lst.append(a)

    return (xp[:, PAD_FRONT + N_META:lp], xs) + tuple(jnp.stack(o) for o in outs)
```

```python
import functools

import jax
import jax.numpy as jnp
from jax import lax
from jax.experimental import pallas as pl
from jax.experimental.pallas import tpu as pltpu

F32 = jnp.float32
BF16 = jnp.bfloat16

CHUNK = 64
N_META = 16
PAD_FRONT = CHUNK - N_META
EPS = 1e-6
MLA_HEADS = 8
QK_NOPE = 64
QK_ROPE = 32
V_HEAD = 64
Q_LORA = 256
KV_LORA = 128
ROPE_THETA = 10000.0
HG_HEADS = 4
HG_DK = 128
HG_DV = 128
CONV_W = 3

HG_KW = HG_HEADS * HG_DK
HG_WIDTH = HG_HEADS * HG_DV
MLA_WIDTH = MLA_HEADS * V_HEAD
LANES = 128
KEY_BLOCK = 256
QCAT = KV_LORA + LANES
VT_ROWS = KV_LORA + 16
IN_PROJ_ROWS = 384
IN_PROJ_STREAMS = 2
Q_SCALE = (QK_NOPE + QK_ROPE) ** -0.5 * 1.4426950408889634
HGRN_FAST_RANGE = 80.0
FFN_STREAMS = 2
HGRN_STREAMS = 4
VMEM_LIMIT = 56 * 1024 * 1024


def _row_tile(n, target):
    if n <= target:
        return n
    best = None
    for t in range(16, target + 1, 16):
        if n % t == 0:
            best = t
    assert best is not None, (n, target)
    return best


def _const_spec(shape):
    nd = len(shape)
    return pl.BlockSpec(shape, lambda *_: (0,) * nd, pipeline_mode=pl.Buffered(1))


def _rms(x, w):
    return x * lax.rsqrt(jnp.mean(x * x, axis=-1, keepdims=True) + EPS) * w


def _silu(x):
    return x * jax.nn.sigmoid(x)


def _mm(a, b):
    return jnp.dot(a.astype(BF16), b.astype(BF16), preferred_element_type=F32)


def _mm_nt(a, b):
    return lax.dot_general(a.astype(BF16), b.astype(BF16), (((1,), (1,)), ((), ())),
                           preferred_element_type=F32)


def _mm_tn(a, b):
    return lax.dot_general(a.astype(BF16), b.astype(BF16), (((0,), (0,)), ((), ())),
                           preferred_element_type=F32)


def _in_proj_body(x_ref, cos_ref, sn_ref, sp_ref, cosq_ref, sinq_ref, npre_ref, win_ref, qn_ref, kvn_ref, wuq_ref,
                  wuk_ref, spread_ref, qcat_ref, kcat_ref, kvt_ref, kv_ref, pe_ref, zh_ref, *, bb, tm, n_valid):
    cos, sn, sp = cos_ref[...], sn_ref[...], sp_ref[...]

    def rope(v):
        return v * cos + pltpu.roll(v, LANES - QK_ROPE // 2, 1) * sn + pltpu.roll(v, QK_ROPE // 2, 1) * sp

    live = (pl.program_id(1) * tm + lax.broadcasted_iota(jnp.int32, (tm, 1), 0)) < n_valid
    h = [_rms(jnp.where(live, x_ref[bi], 0.0), npre_ref[...]) for bi in range(bb)]
    zs = [_mm(h[bi], win_ref[...]) for bi in range(bb)]
    nope_w = MLA_HEADS * QK_NOPE
    for bi, z in enumerate(zs):
        zh_ref[bi] = z[:, Q_LORA + KV_LORA + LANES:]
        kv_lat = _rms(z[:, Q_LORA:Q_LORA + KV_LORA], kvn_ref[...])
        k_pe = rope(z[:, Q_LORA + KV_LORA:Q_LORA + KV_LORA + LANES])
        kv_ref[bi] = kv_lat
        pe_ref[bi] = k_pe[:, :QK_ROPE]
        kcat_ref[bi, :, :KV_LORA] = kv_lat.astype(BF16)
        kcat_ref[bi, :, KV_LORA:] = k_pe.astype(BF16)
        kvt_ref[bi, :KV_LORA, :] = kv_lat.T.astype(BF16)
        kvt_ref[bi, KV_LORA:, :] = jnp.ones((VT_ROWS - KV_LORA, tm), BF16)

        cqn = _rms(z[:, :Q_LORA], qn_ref[...])
        q = _mm(cqn, wuq_ref[...]) * Q_SCALE
        x1, x2 = q[:, nope_w:nope_w + LANES], q[:, nope_w + LANES:]
        cq, sq = cosq_ref[...], sinq_ref[...]
        roped = jnp.concatenate([x1 * cq - x2 * sq, x1 * sq + x2 * cq], axis=1)
        pe_heads = _mm(roped, spread_ref[...])
        for j in range(MLA_HEADS // 2):
            ql = _mm(q[:, 2 * QK_NOPE * j:2 * QK_NOPE * (j + 1)], wuk_ref[j])
            for hh in range(2):
                hd = 2 * j + hh
                qcat_ref[bi, hd, :, :KV_LORA] = ql[:, KV_LORA * hh:KV_LORA * (hh + 1)].astype(BF16)
                qcat_ref[bi, hd, :, KV_LORA:] = pe_heads[:, LANES * hd:LANES * (hd + 1)].astype(BF16)


def _in_proj(x, tabs, w, *, n_valid):
    bx, t, d = x.shape
    tm = _row_tile(t, IN_PROJ_ROWS)
    assert tm % LANES == 0 or tm == t
    nt = t // tm
    bb = IN_PROJ_STREAMS if bx % IN_PROJ_STREAMS == 0 else 1
    row = lambda shape: pl.BlockSpec((bb, tm) + shape, lambda b, i: (b, i) + (0,) * len(shape))
    tab = pl.BlockSpec((tm, LANES), lambda b, i: (i, 0))
    zw = 2 * HG_KW + 2 * HG_WIDTH
    out_shape = (
        jax.ShapeDtypeStruct((bx, MLA_HEADS, t, QCAT), BF16),
        jax.ShapeDtypeStruct((bx, t, QCAT), BF16),
        jax.ShapeDtypeStruct((bx, VT_ROWS, t), BF16),
        jax.ShapeDtypeStruct((bx, t, KV_LORA), F32),
        jax.ShapeDtypeStruct((bx, t, QK_ROPE), F32),
        jax.ShapeDtypeStruct((bx, t, zw), F32),
    )
    out_specs = (
        pl.BlockSpec((bb, MLA_HEADS, tm, QCAT), lambda b, i: (b, 0, i, 0)),
        row((QCAT,)),
        pl.BlockSpec((bb, VT_ROWS, tm), lambda b, i: (b, 0, i)),
        row((KV_LORA,)), row((QK_ROPE,)), row((zw,)),
    )
    weights = (w['norm_mix_pre'], w['w_in'], w['q_norm'], w['kv_norm'], w['w_uq'], w['w_uk'], w['pe_spread'])
    return pl.pallas_call(
        functools.partial(_in_proj_body, bb=bb, tm=tm, n_valid=n_valid),
        grid=(bx // bb, nt),
        in_specs=[row((d,))] + [tab] * len(tabs) + [_const_spec(a.shape) for a in weights],
        out_specs=out_specs,
        out_shape=out_shape,
        compiler_params=pltpu.CompilerParams(dimension_semantics=("parallel", "parallel"),
                                             vmem_limit_bytes=VMEM_LIMIT),
        name="in_proj",
    )(x, *tabs, *weights)


def _mla_prompt_body(*refs, qt, j0, aliased):
    q_ref, k_ref, vt_ref = refs[:3]
    o_ref, m_scr, acc_scr, bias_scr = refs[4:] if aliased else refs[3:]
    j = pl.program_id(1) + j0
    base = j * KEY_BLOCK
    m_scr[...] = jnp.full(m_scr.shape, -jnp.inf, F32)
    acc_scr[...] = jnp.zeros(acc_scr.shape, F32)
    row = lax.broadcasted_iota(jnp.int32, (KEY_BLOCK, qt), 0)
    col = lax.broadcasted_iota(jnp.int32, (KEY_BLOCK, qt), 1)
    real = row >= PAD_FRONT
    bias_scr[0] = jnp.where(real, 0.0, -jnp.inf)
    visible = ((base + row) // CHUNK) <= ((base + col) // CHUNK)
    bias_scr[1] = jnp.where(visible, jnp.where(base + row >= PAD_FRONT, 0.0, -jnp.inf), -jnp.inf)
    bias_scr[2] = jnp.zeros((KEY_BLOCK, qt), F32)

    def keys_of(kb):
        start = kb * KEY_BLOCK
        return pl.ds(start if isinstance(kb, int) else pl.multiple_of(start, KEY_BLOCK), KEY_BLOCK)

    def scores(kb):
        return _mm_nt(k_ref[0, keys_of(kb), :], q_ref[0].reshape(MLA_HEADS * qt, QCAT))

    def absorb(s, kb, bias_idx):
        if bias_idx is not None:
            s = s + jnp.concatenate([bias_scr[bias_idx]] * MLA_HEADS, axis=1)
        m_old = m_scr[...]
        m_new = jnp.maximum(m_old, jnp.max(s, axis=0, keepdims=True))
        p = jnp.exp2(s - m_new)
        acc_scr[...] = jnp.exp2(m_old - m_new) * acc_scr[...] + _mm(vt_ref[0, :, keys_of(kb)], p)
        m_scr[...] = m_new

    def pair(a, b, bias_a, bias_b):
        s_a, s_b = scores(a), scores(b)
        absorb(s_a, a, bias_a)
        absorb(s_b, b, bias_b)

    first_mask, diag_mask, no_mask = 0, 1, 2

    @pl.when(j == 0)
    def _():
        absorb(scores(0), 0, diag_mask)

    @pl.when(j > 0)
    def _():
        pair(j, j - 1, diag_mask, jnp.where(j == 1, first_mask, no_mask))
        n_inner = jnp.maximum(j - 2, 0) // 2

        def inner(p, carry):
            pair(j - 2 * p, j - 2 * p - 1, None, None)
            return carry

        lax.fori_loop(1, n_inner + 1, inner, 0)
        left = j - 1 - 2 * n_inner

        @pl.when(left == 1)
        def _():
            absorb(scores(0), 0, first_mask)

        @pl.when(left == 2)
        def _():
            pair(1, 0, None, first_mask)

    for hd in range(MLA_HEADS):
        acc = acc_scr[:, qt * hd:qt * (hd + 1)]
        o_t = acc[:KV_LORA] * (1.0 / acc[KV_LORA:KV_LORA + 1])
        o_ref[0, :, KV_LORA * hd:KV_LORA * (hd + 1)] = o_t.T.astype(BF16)


def _mla_prompt_call(qcat, kcat, kvt, prev, *, qt, j0, steps):
    b, _, t, _ = qcat.shape
    per_block = KEY_BLOCK // qt
    in_specs = [pl.BlockSpec((1, MLA_HEADS, qt, QCAT), lambda i, j: (i, 0, (j + j0) * per_block, 0)),
                pl.BlockSpec((1, t, QCAT), lambda i, j: (i, 0, 0)),
                pl.BlockSpec((1, VT_ROWS, t), lambda i, j: (i, 0, 0))]
    args = (qcat, kcat, kvt)
    if prev is not None:
        in_specs.append(pl.BlockSpec(memory_space=pl.ANY))
        args += (prev,)
    return pl.pallas_call(
        functools.partial(_mla_prompt_body, qt=qt, j0=j0, aliased=prev is not None),
        grid=(b, steps),
        in_specs=in_specs,
        out_specs=pl.BlockSpec((1, qt, MLA_HEADS * KV_LORA), lambda i, j: (i, (j + j0) * per_block, 0)),
        out_shape=jax.ShapeDtypeStruct((b, t, MLA_HEADS * KV_LORA), BF16),
        input_output_aliases={3: 0} if prev is not None else {},
        scratch_shapes=[pltpu.VMEM((1, MLA_HEADS * qt), F32),
                        pltpu.VMEM((VT_ROWS, MLA_HEADS * qt), F32),
                        pltpu.VMEM((3, KEY_BLOCK, qt), F32)],
        compiler_params=pltpu.CompilerParams(dimension_semantics=("parallel", "parallel"),
                                             vmem_limit_bytes=VMEM_LIMIT),
        name="mla_prompt",
    )(*args)


def _mla_prompt(qcat, kcat, kvt, *, lp):
    t = qcat.shape[2]
    assert t % KEY_BLOCK == 0 and lp <= t
    full, rest = divmod(lp, KEY_BLOCK)
    if rest > LANES:
        full, rest = full + 1, 0
    out = _mla_prompt_call(qcat, kcat, kvt, None, qt=KEY_BLOCK, j0=0, steps=full) if full else None
    if rest:
        out = _mla_prompt_call(qcat, kcat, kvt, out, qt=LANES, j0=full, steps=1)
    return out


def _mla_sample_body(q_ref, k_ref, ckv_ref, cpe_ref, o_ref, *, ts):
    rows = MLA_HEADS * ts
    q = q_ref[0].reshape(rows, QCAT)
    ckv = ckv_ref[0].astype(BF16)
    knew = k_ref[0]
    s_c = _mm_nt(q[:, :KV_LORA], ckv) + _mm_nt(q[:, KV_LORA:KV_LORA + QK_ROPE], cpe_ref[0])
    s_n = _mm_nt(q, knew)
    m = jnp.maximum(jnp.max(s_c, axis=-1, keepdims=True), jnp.max(s_n, axis=-1, keepdims=True))
    p_c = jnp.exp2(s_c - m)
    p_n = jnp.exp2(s_n - m)
    l = jnp.sum(p_c, axis=-1, keepdims=True) + jnp.sum(p_n, axis=-1, keepdims=True)
    o = (_mm(p_c, ckv) + _mm(p_n, knew[:, :KV_LORA])) / l
    for hd in range(MLA_HEADS):
        o_ref[0, :, KV_LORA * hd:KV_LORA * (hd + 1)] = o[ts * hd:ts * (hd + 1)].astype(BF16)


def _mla_sample(qcat, kcat, cache_kv, cache_pe):
    b, _, ts, _ = qcat.shape
    past = cache_kv.shape[1]
    return pl.pallas_call(
        functools.partial(_mla_sample_body, ts=ts),
        grid=(b,),
        in_specs=[pl.BlockSpec((1, MLA_HEADS, ts, QCAT), lambda i: (i, 0, 0, 0)),
                  pl.BlockSpec((1, ts, QCAT), lambda i: (i, 0, 0)),
                  pl.BlockSpec((1, past, KV_LORA), lambda i: (i, 0, 0)),
                  pl.BlockSpec((1, past, QK_ROPE), lambda i: (i, 0, 0))],
        out_specs=pl.BlockSpec((1, ts, MLA_HEADS * KV_LORA), lambda i: (i, 0, 0)),
        out_shape=jax.ShapeDtypeStruct((b, ts, MLA_HEADS * KV_LORA), BF16),
        compiler_params=pltpu.CompilerParams(dimension_semantics=("parallel",),
                                             vmem_limit_bytes=VMEM_LIMIT),
        name="mla_sample",
    )(qcat, kcat, cache_kv, cache_pe)


def _hgrn_body(zh_ref, s0_ref, lb_ref, gain_ref, ob_ref, sout_ref, st_ref, g_scr, q_scr, k_scr, v_scr, o_scr,
               *, bb, cs, pad_front):
    c = pl.program_id(1)
    nc = pl.num_programs(1)
    streams = range(bb)
    heads = [slice(HG_DK * hd, HG_DK * (hd + 1)) for hd in range(HG_HEADS)]

    @pl.when(c == 0)
    def _():
        for bi in streams:
            for hd in range(HG_HEADS):
                st_ref[bi, hd] = s0_ref[bi, hd].T

    hq = zh_ref[:, :, :HG_KW]
    hf = zh_ref[:, :, HG_KW:2 * HG_KW]
    v = zh_ref[:, :, 2 * HG_KW:2 * HG_KW + HG_WIDTH]
    hg = zh_ref[:, :, 2 * HG_KW + HG_WIDTH:]
    lb = lb_ref[...]
    e = jnp.exp(-jnp.abs(hf))
    r = 1.0 / (1.0 + e)
    pos = hf >= 0.0
    f = lb + (1.0 - lb) * jnp.where(pos, r, e * r)
    gl = jnp.log(f)
    k = (1.0 - lb) * jnp.where(pos, e * r, r)
    q = _silu(hq)
    if pad_front:
        live = (c * cs + lax.broadcasted_iota(jnp.int32, (cs, 1), 0)) >= pad_front
        gl = jnp.where(live, gl, 0.0)
        k = jnp.where(live, k, 0.0)
        q = jnp.where(live, q, 0.0)
    t_idx = lax.broadcasted_iota(jnp.int32, (cs, cs), 0)
    s_idx = lax.broadcasted_iota(jnp.int32, (cs, cs), 1)
    causal = t_idx >= s_idx
    tri = causal.astype(F32)
    g = jnp.stack([jnp.dot(tri, gl[bi], precision=lax.Precision.HIGHEST, preferred_element_type=F32)
                   for bi in streams])
    g_tot = g[:, cs - 1:cs, :]
    qt = q * jnp.exp(g)
    khat = k * jnp.exp(g_tot - g)
    kt = k * jnp.exp(-g)
    a = [[jnp.where(causal, _mm_nt(qt[bi][:, sl], kt[bi][:, sl]), 0.0) for sl in heads] for bi in streams]
    for bi in streams:
        o_scr[bi] = jnp.concatenate([_mm(a[bi][hd], v[bi][:, sl]) for hd, sl in enumerate(heads)], axis=-1)

    for bi in streams:
        @pl.when(jnp.max(-g_tot[bi]) > HGRN_FAST_RANGE)
        def _(bi=bi):
            g_scr[...] = g[bi]
            q_scr[...] = q[bi]
            k_scr[...] = k[bi]
            v_scr[...] = v[bi]
            o_scr[bi] = jnp.zeros((cs, HG_WIDTH), F32)
            t_col = lax.broadcasted_iota(jnp.int32, (cs, 1), 0)

            def cols(i, carry):
                base = pl.multiple_of(i * 8, 8)
                g8, k8, v8 = (ref[pl.ds(base, 8), :] for ref in (g_scr, k_scr, v_scr))
                for j in range(8):
                    dec = jnp.exp(jnp.where(t_col >= base + j, g_scr[...] - g8[j:j + 1], -jnp.inf))
                    pr = q_scr[...] * dec * k8[j:j + 1]
                    for sl in heads:
                        o_scr[bi, :, sl] += jnp.sum(pr[:, sl], axis=-1, keepdims=True) * v8[j:j + 1, sl]
                return carry

            lax.fori_loop(0, cs // 8, cols, 0)

    gain = gain_ref[...]
    o_inter = [[_mm_nt(qt[bi][:, sl], st_ref[bi, hd]) for hd, sl in enumerate(heads)] for bi in streams]
    kv_new = [[_mm_tn(v[bi][:, sl], khat[bi][:, sl]) for sl in heads] for bi in streams]
    decay = jnp.exp(g_tot)
    for bi in streams:
        for hd, sl in enumerate(heads):
            st_ref[bi, hd] = decay[bi][:, sl] * st_ref[bi, hd] + kv_new[bi][hd]
        o = [_rms(o_scr[bi, :, sl] + o_inter[bi][hd], gain) for hd, sl in enumerate(heads)]
        ob_ref[bi] = (jnp.concatenate(o, axis=-1) * _silu(hg[bi])).astype(BF16)

    @pl.when(c == nc - 1)
    def _():
        for bi in streams:
            for hd in range(HG_HEADS):
                sout_ref[bi, hd] = st_ref[bi, hd].T


def _hgrn(zh, s0, lb, gain, *, n_rows, cs, pad_front):
    b, t, zw = zh.shape
    assert n_rows % cs == 0 and n_rows <= t
    bb = HGRN_STREAMS if b % HGRN_STREAMS == 0 else 1
    return pl.pallas_call(
        functools.partial(_hgrn_body, bb=bb, cs=cs, pad_front=pad_front),
        grid=(b // bb, n_rows // cs),
        in_specs=[pl.BlockSpec((bb, cs, zw), lambda i, c: (i, c, 0)),
                  pl.BlockSpec((bb, HG_HEADS, HG_DK, HG_DV), lambda i, c: (i, 0, 0, 0)),
                  _const_spec(lb.shape), _const_spec(gain.shape)],
        out_specs=(pl.BlockSpec((bb, cs, HG_WIDTH), lambda i, c: (i, c, 0)),
                   pl.BlockSpec((bb, HG_HEADS, HG_DK, HG_DV), lambda i, c: (i, 0, 0, 0))),
        out_shape=(jax.ShapeDtypeStruct((b, t, HG_WIDTH), BF16),
                   jax.ShapeDtypeStruct((b, HG_HEADS, HG_DK, HG_DV), F32)),
        scratch_shapes=[pltpu.VMEM((bb, HG_HEADS, HG_DV, HG_DK), F32)]
        + [pltpu.VMEM((cs, HG_KW), F32) for _ in range(4)] + [pltpu.VMEM((bb, cs, HG_KW), F32)],
        compiler_params=pltpu.CompilerParams(dimension_semantics=("parallel", "arbitrary"),
                                             vmem_limit_bytes=VMEM_LIMIT),
        name="hgrn",
    )(zh, s0, lb, gain)


def _out_ffn_body(x_ref, ol_ref, ob_ref, buf_ref, wuv_ref, wout_ref, npost_ref, nfpre_ref, wup_ref, cw_ref,
                  cb_ref, wdn_ref, nfpost_ref, fin_ref, xo_ref, bufo_ref, a_scr, *, bb, tm, pad_front, last):
    t = pl.program_id(1)
    nt = pl.num_programs(1)
    d_ff = cw_ref.shape[1]
    keep = CONV_W - 1
    streams = range(bb)

    @pl.when(t == 0)
    def _():
        for bi in streams:
            a_scr[bi, 0:8, :] = jnp.zeros((8, d_ff), F32)
            a_scr[bi, 8 - keep:8, :] = buf_ref[bi]

    mixed = []
    for bi in streams:
        ol = ol_ref[bi]
        o_a = [_mm(ol[:, 2 * KV_LORA * j:2 * KV_LORA * (j + 1)], wuv_ref[j]) for j in range(MLA_HEADS // 2)]
        mixed.append(jnp.concatenate([o.astype(BF16) for o in o_a] + [ob_ref[bi]], axis=-1))
    y = [_mm(mixed[bi], wout_ref[...]) for bi in streams]
    x1 = [x_ref[bi] + _rms(y[bi], npost_ref[...]) for bi in streams]
    u = [_mm(_rms(x1[bi], nfpre_ref[...]), wup_ref[...]) for bi in streams]
    cw = cw_ref[...]
    if pad_front:
        live = (t * tm + lax.broadcasted_iota(jnp.int32, (tm, 1), 0)) >= pad_front
    f = []
    for bi in streams:
        a = u[bi][:, :d_ff]
        if pad_front:
            a = jnp.where(live, a, 0.0)
        a_scr[bi, 8:8 + tm, :] = a
        conv = cb_ref[...] + a * cw[keep:keep + 1]
        for j in range(keep):
            conv = conv + a_scr[bi, 8 - keep + j:8 - keep + j + tm, :] * cw[j:j + 1]
        f.append(_mm(_silu(conv) * u[bi][:, d_ff:], wdn_ref[...]))
    for bi in streams:
        x2 = x1[bi] + _rms(f[bi], nfpost_ref[...])
        xo_ref[bi] = _rms(x2, fin_ref[...]) if last else x2
        a_scr[bi, 0:8, :] = a_scr[bi, tm:tm + 8, :]

    @pl.when(t == nt - 1)
    def _():
        for bi in streams:
            bufo_ref[bi] = a_scr[bi, 8 - keep:8, :]


def _out_ffn(x, ol, ob, buf, w, final_norm, *, n_rows, pad_front, last):
    b, t, d = x.shape
    d_ff = buf.shape[-1]
    tm = _row_tile(n_rows, 192)
    assert tm >= 8 and n_rows <= t
    bb = FFN_STREAMS if b % FFN_STREAMS == 0 else 1
    row = lambda width: pl.BlockSpec((bb, tm, width), lambda i, j: (i, j, 0))
    per_b = pl.BlockSpec((bb, CONV_W - 1, d_ff), lambda i, j: (i, 0, 0))
    weights = (w['w_uv'], w['w_out'], w['norm_mix_post'], w['norm_ffn_pre'], w['w_ffn_up'], w['ffn_conv_w'],
               w['ffn_conv_b'], w['w_ffn_down'], w['norm_ffn_post'], final_norm)
    return pl.pallas_call(
        functools.partial(_out_ffn_body, bb=bb, tm=tm, pad_front=pad_front, last=last),
        grid=(b // bb, n_rows // tm),
        in_specs=[row(d), row(MLA_HEADS * KV_LORA), row(HG_WIDTH), per_b] + [_const_spec(a.shape) for a in weights],
        out_specs=(row(d), per_b),
        out_shape=(jax.ShapeDtypeStruct((b, t, d), F32), jax.ShapeDtypeStruct((b, CONV_W - 1, d_ff), F32)),
        scratch_shapes=[pltpu.VMEM((bb, tm + 8, d_ff), F32)],
        compiler_params=pltpu.CompilerParams(dimension_semantics=("parallel", "arbitrary"),
                                             vmem_limit_bytes=VMEM_LIMIT),
        name="out_ffn",
    )(x, ol, ob, buf, *weights)


def _rope_tables(pos):
    half = QK_ROPE // 2
    inv = ROPE_THETA ** (-jnp.arange(half, dtype=F32) / half)
    ang = pos.astype(F32)[:, None] * inv
    cos, sin = jnp.cos(ang), jnp.sin(ang)
    zero = jnp.zeros_like(cos)
    pad = jnp.zeros((pos.shape[0], LANES - QK_ROPE), F32)
    return (jnp.concatenate([cos, cos, pad], axis=1),
            jnp.concatenate([-sin, zero, pad], axis=1),
            jnp.concatenate([zero, sin, pad], axis=1),
            jnp.tile(cos, (1, MLA_HEADS)), jnp.tile(sin, (1, MLA_HEADS)))


def _layer_weights(l, w_in, norm_mix_pre, norm_mix_post, q_norm, kv_norm, w_uq, w_uk, w_uv, hg_out_norm, w_out,
                   norm_ffn_pre, norm_ffn_post, w_ffn_up, ffn_conv_w, ffn_conv_b, w_ffn_down):
    d = w_in.shape[1]
    head_w = Q_LORA + KV_LORA + QK_ROPE
    w_in_r = jnp.concatenate([w_in[l][:, :head_w], jnp.zeros((d, LANES - QK_ROPE), F32), w_in[l][:, head_w:]], axis=1)
    uq = w_uq[l].reshape(Q_LORA, MLA_HEADS, QK_NOPE + QK_ROPE)
    half = QK_ROPE // 2
    w_uq_r = jnp.concatenate([uq[:, :, :QK_NOPE].reshape(Q_LORA, -1),
                              uq[:, :, QK_NOPE:QK_NOPE + half].reshape(Q_LORA, -1),
                              uq[:, :, QK_NOPE + half:].reshape(Q_LORA, -1)], axis=1)
    src = jnp.arange(2 * MLA_HEADS * half)
    dst = (src % (MLA_HEADS * half)) // half * LANES + src // (MLA_HEADS * half) * half + src % half
    pe_spread = jnp.zeros((2 * MLA_HEADS * half, MLA_HEADS * LANES), F32).at[src, dst].set(1.0)
    uk = jnp.transpose(w_uk[l], (1, 2, 0))
    uk_pair = jnp.zeros((MLA_HEADS // 2, 2 * QK_NOPE, 2 * KV_LORA), F32)
    uk_pair = uk_pair.at[:, :QK_NOPE, :KV_LORA].set(uk[0::2]).at[:, QK_NOPE:, KV_LORA:].set(uk[1::2])
    uv = jnp.transpose(w_uv[l], (1, 0, 2))
    uv_pair = jnp.zeros((MLA_HEADS // 2, 2 * KV_LORA, 2 * V_HEAD), F32)
    uv_pair = uv_pair.at[:, :KV_LORA, :V_HEAD].set(uv[0::2]).at[:, KV_LORA:, V_HEAD:].set(uv[1::2])
    row = lambda a: a.reshape(1, -1)
    return {
        'w_in': w_in_r.astype(BF16), 'norm_mix_pre': row(norm_mix_pre[l]), 'norm_mix_post': row(norm_mix_post[l]),
        'q_norm': row(q_norm[l]), 'kv_norm': row(kv_norm[l]), 'w_uq': w_uq_r.astype(BF16),
        'w_uk': uk_pair.astype(BF16), 'w_uv': uv_pair.astype(BF16), 'pe_spread': pe_spread.astype(BF16),
        'hg_gain': row(hg_out_norm[l]), 'w_out': w_out[l].astype(BF16),
        'norm_ffn_pre': row(norm_ffn_pre[l]), 'norm_ffn_post': row(norm_ffn_post[l]),
        'w_ffn_up': w_ffn_up[l].astype(BF16), 'ffn_conv_w': ffn_conv_w[l], 'ffn_conv_b': row(ffn_conv_b[l]),
        'w_ffn_down': w_ffn_down[l].astype(BF16),
    }


def kernel(x_prompt, x_sample, cache_kv_latent, cache_k_rope, state_hgrn, state_ffn_conv, meta_tokens, w_in, norm_mix_pre, norm_mix_post, q_norm, kv_norm, w_uq, w_uk, w_uv, hg_lower_bounds, hg_out_norm, w_out, norm_ffn_pre, norm_ffn_post, w_ffn_up, ffn_conv_w, ffn_conv_b, w_ffn_down, final_norm):
    depth = w_in.shape[0]
    b, seq, d = x_prompt.shape
    bs, ts, _ = x_sample.shape
    past = cache_kv_latent.shape[2]
    d_ff = ffn_conv_w.shape[-1]
    lp = PAD_FRONT + N_META + seq

    lb_sm = jax.nn.softmax(hg_lower_bounds.astype(F32), axis=0)
    lb_all = jnp.cumsum(lb_sm, axis=0) - lb_sm[0]

    unit = IN_PROJ_ROWS * KEY_BLOCK // 128
    lp_alloc = -(-lp // unit) * unit
    meta = jnp.broadcast_to(meta_tokens[None], (b, N_META, d))
    xp = jnp.concatenate([jnp.zeros((b, PAD_FRONT, d), F32), meta, x_prompt,
                          jnp.zeros((b, lp_alloc - lp, d), F32)], axis=1)
    xs = x_sample
    tabs_p = _rope_tables(jnp.arange(lp_alloc, dtype=jnp.int32) - CHUNK)
    tabs_s = tuple(jnp.tile(a, (bs, 1)) for a in _rope_tables(past + jnp.arange(ts, dtype=jnp.int32)))
    fin = final_norm.reshape(1, -1)
    s0_p = jnp.zeros((b, HG_HEADS, HG_DK, HG_DV), F32)
    buf0_p = jnp.zeros((b, CONV_W - 1, d_ff), F32)

    outs = [[] for _ in range(8)]
    for l in range(depth):
        w = _layer_weights(l, w_in, norm_mix_pre, norm_mix_post, q_norm, kv_norm, w_uq, w_uk, w_uv, hg_out_norm,
                           w_out, norm_ffn_pre, norm_ffn_post, w_ffn_up, ffn_conv_w, ffn_conv_b, w_ffn_down)
        lb = lb_all[l].reshape(1, -1)
        gain = w['hg_gain']
        last = l == depth - 1

        qcat, kcat, kvt, kv, pe, zh = _in_proj(xp, tabs_p, w, n_valid=lp)
        ol = _mla_prompt(qcat, kcat, kvt, lp=lp)
        ob, s_new = _hgrn(zh, s0_p, lb, gain, n_rows=lp, cs=CHUNK, pad_front=PAD_FRONT)
        xp, buf = _out_ffn(xp, ol, ob, buf0_p, w, fin, n_rows=lp, pad_front=PAD_FRONT, last=last)
        for lst, a in zip(outs[:4], (kv[:, PAD_FRONT:lp], pe[:, PAD_FRONT:lp], s_new, buf)):
            lst.append(a)

        qcat, kcat, _, kv, pe, zh = _in_proj(xs.reshape(1, bs * ts, d), tabs_s, w, n_valid=bs * ts)
        qcat = jnp.swapaxes(qcat.reshape(MLA_HEADS, bs, ts, QCAT), 0, 1)
        kcat, kv, pe, zh = (a.reshape(bs, ts, -1) for a in (kcat, kv, pe, zh))
        ol = _mla_sample(qcat, kcat, cache_kv_latent[l], cache_k_rope[l])
        ob, s_new = _hgrn(zh, state_hgrn[l], lb, gain, n_rows=ts, cs=ts, pad_front=0)
        xs, buf = _out_ffn(xs, ol, ob, state_ffn_conv[l], w, fin, n_rows=ts, pad_front=0, last=last)
        for lst, a in zip(outs[4:], (kv, pe, s_new, buf)):
            lst.append(a)

    return (xp[:, PAD_FRONT + N_META:lp], xs) + tuple(jnp.stack(o) for o in outs)
```

```python
import functools

import jax
import jax.numpy as jnp
from jax import lax
from jax.experimental import pallas as pl
from jax.experimental.pallas import tpu as pltpu

F32 = jnp.float32
BF16 = jnp.bfloat16

CHUNK = 64
N_META = 16
PAD_FRONT = CHUNK - N_META
EPS = 1e-6
MLA_HEADS = 8
QK_NOPE = 64
QK_ROPE = 32
V_HEAD = 64
Q_LORA = 256
KV_LORA = 128
ROPE_THETA = 10000.0
HG_HEADS = 4
HG_DK = 128
HG_DV = 128
CONV_W = 3

HG_KW = HG_HEADS * HG_DK
HG_WIDTH = HG_HEADS * HG_DV
MLA_WIDTH = MLA_HEADS * V_HEAD
LANES = 128
KEY_BLOCK = 256
QCAT = KV_LORA + LANES
VT_ROWS = KV_LORA + 16
IN_PROJ_ROWS = 384
IN_PROJ_STREAMS = 2
Q_SCALE = (QK_NOPE + QK_ROPE) ** -0.5 * 1.4426950408889634
HGRN_FAST_RANGE = 80.0
FFN_STREAMS = 2
HGRN_STREAMS = 4
VMEM_LIMIT = 56 * 1024 * 1024


def _row_tile(n, target):
    if n <= target:
        return n
    best = None
    for t in range(16, target + 1, 16):
        if n % t == 0:
            best = t
    assert best is not None, (n, target)
    return best


def _const_spec(shape):
    nd = len(shape)
    return pl.BlockSpec(shape, lambda *_: (0,) * nd, pipeline_mode=pl.Buffered(1))


def _rms(x, w):
    return x * lax.rsqrt(jnp.mean(x * x, axis=-1, keepdims=True) + EPS) * w


def _silu(x):
    return x * jax.nn.sigmoid(x)


def _mm(a, b):
    return jnp.dot(a.astype(BF16), b.astype(BF16), preferred_element_type=F32)


def _mm_nt(a, b):
    return lax.dot_general(a.astype(BF16), b.astype(BF16), (((1,), (1,)), ((), ())),
                           preferred_element_type=F32)


def _mm_tn(a, b):
    return lax.dot_general(a.astype(BF16), b.astype(BF16), (((0,), (0,)), ((), ())),
                           preferred_element_type=F32)


def _in_proj_body(x_ref, cos_ref, sn_ref, sp_ref, cosq_ref, sinq_ref, npre_ref, win_ref, qn_ref, kvn_ref, wuq_ref,
                  wuk_ref, spread_ref, lb_ref, qcat_ref, kcat_ref, kvt_ref, kv_ref, pe_ref, zh_ref,
                  *, bb, tm, n_valid, pad_front):
    cos, sn, sp = cos_ref[...], sn_ref[...], sp_ref[...]
    lb = lb_ref[...]

    def rope(v):
        return v * cos + pltpu.roll(v, LANES - QK_ROPE // 2, 1) * sn + pltpu.roll(v, QK_ROPE // 2, 1) * sp

    row = pl.program_id(1) * tm + lax.broadcasted_iota(jnp.int32, (tm, 1), 0)
    live = row < n_valid
    real = row >= pad_front
    h = [_rms(jnp.where(live, x_ref[bi], 0.0), npre_ref[...]) for bi in range(bb)]
    zs = [_mm(h[bi], win_ref[...]) for bi in range(bb)]
    nope_w = MLA_HEADS * QK_NOPE
    for bi, z in enumerate(zs):
        off = Q_LORA + KV_LORA + LANES
        hf = z[:, off + HG_KW:off + 2 * HG_KW]
        e = jnp.exp(-jnp.abs(hf))
        r = 1.0 / (1.0 + e)
        f = lb + (1.0 - lb) * jnp.where(hf >= 0.0, r, e * r)
        zh_ref[bi, :, :HG_KW] = jnp.where(real, _silu(z[:, off:off + HG_KW]), 0.0)
        zh_ref[bi, :, HG_KW:2 * HG_KW] = jnp.where(real, jnp.log(f), 0.0)
        zh_ref[bi, :, 2 * HG_KW:3 * HG_KW] = jnp.where(real, (1.0 - lb) * jnp.where(hf >= 0.0, e * r, r), 0.0)
        zh_ref[bi, :, 3 * HG_KW:] = z[:, off + 2 * HG_KW:]
        kv_lat = _rms(z[:, Q_LORA:Q_LORA + KV_LORA], kvn_ref[...])
        k_pe = rope(z[:, Q_LORA + KV_LORA:Q_LORA + KV_LORA + LANES])
        kv_ref[bi] = kv_lat
        pe_ref[bi] = k_pe[:, :QK_ROPE]
        kcat_ref[bi, :, :KV_LORA] = kv_lat.astype(BF16)
        kcat_ref[bi, :, KV_LORA:] = k_pe.astype(BF16)
        kvt_ref[bi, :KV_LORA, :] = kv_lat.T.astype(BF16)
        kvt_ref[bi, KV_LORA:, :] = jnp.ones((VT_ROWS - KV_LORA, tm), BF16)

        cqn = _rms(z[:, :Q_LORA], qn_ref[...])
        q = _mm(cqn, wuq_ref[...]) * Q_SCALE
        x1, x2 = q[:, nope_w:nope_w + LANES], q[:, nope_w + LANES:]
        cq, sq = cosq_ref[...], sinq_ref[...]
        roped = jnp.concatenate([x1 * cq - x2 * sq, x1 * sq + x2 * cq], axis=1)
        pe_heads = _mm(roped, spread_ref[...])
        for j in range(MLA_HEADS // 2):
            ql = _mm(q[:, 2 * QK_NOPE * j:2 * QK_NOPE * (j + 1)], wuk_ref[j])
            for hh in range(2):
                hd = 2 * j + hh
                qcat_ref[bi, hd, :, :KV_LORA] = ql[:, KV_LORA * hh:KV_LORA * (hh + 1)].astype(BF16)
                qcat_ref[bi, hd, :, KV_LORA:] = pe_heads[:, LANES * hd:LANES * (hd + 1)].astype(BF16)


def _in_proj(x, tabs, w, *, n_valid, pad_front):
    bx, t, d = x.shape
    tm = _row_tile(t, IN_PROJ_ROWS)
    assert tm % LANES == 0 or tm == t
    nt = t // tm
    bb = IN_PROJ_STREAMS if bx % IN_PROJ_STREAMS == 0 else 1
    row = lambda shape: pl.BlockSpec((bb, tm) + shape, lambda b, i: (b, i) + (0,) * len(shape))
    tab = pl.BlockSpec((tm, LANES), lambda b, i: (i, 0))
    zw = 3 * HG_KW + 2 * HG_WIDTH
    out_shape = (
        jax.ShapeDtypeStruct((bx, MLA_HEADS, t, QCAT), BF16),
        jax.ShapeDtypeStruct((bx, t, QCAT), BF16),
        jax.ShapeDtypeStruct((bx, VT_ROWS, t), BF16),
        jax.ShapeDtypeStruct((bx, t, KV_LORA), F32),
        jax.ShapeDtypeStruct((bx, t, QK_ROPE), F32),
        jax.ShapeDtypeStruct((bx, t, zw), F32),
    )
    out_specs = (
        pl.BlockSpec((bb, MLA_HEADS, tm, QCAT), lambda b, i: (b, 0, i, 0)),
        row((QCAT,)),
        pl.BlockSpec((bb, VT_ROWS, tm), lambda b, i: (b, 0, i)),
        row((KV_LORA,)), row((QK_ROPE,)), row((zw,)),
    )
    weights = (w['norm_mix_pre'], w['w_in'], w['q_norm'], w['kv_norm'], w['w_uq'], w['w_uk'], w['pe_spread'],
               w['hg_lb'])
    return pl.pallas_call(
        functools.partial(_in_proj_body, bb=bb, tm=tm, n_valid=n_valid, pad_front=pad_front),
        grid=(bx // bb, nt),
        in_specs=[row((d,))] + [tab] * len(tabs) + [_const_spec(a.shape) for a in weights],
        out_specs=out_specs,
        out_shape=out_shape,
        compiler_params=pltpu.CompilerParams(dimension_semantics=("parallel", "parallel"),
                                             vmem_limit_bytes=VMEM_LIMIT),
        name="in_proj",
    )(x, *tabs, *weights)


def _mla_prompt_body(*refs, qt, j0, aliased):
    q_ref, k_ref, vt_ref = refs[:3]
    o_ref, m_scr, acc_scr = refs[4:] if aliased else refs[3:]
    j = pl.program_id(1) + j0
    m_scr[...] = jnp.full(m_scr.shape, -jnp.inf, F32)
    acc_scr[...] = jnp.zeros(acc_scr.shape, F32)
    width = MLA_HEADS * qt
    lower_half = lax.broadcasted_iota(jnp.int32, (CHUNK, LANES), 1) < CHUNK

    def hidden(rows, cols=LANES):
        return jnp.full((rows, cols), -jnp.inf, F32)

    def mask_dummy(s):
        return jnp.concatenate([hidden(PAD_FRONT, s.shape[1]), s[PAD_FRONT:]], axis=0)

    def mask_diagonal(s):
        strips = []
        for c0 in range(0, width, LANES):
            t = s[:, c0:c0 + LANES]
            if (c0 % qt) // LANES == 0:
                t = jnp.concatenate([t[:CHUNK], jnp.where(lower_half, -jnp.inf, t[CHUNK:2 * CHUNK]),
                                     hidden(2 * CHUNK)], axis=0)
            else:
                t = jnp.concatenate([t[:3 * CHUNK], jnp.where(lower_half, -jnp.inf, t[3 * CHUNK:])], axis=0)
            strips.append(t)
        return jnp.concatenate(strips, axis=1)

    def keys_of(kb):
        start = kb * KEY_BLOCK
        return pl.ds(start if isinstance(kb, int) else pl.multiple_of(start, KEY_BLOCK), KEY_BLOCK)

    def scores(kb):
        return _mm_nt(k_ref[0, keys_of(kb), :], q_ref[0].reshape(width, QCAT))

    def absorb(s, kb, masks=()):
        for mask in masks:
            s = mask(s)
        m_old = m_scr[...]
        m_new = jnp.maximum(m_old, jnp.max(s, axis=0, keepdims=True))
        p = jnp.exp2(s - m_new)
        acc_scr[...] = jnp.exp2(m_old - m_new) * acc_scr[...] + _mm(vt_ref[0, :, keys_of(kb)], p)
        m_scr[...] = m_new

    def pair(a, b, masks_a=(), masks_b=()):
        s_a, s_b = scores(a), scores(b)
        absorb(s_a, a, masks_a)
        absorb(s_b, b, masks_b)

    @pl.when(j == 0)
    def _():
        absorb(scores(0), 0, (mask_diagonal, mask_dummy))

    @pl.when(j == 1)
    def _():
        pair(1, 0, (mask_diagonal,), (mask_dummy,))

    @pl.when(j > 1)
    def _():
        pair(j, j - 1, (mask_diagonal,))
        n_inner = (j - 2) // 2

        def inner(p, carry):
            pair(j - 2 * p, j - 2 * p - 1)
            return carry

        lax.fori_loop(1, n_inner + 1, inner, 0)

        @pl.when(j % 2 == 0)
        def _():
            absorb(scores(0), 0, (mask_dummy,))

        @pl.when(j % 2 == 1)
        def _():
            pair(1, 0, (), (mask_dummy,))

    for hd in range(MLA_HEADS):
        acc = acc_scr[:, qt * hd:qt * (hd + 1)]
        o_t = acc[:KV_LORA] * (1.0 / acc[KV_LORA:KV_LORA + 1])
        o_ref[0, :, KV_LORA * hd:KV_LORA * (hd + 1)] = o_t.T.astype(BF16)


def _mla_prompt_call(qcat, kcat, kvt, prev, *, qt, j0, steps):
    b, _, t, _ = qcat.shape
    per_block = KEY_BLOCK // qt
    in_specs = [pl.BlockSpec((1, MLA_HEADS, qt, QCAT), lambda i, j: (i, 0, (j + j0) * per_block, 0)),
                pl.BlockSpec((1, t, QCAT), lambda i, j: (i, 0, 0)),
                pl.BlockSpec((1, VT_ROWS, t), lambda i, j: (i, 0, 0))]
    args = (qcat, kcat, kvt)
    if prev is not None:
        in_specs.append(pl.BlockSpec(memory_space=pl.ANY))
        args += (prev,)
    return pl.pallas_call(
        functools.partial(_mla_prompt_body, qt=qt, j0=j0, aliased=prev is not None),
        grid=(b, steps),
        in_specs=in_specs,
        out_specs=pl.BlockSpec((1, qt, MLA_HEADS * KV_LORA), lambda i, j: (i, (j + j0) * per_block, 0)),
        out_shape=jax.ShapeDtypeStruct((b, t, MLA_HEADS * KV_LORA), BF16),
        input_output_aliases={3: 0} if prev is not None else {},
        scratch_shapes=[pltpu.VMEM((1, MLA_HEADS * qt), F32),
                        pltpu.VMEM((VT_ROWS, MLA_HEADS * qt), F32)],
        compiler_params=pltpu.CompilerParams(dimension_semantics=("parallel", "parallel"),
                                             vmem_limit_bytes=VMEM_LIMIT),
        name="mla_prompt",
    )(*args)


def _mla_prompt(qcat, kcat, kvt, *, lp):
    t = qcat.shape[2]
    assert t % KEY_BLOCK == 0 and lp <= t
    full, rest = divmod(lp, KEY_BLOCK)
    if rest > LANES:
        full, rest = full + 1, 0
    out = _mla_prompt_call(qcat, kcat, kvt, None, qt=KEY_BLOCK, j0=0, steps=full) if full else None
    if rest:
        out = _mla_prompt_call(qcat, kcat, kvt, out, qt=LANES, j0=full, steps=1)
    return out


def _mla_sample_body(q_ref, k_ref, ckv_ref, cpe_ref, o_ref, *, ts):
    rows = MLA_HEADS * ts
    q = q_ref[0].reshape(rows, QCAT)
    ckv = ckv_ref[0].astype(BF16)
    knew = k_ref[0]
    s_c = _mm_nt(q[:, :KV_LORA], ckv) + _mm_nt(q[:, KV_LORA:KV_LORA + QK_ROPE], cpe_ref[0])
    s_n = _mm_nt(q, knew)
    m = jnp.maximum(jnp.max(s_c, axis=-1, keepdims=True), jnp.max(s_n, axis=-1, keepdims=True))
    p_c = jnp.exp2(s_c - m)
    p_n = jnp.exp2(s_n - m)
    l = jnp.sum(p_c, axis=-1, keepdims=True) + jnp.sum(p_n, axis=-1, keepdims=True)
    o = (_mm(p_c, ckv) + _mm(p_n, knew[:, :KV_LORA])) / l
    for hd in range(MLA_HEADS):
        o_ref[0, :, KV_LORA * hd:KV_LORA * (hd + 1)] = o[ts * hd:ts * (hd + 1)].astype(BF16)


def _mla_sample(qcat, kcat, cache_kv, cache_pe):
    b, _, ts, _ = qcat.shape
    past = cache_kv.shape[1]
    return pl.pallas_call(
        functools.partial(_mla_sample_body, ts=ts),
        grid=(b,),
        in_specs=[pl.BlockSpec((1, MLA_HEADS, ts, QCAT), lambda i: (i, 0, 0, 0)),
                  pl.BlockSpec((1, ts, QCAT), lambda i: (i, 0, 0)),
                  pl.BlockSpec((1, past, KV_LORA), lambda i: (i, 0, 0)),
                  pl.BlockSpec((1, past, QK_ROPE), lambda i: (i, 0, 0))],
        out_specs=pl.BlockSpec((1, ts, MLA_HEADS * KV_LORA), lambda i: (i, 0, 0)),
        out_shape=jax.ShapeDtypeStruct((b, ts, MLA_HEADS * KV_LORA), BF16),
        compiler_params=pltpu.CompilerParams(dimension_semantics=("parallel",),
                                             vmem_limit_bytes=VMEM_LIMIT),
        name="mla_sample",
    )(qcat, kcat, cache_kv, cache_pe)


def _hgrn_body(zh_ref, s0_ref, gain_ref, ob_ref, sout_ref, st_ref, g_scr, q_scr, k_scr, v_scr, o_scr, oi_scr,
               *, bb, cs):
    c = pl.program_id(1)
    nc = pl.num_programs(1)
    streams = range(bb)
    heads = [slice(HG_DK * hd, HG_DK * (hd + 1)) for hd in range(HG_HEADS)]

    @pl.when(c == 0)
    def _():
        for bi in streams:
            for hd in range(HG_HEADS):
                st_ref[bi, hd] = s0_ref[bi, hd].T

    q = zh_ref[:, :, :HG_KW]
    gl = zh_ref[:, :, HG_KW:2 * HG_KW]
    k = zh_ref[:, :, 2 * HG_KW:3 * HG_KW]
    v = zh_ref[:, :, 3 * HG_KW:3 * HG_KW + HG_WIDTH]
    t_idx = lax.broadcasted_iota(jnp.int32, (cs, cs), 0)
    s_idx = lax.broadcasted_iota(jnp.int32, (cs, cs), 1)
    causal = t_idx >= s_idx
    tri = causal.astype(F32)
    g = jnp.stack([jnp.dot(tri, gl[bi], precision=lax.Precision.HIGHEST, preferred_element_type=F32)
                   for bi in streams])
    g_tot = g[:, cs - 1:cs, :]
    any_out_of_range = jnp.max(-g_tot) > HGRN_FAST_RANGE
    qt = q * jnp.exp(g)
    khat = k * jnp.exp(g_tot - g)
    kt = k * jnp.exp(-g)
    a = [[jnp.where(causal, _mm_nt(qt[bi][:, sl], kt[bi][:, sl]), 0.0) for sl in heads] for bi in streams]
    for bi in streams:
        o_scr[bi] = jnp.concatenate([_mm(a[bi][hd], v[bi][:, sl]) for hd, sl in enumerate(heads)], axis=-1)

    o_inter = [[_mm_nt(qt[bi][:, sl], st_ref[bi, hd]) for hd, sl in enumerate(heads)] for bi in streams]
    kv_new = [[_mm_tn(v[bi][:, sl], khat[bi][:, sl]) for sl in heads] for bi in streams]
    decay = jnp.exp(g_tot)
    for bi in streams:
        oi_scr[bi] = jnp.concatenate(o_inter[bi], axis=-1)
        for hd, sl in enumerate(heads):
            st_ref[bi, hd] = decay[bi][:, sl] * st_ref[bi, hd] + kv_new[bi][hd]

    @pl.when(any_out_of_range)
    def _():
        for bi in streams:
            @pl.when(jnp.max(-g_tot[bi]) > HGRN_FAST_RANGE)
            def _(bi=bi):
                g_scr[...] = g[bi]
                q_scr[...] = q[bi]
                k_scr[...] = k[bi]
                v_scr[...] = v[bi]
                o_scr[bi] = jnp.zeros((cs, HG_WIDTH), F32)
                t_col = lax.broadcasted_iota(jnp.int32, (cs, 1), 0)

                def cols(i, carry):
                    base = pl.multiple_of(i * 8, 8)
                    g8, k8, v8 = (ref[pl.ds(base, 8), :] for ref in (g_scr, k_scr, v_scr))
                    for j in range(8):
                        dec = jnp.exp(jnp.where(t_col >= base + j, g_scr[...] - g8[j:j + 1], -jnp.inf))
                        pr = q_scr[...] * dec * k8[j:j + 1]
                        for sl in heads:
                            o_scr[bi, :, sl] += jnp.sum(pr[:, sl], axis=-1, keepdims=True) * v8[j:j + 1, sl]
                    return carry

                lax.fori_loop(0, cs // 8, cols, 0)

    gain = gain_ref[...]
    for bi in streams:
        o = o_scr[bi] + oi_scr[bi]
        o = [_rms(o[:, sl], gain) for sl in heads]
        ob_ref[bi] = (jnp.concatenate(o, axis=-1) * _silu(zh_ref[bi, :, 3 * HG_KW + HG_WIDTH:])).astype(BF16)

    @pl.when(c == nc - 1)
    def _():
        for bi in streams:
            for hd in range(HG_HEADS):
                sout_ref[bi, hd] = st_ref[bi, hd].T


def _hgrn(zh, s0, gain, *, n_rows, cs):
    b, t, zw = zh.shape
    assert n_rows % cs == 0 and n_rows <= t
    bb = HGRN_STREAMS if b % HGRN_STREAMS == 0 else 1
    return pl.pallas_call(
        functools.partial(_hgrn_body, bb=bb, cs=cs),
        grid=(b // bb, n_rows // cs),
        in_specs=[pl.BlockSpec((bb, cs, zw), lambda i, c: (i, c, 0)),
                  pl.BlockSpec((bb, HG_HEADS, HG_DK, HG_DV), lambda i, c: (i, 0, 0, 0)),
                  _const_spec(gain.shape)],
        out_specs=(pl.BlockSpec((bb, cs, HG_WIDTH), lambda i, c: (i, c, 0)),
                   pl.BlockSpec((bb, HG_HEADS, HG_DK, HG_DV), lambda i, c: (i, 0, 0, 0))),
        out_shape=(jax.ShapeDtypeStruct((b, t, HG_WIDTH), BF16),
                   jax.ShapeDtypeStruct((b, HG_HEADS, HG_DK, HG_DV), F32)),
        scratch_shapes=[pltpu.VMEM((bb, HG_HEADS, HG_DV, HG_DK), F32)]
        + [pltpu.VMEM((cs, HG_KW), F32) for _ in range(4)] + [pltpu.VMEM((bb, cs, HG_KW), F32) for _ in range(2)],
        compiler_params=pltpu.CompilerParams(dimension_semantics=("parallel", "arbitrary"),
                                             vmem_limit_bytes=VMEM_LIMIT),
        name="hgrn",
    )(zh, s0, gain)


def _out_ffn_body(x_ref, ol_ref, ob_ref, buf_ref, wuv_ref, wout_ref, npost_ref, nfpre_ref, wup_ref, cw_ref,
                  cb_ref, wdn_ref, nfpost_ref, fin_ref, xo_ref, bufo_ref, a_scr, *, bb, tm, pad_front, last):
    t = pl.program_id(1)
    nt = pl.num_programs(1)
    d_ff = cw_ref.shape[1]
    keep = CONV_W - 1
    streams = range(bb)

    @pl.when(t == 0)
    def _():
        for bi in streams:
            a_scr[bi, 0:8, :] = jnp.zeros((8, d_ff), F32)
            a_scr[bi, 8 - keep:8, :] = buf_ref[bi]

    mixed = []
    for bi in streams:
        ol = ol_ref[bi]
        o_a = [_mm(ol[:, 2 * KV_LORA * j:2 * KV_LORA * (j + 1)], wuv_ref[j]) for j in range(MLA_HEADS // 2)]
        mixed.append(jnp.concatenate([o.astype(BF16) for o in o_a] + [ob_ref[bi]], axis=-1))
    y = [_mm(mixed[bi], wout_ref[...]) for bi in streams]
    x1 = [x_ref[bi] + _rms(y[bi], npost_ref[...]) for bi in streams]
    u = [_mm(_rms(x1[bi], nfpre_ref[...]), wup_ref[...]) for bi in streams]
    cw = cw_ref[...]
    if pad_front:
        live = (t * tm + lax.broadcasted_iota(jnp.int32, (tm, 1), 0)) >= pad_front
    f = []
    for bi in streams:
        a = u[bi][:, :d_ff]
        if pad_front:
            a = jnp.where(live, a, 0.0)
        a_scr[bi, 8:8 + tm, :] = a
        conv = cb_ref[...] + a * cw[keep:keep + 1]
        for j in range(keep):
            conv = conv + a_scr[bi, 8 - keep + j:8 - keep + j + tm, :] * cw[j:j + 1]
        f.append(_mm(_silu(conv) * u[bi][:, d_ff:], wdn_ref[...]))
    for bi in streams:
        x2 = x1[bi] + _rms(f[bi], nfpost_ref[...])
        xo_ref[bi] = _rms(x2, fin_ref[...]) if last else x2
        a_scr[bi, 0:8, :] = a_scr[bi, tm:tm + 8, :]

    @pl.when(t == nt - 1)
    def _():
        for bi in streams:
            bufo_ref[bi] = a_scr[bi, 8 - keep:8, :]


def _out_ffn(x, ol, ob, buf, w, final_norm, *, n_rows, pad_front, last):
    b, t, d = x.shape
    d_ff = buf.shape[-1]
    tm = _row_tile(n_rows, 192)
    assert tm >= 8 and n_rows <= t
    bb = FFN_STREAMS if b % FFN_STREAMS == 0 else 1
    row = lambda width: pl.BlockSpec((bb, tm, width), lambda i, j: (i, j, 0))
    per_b = pl.BlockSpec((bb, CONV_W - 1, d_ff), lambda i, j: (i, 0, 0))
    weights = (w['w_uv'], w['w_out'], w['norm_mix_post'], w['norm_ffn_pre'], w['w_ffn_up'], w['ffn_conv_w'],
               w['ffn_conv_b'], w['w_ffn_down'], w['norm_ffn_post'], final_norm)
    return pl.pallas_call(
        functools.partial(_out_ffn_body, bb=bb, tm=tm, pad_front=pad_front, last=last),
        grid=(b // bb, n_rows // tm),
        in_specs=[row(d), row(MLA_HEADS * KV_LORA), row(HG_WIDTH), per_b] + [_const_spec(a.shape) for a in weights],
        out_specs=(row(d), per_b),
        out_shape=(jax.ShapeDtypeStruct((b, t, d), F32), jax.ShapeDtypeStruct((b, CONV_W - 1, d_ff), F32)),
        scratch_shapes=[pltpu.VMEM((bb, tm + 8, d_ff), F32)],
        compiler_params=pltpu.CompilerParams(dimension_semantics=("parallel", "arbitrary"),
                                             vmem_limit_bytes=VMEM_LIMIT),
        name="out_ffn",
    )(x, ol, ob, buf, *weights)


def _rope_tables(pos):
    half = QK_ROPE // 2
    inv = ROPE_THETA ** (-jnp.arange(half, dtype=F32) / half)
    ang = pos.astype(F32)[:, None] * inv
    cos, sin = jnp.cos(ang), jnp.sin(ang)
    zero = jnp.zeros_like(cos)
    pad = jnp.zeros((pos.shape[0], LANES - QK_ROPE), F32)
    return (jnp.concatenate([cos, cos, pad], axis=1),
            jnp.concatenate([-sin, zero, pad], axis=1),
            jnp.concatenate([zero, sin, pad], axis=1),
            jnp.tile(cos, (1, MLA_HEADS)), jnp.tile(sin, (1, MLA_HEADS)))


def _layer_weights(l, w_in, norm_mix_pre, norm_mix_post, q_norm, kv_norm, w_uq, w_uk, w_uv, hg_out_norm, w_out,
                   norm_ffn_pre, norm_ffn_post, w_ffn_up, ffn_conv_w, ffn_conv_b, w_ffn_down):
    d = w_in.shape[1]
    head_w = Q_LORA + KV_LORA + QK_ROPE
    w_in_r = jnp.concatenate([w_in[l][:, :head_w], jnp.zeros((d, LANES - QK_ROPE), F32), w_in[l][:, head_w:]], axis=1)
    uq = w_uq[l].reshape(Q_LORA, MLA_HEADS, QK_NOPE + QK_ROPE)
    half = QK_ROPE // 2
    w_uq_r = jnp.concatenate([uq[:, :, :QK_NOPE].reshape(Q_LORA, -1),
                              uq[:, :, QK_NOPE:QK_NOPE + half].reshape(Q_LORA, -1),
                              uq[:, :, QK_NOPE + half:].reshape(Q_LORA, -1)], axis=1)
    src = jnp.arange(2 * MLA_HEADS * half)
    dst = (src % (MLA_HEADS * half)) // half * LANES + src // (MLA_HEADS * half) * half + src % half
    pe_spread = jnp.zeros((2 * MLA_HEADS * half, MLA_HEADS * LANES), F32).at[src, dst].set(1.0)
    uk = jnp.transpose(w_uk[l], (1, 2, 0))
    uk_pair = jnp.zeros((MLA_HEADS // 2, 2 * QK_NOPE, 2 * KV_LORA), F32)
    uk_pair = uk_pair.at[:, :QK_NOPE, :KV_LORA].set(uk[0::2]).at[:, QK_NOPE:, KV_LORA:].set(uk[1::2])
    uv = jnp.transpose(w_uv[l], (1, 0, 2))
    uv_pair = jnp.zeros((MLA_HEADS // 2, 2 * KV_LORA, 2 * V_HEAD), F32)
    uv_pair = uv_pair.at[:, :KV_LORA, :V_HEAD].set(uv[0::2]).at[:, KV_LORA:, V_HEAD:].set(uv[1::2])
    row = lambda a: a.reshape(1, -1)
    return {
        'w_in': w_in_r.astype(BF16), 'norm_mix_pre': row(norm_mix_pre[l]), 'norm_mix_post': row(norm_mix_post[l]),
        'q_norm': row(q_norm[l]), 'kv_norm': row(kv_norm[l]), 'w_uq': w_uq_r.astype(BF16),
        'w_uk': uk_pair.astype(BF16), 'w_uv': uv_pair.astype(BF16), 'pe_spread': pe_spread.astype(BF16),
        'hg_gain': row(hg_out_norm[l]), 'w_out': w_out[l].astype(BF16),
        'norm_ffn_pre': row(norm_ffn_pre[l]), 'norm_ffn_post': row(norm_ffn_post[l]),
        'w_ffn_up': w_ffn_up[l].astype(BF16), 'ffn_conv_w': ffn_conv_w[l], 'ffn_conv_b': row(ffn_conv_b[l]),
        'w_ffn_down': w_ffn_down[l].astype(BF16),
    }


def kernel(x_prompt, x_sample, cache_kv_latent, cache_k_rope, state_hgrn, state_ffn_conv, meta_tokens, w_in, norm_mix_pre, norm_mix_post, q_norm, kv_norm, w_uq, w_uk, w_uv, hg_lower_bounds, hg_out_norm, w_out, norm_ffn_pre, norm_ffn_post, w_ffn_up, ffn_conv_w, ffn_conv_b, w_ffn_down, final_norm):
    depth = w_in.shape[0]
    b, seq, d = x_prompt.shape
    bs, ts, _ = x_sample.shape
    past = cache_kv_latent.shape[2]
    d_ff = ffn_conv_w.shape[-1]
    lp = PAD_FRONT + N_META + seq

    lb_sm = jax.nn.softmax(hg_lower_bounds.astype(F32), axis=0)
    lb_all = jnp.cumsum(lb_sm, axis=0) - lb_sm[0]

    unit = IN_PROJ_ROWS * KEY_BLOCK // 128
    lp_alloc = -(-lp // unit) * unit
    meta = jnp.broadcast_to(meta_tokens[None], (b, N_META, d))
    xp = jnp.concatenate([jnp.zeros((b, PAD_FRONT, d), F32), meta, x_prompt,
                          jnp.zeros((b, lp_alloc - lp, d), F32)], axis=1)
    xs = x_sample
    tabs_p = _rope_tables(jnp.arange(lp_alloc, dtype=jnp.int32) - CHUNK)
    tabs_s = tuple(jnp.tile(a, (bs, 1)) for a in _rope_tables(past + jnp.arange(ts, dtype=jnp.int32)))
    fin = final_norm.reshape(1, -1)
    s0_p = jnp.zeros((b, HG_HEADS, HG_DK, HG_DV), F32)
    buf0_p = jnp.zeros((b, CONV_W - 1, d_ff), F32)

    outs = [[] for _ in range(8)]
    for l in range(depth):
        w = _layer_weights(l, w_in, norm_mix_pre, norm_mix_post, q_norm, kv_norm, w_uq, w_uk, w_uv, hg_out_norm,
                           w_out, norm_ffn_pre, norm_ffn_post, w_ffn_up, ffn_conv_w, ffn_conv_b, w_ffn_down)
        w['hg_lb'] = lb_all[l].reshape(1, -1)
        gain = w['hg_gain']
        last = l == depth - 1

        qcat, kcat, kvt, kv, pe, zh = _in_proj(xp, tabs_p, w, n_valid=lp, pad_front=PAD_FRONT)
        ol = _mla_prompt(qcat, kcat, kvt, lp=lp)
        ob, s_new = _hgrn(zh, s0_p, gain, n_rows=lp, cs=CHUNK)
        xp, buf = _out_ffn(xp, ol, ob, buf0_p, w, fin, n_rows=lp, pad_front=PAD_FRONT, last=last)
        for lst, a in zip(outs[:4], (kv[:, PAD_FRONT:lp], pe[:, PAD_FRONT:lp], s_new, buf)):
            lst.append(a)

        qcat, kcat, _, kv, pe, zh = _in_proj(xs.reshape(1, bs * ts, d), tabs_s, w, n_valid=bs * ts, pad_front=0)
        qcat = jnp.swapaxes(qcat.reshape(MLA_HEADS, bs, ts, QCAT), 0, 1)
        kcat, kv, pe, zh = (a.reshape(bs, ts, -1) for a in (kcat, kv, pe, zh))
        ol = _mla_sample(qcat, kcat, cache_kv_latent[l], cache_k_rope[l])
        ob, s_new = _hgrn(zh, state_hgrn[l], gain, n_rows=ts, cs=ts)
        xs, buf = _out_ffn(xs, ol, ob, state_ffn_conv[l], w, fin, n_rows=ts, pad_front=0, last=last)
        for lst, a in zip(outs[4:], (kv, pe, s_new, buf)):
            lst.append(a)

    return (xp[:, PAD_FRONT + N_META:lp], xs) + tuple(jnp.stack(o) for o in outs)
```

```python
import functools

import jax
import jax.numpy as jnp
from jax import lax
from jax.experimental import pallas as pl
from jax.experimental.pallas import tpu as pltpu

F32 = jnp.float32
BF16 = jnp.bfloat16

CHUNK = 64
N_META = 16
PAD_FRONT = CHUNK - N_META
EPS = 1e-6
MLA_HEADS = 8
QK_NOPE = 64
QK_ROPE = 32
V_HEAD = 64
Q_LORA = 256
KV_LORA = 128
ROPE_THETA = 10000.0
HG_HEADS = 4
HG_DK = 128
HG_DV = 128
CONV_W = 3

HG_KW = HG_HEADS * HG_DK
HG_WIDTH = HG_HEADS * HG_DV
MLA_WIDTH = MLA_HEADS * V_HEAD
LANES = 128
KEY_BLOCK = 256
QCAT = KV_LORA + LANES
VT_ROWS = KV_LORA + 16
PREFIX_KEYS = N_META
IN_PROJ_ROWS = 384
IN_PROJ_STREAMS = 2
Q_SCALE = (QK_NOPE + QK_ROPE) ** -0.5 * 1.4426950408889634
HGRN_FAST_RANGE = 80.0
FFN_ROWS = 256
FFN_STREAMS = 2
HGRN_STREAMS = 4
VMEM_LIMIT = 56 * 1024 * 1024


def _row_tile(n, target):
    if n <= target:
        return n
    best = None
    for t in range(16, target + 1, 16):
        if n % t == 0:
            best = t
    assert best is not None, (n, target)
    return best


def _const_spec(shape):
    nd = len(shape)
    return pl.BlockSpec(shape, lambda *_: (0,) * nd, pipeline_mode=pl.Buffered(1))


def _rms(x, w):
    return x * lax.rsqrt(jnp.mean(x * x, axis=-1, keepdims=True) + EPS) * w


def _silu(x):
    return x * jax.nn.sigmoid(x)


def _mm(a, b):
    return jnp.dot(a.astype(BF16), b.astype(BF16), preferred_element_type=F32)


def _mm_nt(a, b):
    return lax.dot_general(a.astype(BF16), b.astype(BF16), (((1,), (1,)), ((), ())),
                           preferred_element_type=F32)


def _mm_tn(a, b):
    return lax.dot_general(a.astype(BF16), b.astype(BF16), (((0,), (0,)), ((), ())),
                           preferred_element_type=F32)


def _in_proj_body(x_ref, cos_ref, sn_ref, sp_ref, cosq_ref, sinq_ref, npre_ref, win_ref, qn_ref, kvn_ref, wuq_ref,
                  wuk_ref, spread_ref, lb_ref, qcat_ref, kcat_ref, kvt_ref, kv_ref, pe_ref, zh_ref,
                  *, bb, tm, n_valid, pad_front):
    cos, sn, sp = cos_ref[...], sn_ref[...], sp_ref[...]
    lb = lb_ref[...]

    def rope(v):
        return v * cos + pltpu.roll(v, LANES - QK_ROPE // 2, 1) * sn + pltpu.roll(v, QK_ROPE // 2, 1) * sp

    row = pl.program_id(1) * tm + lax.broadcasted_iota(jnp.int32, (tm, 1), 0)
    live = row < n_valid
    real = row >= pad_front
    h = [_rms(jnp.where(live, x_ref[bi], 0.0), npre_ref[...]) for bi in range(bb)]
    zs = [_mm(h[bi], win_ref[...]) for bi in range(bb)]
    nope_w = MLA_HEADS * QK_NOPE
    for bi, z in enumerate(zs):
        off = Q_LORA + KV_LORA + LANES
        hf = z[:, off + HG_KW:off + 2 * HG_KW]
        e = jnp.exp(-jnp.abs(hf))
        r = 1.0 / (1.0 + e)
        f = lb + (1.0 - lb) * jnp.where(hf >= 0.0, r, e * r)
        zh_ref[bi, :, :HG_KW] = jnp.where(real, _silu(z[:, off:off + HG_KW]), 0.0)
        zh_ref[bi, :, HG_KW:2 * HG_KW] = jnp.where(real, jnp.log(f), 0.0)
        zh_ref[bi, :, 2 * HG_KW:3 * HG_KW] = jnp.where(real, (1.0 - lb) * jnp.where(hf >= 0.0, e * r, r), 0.0)
        zh_ref[bi, :, 3 * HG_KW:] = z[:, off + 2 * HG_KW:]
        kv_lat = _rms(z[:, Q_LORA:Q_LORA + KV_LORA], kvn_ref[...])
        k_pe = rope(z[:, Q_LORA + KV_LORA:Q_LORA + KV_LORA + LANES])
        kv_ref[bi] = kv_lat
        pe_ref[bi] = k_pe[:, :QK_ROPE]
        kcat_ref[bi, :, :KV_LORA] = kv_lat.astype(BF16)
        kcat_ref[bi, :, KV_LORA:] = k_pe.astype(BF16)
        kvt_ref[bi, :KV_LORA, :] = kv_lat.T.astype(BF16)
        kvt_ref[bi, KV_LORA:, :] = jnp.ones((VT_ROWS - KV_LORA, tm), BF16)

        cqn = _rms(z[:, :Q_LORA], qn_ref[...])
        q = _mm(cqn, wuq_ref[...]) * Q_SCALE
        x1, x2 = q[:, nope_w:nope_w + LANES], q[:, nope_w + LANES:]
        cq, sq = cosq_ref[...], sinq_ref[...]
        roped = jnp.concatenate([x1 * cq - x2 * sq, x1 * sq + x2 * cq], axis=1)
        pe_heads = _mm(roped, spread_ref[...])
        for j in range(MLA_HEADS // 2):
            ql = _mm(q[:, 2 * QK_NOPE * j:2 * QK_NOPE * (j + 1)], wuk_ref[j])
            for hh in range(2):
                hd = 2 * j + hh
                qcat_ref[bi, hd, :, :KV_LORA] = ql[:, KV_LORA * hh:KV_LORA * (hh + 1)].astype(BF16)
                qcat_ref[bi, hd, :, KV_LORA:] = pe_heads[:, LANES * hd:LANES * (hd + 1)].astype(BF16)


def _in_proj(x, tabs, w, *, n_valid, pad_front):
    bx, t, d = x.shape
    tm = _row_tile(t, IN_PROJ_ROWS)
    assert tm % LANES == 0 or tm == t
    nt = t // tm
    bb = IN_PROJ_STREAMS if bx % IN_PROJ_STREAMS == 0 else 1
    row = lambda shape: pl.BlockSpec((bb, tm) + shape, lambda b, i: (b, i) + (0,) * len(shape))
    tab = pl.BlockSpec((tm, LANES), lambda b, i: (i, 0))
    zw = 3 * HG_KW + 2 * HG_WIDTH
    out_shape = (
        jax.ShapeDtypeStruct((bx, MLA_HEADS, t, QCAT), BF16),
        jax.ShapeDtypeStruct((bx, t, QCAT), BF16),
        jax.ShapeDtypeStruct((bx, VT_ROWS, t), BF16),
        jax.ShapeDtypeStruct((bx, t, KV_LORA), F32),
        jax.ShapeDtypeStruct((bx, t, QK_ROPE), F32),
        jax.ShapeDtypeStruct((bx, t, zw), F32),
    )
    out_specs = (
        pl.BlockSpec((bb, MLA_HEADS, tm, QCAT), lambda b, i: (b, 0, i, 0)),
        row((QCAT,)),
        pl.BlockSpec((bb, VT_ROWS, tm), lambda b, i: (b, 0, i)),
        row((KV_LORA,)), row((QK_ROPE,)), row((zw,)),
    )
    weights = (w['norm_mix_pre'], w['w_in'], w['q_norm'], w['kv_norm'], w['w_uq'], w['w_uk'], w['pe_spread'],
               w['hg_lb'])
    return pl.pallas_call(
        functools.partial(_in_proj_body, bb=bb, tm=tm, n_valid=n_valid, pad_front=pad_front),
        grid=(bx // bb, nt),
        in_specs=[row((d,))] + [tab] * len(tabs) + [_const_spec(a.shape) for a in weights],
        out_specs=out_specs,
        out_shape=out_shape,
        compiler_params=pltpu.CompilerParams(dimension_semantics=("parallel", "parallel"),
                                             vmem_limit_bytes=VMEM_LIMIT),
        name="in_proj",
    )(x, *tabs, *weights)


def _mla_prompt_body(*refs, qt, j0, aliased, dummy, prefix):
    q_ref, k_ref, vt_ref = refs[:3]
    kpre_ref, vtpre_ref = refs[3:5] if prefix else (None, None)
    o_ref, m_scr, acc_scr = refs[-3:]
    j = pl.program_id(1) + j0
    m_scr[...] = jnp.full(m_scr.shape, -jnp.inf, F32)
    acc_scr[...] = jnp.zeros(acc_scr.shape, F32)
    width = MLA_HEADS * qt
    lower_half = lax.broadcasted_iota(jnp.int32, (CHUNK, LANES), 1) < CHUNK

    def hidden(rows, cols=LANES):
        return jnp.full((rows, cols), -jnp.inf, F32)

    def mask_dummy(s):
        return jnp.concatenate([hidden(dummy, s.shape[1]), s[dummy:]], axis=0) if dummy else s

    def mask_diagonal(s):
        strips = []
        for c0 in range(0, width, LANES):
            t = s[:, c0:c0 + LANES]
            if (c0 % qt) // LANES == 0:
                rows = [t[:CHUNK], jnp.where(lower_half, -jnp.inf, t[CHUNK:2 * CHUNK]), hidden(2 * CHUNK)]
            else:
                rows = [t[:3 * CHUNK], jnp.where(lower_half, -jnp.inf, t[3 * CHUNK:KEY_BLOCK])]
            if prefix:
                rows += [t[KEY_BLOCK:KEY_BLOCK + PREFIX_KEYS], hidden(LANES - PREFIX_KEYS)]
            strips.append(jnp.concatenate(rows, axis=0))
        return jnp.concatenate(strips, axis=1)

    def keys_of(kb):
        start = kb * KEY_BLOCK
        return pl.ds(start if isinstance(kb, int) else pl.multiple_of(start, KEY_BLOCK), KEY_BLOCK)

    def scores(kb, diagonal=False):
        keys = k_ref[0, keys_of(kb), :]
        if diagonal and prefix:
            keys = jnp.concatenate([keys, kpre_ref[...]], axis=0)
        return _mm_nt(keys, q_ref[0].reshape(width, QCAT))

    def absorb(s, kb, diagonal=False):
        if diagonal:
            s = mask_diagonal(s)
        if isinstance(kb, int) and kb == 0:
            s = mask_dummy(s)
        vt = vt_ref[0, :, keys_of(kb)]
        for hd in range(MLA_HEADS):
            cols = slice(qt * hd, qt * (hd + 1))
            m_old = m_scr[:, cols]
            m_new = jnp.maximum(m_old, jnp.max(s[:, cols], axis=0, keepdims=True))
            p = jnp.exp2(s[:, cols] - m_new)
            pv = _mm(vt, p[:KEY_BLOCK])
            if diagonal and prefix:
                pv = pv + _mm(vtpre_ref[...], p[KEY_BLOCK:])
            acc_scr[:, cols] = jnp.exp2(m_old - m_new) * acc_scr[:, cols] + pv
            m_scr[:, cols] = m_new

    def pair(a, b, diagonal=False):
        s_a, s_b = scores(a, diagonal), scores(b)
        absorb(s_a, a, diagonal)
        absorb(s_b, b)

    def finish():
        for hd in range(MLA_HEADS):
            acc = acc_scr[:, qt * hd:qt * (hd + 1)]
            o_t = acc[:KV_LORA] * (1.0 / acc[KV_LORA:KV_LORA + 1])
            o_ref[0, :, KV_LORA * hd:KV_LORA * (hd + 1)] = o_t.T.astype(BF16)

    @pl.when(j == 0)
    def _():
        absorb(scores(0, True), 0, True)

    n_blocks = k_ref.shape[1] // KEY_BLOCK
    if n_blocks < 2:
        return finish()

    @pl.when(j == 1)
    def _():
        pair(1, 0, True)

    if n_blocks < 3:
        return finish()

    @pl.when(j > 1)
    def _():
        pair(j, j - 1, True)
        n_inner = (j - 2) // 2

        def inner(p, carry):
            pair(j - 2 * p, j - 2 * p - 1)
            return carry

        lax.fori_loop(1, n_inner + 1, inner, 0)

        @pl.when(j % 2 == 0)
        def _():
            absorb(scores(0), 0)

        @pl.when(j % 2 == 1)
        def _():
            pair(1, 0)

    finish()


def _mla_prompt_call(qcat, kcat, kvt, prefix, prev, *, qt, j0, steps, dummy):
    b, _, t, _ = qcat.shape
    per_block = KEY_BLOCK // qt
    in_specs = [pl.BlockSpec((1, MLA_HEADS, qt, QCAT), lambda i, j: (i, 0, (j + j0) * per_block, 0)),
                pl.BlockSpec((1, t, QCAT), lambda i, j: (i, 0, 0)),
                pl.BlockSpec((1, VT_ROWS, t), lambda i, j: (i, 0, 0))]
    args = (qcat, kcat, kvt)
    if prefix is not None:
        in_specs += [_const_spec(a.shape) for a in prefix]
        args += tuple(prefix)
    if prev is not None:
        in_specs.append(pl.BlockSpec(memory_space=pl.ANY))
        args += (prev,)
    return pl.pallas_call(
        functools.partial(_mla_prompt_body, qt=qt, j0=j0, aliased=prev is not None, dummy=dummy,
                          prefix=prefix is not None),
        grid=(b, steps),
        in_specs=in_specs,
        out_specs=pl.BlockSpec((1, qt, MLA_HEADS * KV_LORA), lambda i, j: (i, (j + j0) * per_block, 0)),
        out_shape=jax.ShapeDtypeStruct((b, t, MLA_HEADS * KV_LORA), BF16),
        input_output_aliases={len(args) - 1: 0} if prev is not None else {},
        scratch_shapes=[pltpu.VMEM((1, MLA_HEADS * qt), F32),
                        pltpu.VMEM((VT_ROWS, MLA_HEADS * qt), F32)],
        compiler_params=pltpu.CompilerParams(dimension_semantics=("parallel", "parallel"),
                                             vmem_limit_bytes=VMEM_LIMIT),
        name="mla_prompt",
    )(*args)


def _mla_prompt(qcat, kcat, kvt, prefix, *, lp, dummy):
    t = qcat.shape[2]
    assert t % KEY_BLOCK == 0 and lp <= t
    full, rest = divmod(lp, KEY_BLOCK)
    if rest > LANES:
        full, rest = full + 1, 0
    out = None
    if full:
        out = _mla_prompt_call(qcat, kcat, kvt, prefix, None, qt=KEY_BLOCK, j0=0, steps=full, dummy=dummy)
    if rest:
        out = _mla_prompt_call(qcat, kcat, kvt, prefix, out, qt=LANES, j0=full, steps=1, dummy=dummy)
    return out


def _mla_sample_body(q_ref, k_ref, ckv_ref, cpe_ref, o_ref, *, ts):
    rows = MLA_HEADS * ts
    q = q_ref[0].reshape(rows, QCAT)
    ckv = ckv_ref[0].astype(BF16)
    knew = k_ref[0]
    s_c = _mm_nt(q[:, :KV_LORA], ckv) + _mm_nt(q[:, KV_LORA:KV_LORA + QK_ROPE], cpe_ref[0])
    s_n = _mm_nt(q, knew)
    m = jnp.maximum(jnp.max(s_c, axis=-1, keepdims=True), jnp.max(s_n, axis=-1, keepdims=True))
    p_c = jnp.exp2(s_c - m)
    p_n = jnp.exp2(s_n - m)
    l = jnp.sum(p_c, axis=-1, keepdims=True) + jnp.sum(p_n, axis=-1, keepdims=True)
    o = (_mm(p_c, ckv) + _mm(p_n, knew[:, :KV_LORA])) / l
    for hd in range(MLA_HEADS):
        o_ref[0, :, KV_LORA * hd:KV_LORA * (hd + 1)] = o[ts * hd:ts * (hd + 1)].astype(BF16)


def _mla_sample(qcat, kcat, cache_kv, cache_pe):
    b, _, ts, _ = qcat.shape
    past = cache_kv.shape[1]
    return pl.pallas_call(
        functools.partial(_mla_sample_body, ts=ts),
        grid=(b,),
        in_specs=[pl.BlockSpec((1, MLA_HEADS, ts, QCAT), lambda i: (i, 0, 0, 0)),
                  pl.BlockSpec((1, ts, QCAT), lambda i: (i, 0, 0)),
                  pl.BlockSpec((1, past, KV_LORA), lambda i: (i, 0, 0)),
                  pl.BlockSpec((1, past, QK_ROPE), lambda i: (i, 0, 0))],
        out_specs=pl.BlockSpec((1, ts, MLA_HEADS * KV_LORA), lambda i: (i, 0, 0)),
        out_shape=jax.ShapeDtypeStruct((b, ts, MLA_HEADS * KV_LORA), BF16),
        compiler_params=pltpu.CompilerParams(dimension_semantics=("parallel",),
                                             vmem_limit_bytes=VMEM_LIMIT),
        name="mla_sample",
    )(qcat, kcat, cache_kv, cache_pe)


def _hgrn_body(zh_ref, s0_ref, gain_ref, ob_ref, sout_ref, st_ref, g_scr, q_scr, k_scr, v_scr, o_scr, oi_scr,
               *, bb, cs):
    c = pl.program_id(1)
    nc = pl.num_programs(1)
    streams = range(bb)
    heads = [slice(HG_DK * hd, HG_DK * (hd + 1)) for hd in range(HG_HEADS)]

    @pl.when(c == 0)
    def _():
        for bi in streams:
            for hd in range(HG_HEADS):
                st_ref[bi, hd] = s0_ref[bi, hd].T

    q = zh_ref[:, :, :HG_KW]
    gl = zh_ref[:, :, HG_KW:2 * HG_KW]
    k = zh_ref[:, :, 2 * HG_KW:3 * HG_KW]
    v = zh_ref[:, :, 3 * HG_KW:3 * HG_KW + HG_WIDTH]
    t_idx = lax.broadcasted_iota(jnp.int32, (cs, cs), 0)
    s_idx = lax.broadcasted_iota(jnp.int32, (cs, cs), 1)
    causal = t_idx >= s_idx
    tri = causal.astype(F32)
    g = jnp.stack([jnp.dot(tri, gl[bi], precision=lax.Precision.HIGHEST, preferred_element_type=F32)
                   for bi in streams])
    g_tot = g[:, cs - 1:cs, :]
    any_out_of_range = jnp.max(-g_tot) > HGRN_FAST_RANGE
    qt = q * jnp.exp(g)
    khat = k * jnp.exp(g_tot - g)
    kt = k * jnp.exp(-g)
    a = [[jnp.where(causal, _mm_nt(qt[bi][:, sl], kt[bi][:, sl]), 0.0) for sl in heads] for bi in streams]
    for bi in streams:
        o_scr[bi] = jnp.concatenate([_mm(a[bi][hd], v[bi][:, sl]) for hd, sl in enumerate(heads)], axis=-1)

    o_inter = [[_mm_nt(qt[bi][:, sl], st_ref[bi, hd]) for hd, sl in enumerate(heads)] for bi in streams]
    kv_new = [[_mm_tn(v[bi][:, sl], khat[bi][:, sl]) for sl in heads] for bi in streams]
    decay = jnp.exp(g_tot)
    for bi in streams:
        oi_scr[bi] = jnp.concatenate(o_inter[bi], axis=-1)
        for hd, sl in enumerate(heads):
            st_ref[bi, hd] = decay[bi][:, sl] * st_ref[bi, hd] + kv_new[bi][hd]

    @pl.when(any_out_of_range)
    def _():
        for bi in streams:
            @pl.when(jnp.max(-g_tot[bi]) > HGRN_FAST_RANGE)
            def _(bi=bi):
                g_scr[...] = g[bi]
                q_scr[...] = q[bi]
                k_scr[...] = k[bi]
                v_scr[...] = v[bi]
                o_scr[bi] = jnp.zeros((cs, HG_WIDTH), F32)
                t_col = lax.broadcasted_iota(jnp.int32, (cs, 1), 0)

                def cols(i, carry):
                    base = pl.multiple_of(i * 8, 8)
                    g8, k8, v8 = (ref[pl.ds(base, 8), :] for ref in (g_scr, k_scr, v_scr))
                    for j in range(8):
                        dec = jnp.exp(jnp.where(t_col >= base + j, g_scr[...] - g8[j:j + 1], -jnp.inf))
                        pr = q_scr[...] * dec * k8[j:j + 1]
                        for sl in heads:
                            o_scr[bi, :, sl] += jnp.sum(pr[:, sl], axis=-1, keepdims=True) * v8[j:j + 1, sl]
                    return carry

                lax.fori_loop(0, cs // 8, cols, 0)

    gain = gain_ref[...]
    for bi in streams:
        o = o_scr[bi] + oi_scr[bi]
        o = [_rms(o[:, sl], gain) for sl in heads]
        ob_ref[bi] = (jnp.concatenate(o, axis=-1) * _silu(zh_ref[bi, :, 3 * HG_KW + HG_WIDTH:])).astype(BF16)

    @pl.when(c == nc - 1)
    def _():
        for bi in streams:
            for hd in range(HG_HEADS):
                sout_ref[bi, hd] = st_ref[bi, hd].T


def _hgrn(zh, s0, gain, *, n_rows, cs):
    b, t, zw = zh.shape
    assert n_rows % cs == 0 and n_rows <= t
    bb = HGRN_STREAMS if b % HGRN_STREAMS == 0 else 1
    return pl.pallas_call(
        functools.partial(_hgrn_body, bb=bb, cs=cs),
        grid=(b // bb, n_rows // cs),
        in_specs=[pl.BlockSpec((bb, cs, zw), lambda i, c: (i, c, 0)),
                  pl.BlockSpec((bb, HG_HEADS, HG_DK, HG_DV), lambda i, c: (i, 0, 0, 0)),
                  _const_spec(gain.shape)],
        out_specs=(pl.BlockSpec((bb, cs, HG_WIDTH), lambda i, c: (i, c, 0)),
                   pl.BlockSpec((bb, HG_HEADS, HG_DK, HG_DV), lambda i, c: (i, 0, 0, 0))),
        out_shape=(jax.ShapeDtypeStruct((b, t, HG_WIDTH), BF16),
                   jax.ShapeDtypeStruct((b, HG_HEADS, HG_DK, HG_DV), F32)),
        scratch_shapes=[pltpu.VMEM((bb, HG_HEADS, HG_DV, HG_DK), F32)]
        + [pltpu.VMEM((cs, HG_KW), F32) for _ in range(4)] + [pltpu.VMEM((bb, cs, HG_KW), F32) for _ in range(2)],
        compiler_params=pltpu.CompilerParams(dimension_semantics=("parallel", "arbitrary"),
                                             vmem_limit_bytes=VMEM_LIMIT),
        name="hgrn",
    )(zh, s0, gain)


def _out_ffn_body(x_ref, ol_ref, ob_ref, buf_ref, wuv_ref, wout_ref, npost_ref, nfpre_ref, wup_ref, cw_ref,
                  cb_ref, wdn_ref, nfpost_ref, fin_ref, xo_ref, bufo_ref, a_scr, *, bb, tm, pad_front, last):
    t = pl.program_id(1)
    nt = pl.num_programs(1)
    d_ff = cw_ref.shape[1]
    keep = CONV_W - 1
    streams = range(bb)

    @pl.when(t == 0)
    def _():
        for bi in streams:
            a_scr[bi, 0:8, :] = jnp.zeros((8, d_ff), F32)
            a_scr[bi, 8 - keep:8, :] = buf_ref[bi]

    mixed = []
    for bi in streams:
        ol = ol_ref[bi]
        o_a = [_mm(ol[:, 2 * KV_LORA * j:2 * KV_LORA * (j + 1)], wuv_ref[j]) for j in range(MLA_HEADS // 2)]
        mixed.append(jnp.concatenate([o.astype(BF16) for o in o_a] + [ob_ref[bi]], axis=-1))
    y = [_mm(mixed[bi], wout_ref[...]) for bi in streams]
    x1 = [x_ref[bi] + _rms(y[bi], npost_ref[...]) for bi in streams]
    u = [_mm(_rms(x1[bi], nfpre_ref[...]), wup_ref[...]) for bi in streams]
    cw = cw_ref[...]
    if pad_front:
        live = (t * tm + lax.broadcasted_iota(jnp.int32, (tm, 1), 0)) >= pad_front
    f = []
    for bi in streams:
        a = u[bi][:, :d_ff]
        if pad_front:
            a = jnp.where(live, a, 0.0)
        a_scr[bi, 8:8 + tm, :] = a
        conv = cb_ref[...] + a * cw[keep:keep + 1]
        for j in range(keep):
            conv = conv + a_scr[bi, 8 - keep + j:8 - keep + j + tm, :] * cw[j:j + 1]
        f.append(_mm(_silu(conv) * u[bi][:, d_ff:], wdn_ref[...]))
    for bi in streams:
        x2 = x1[bi] + _rms(f[bi], nfpost_ref[...])
        xo_ref[bi] = _rms(x2, fin_ref[...]) if last else x2
        a_scr[bi, 0:8, :] = a_scr[bi, tm:tm + 8, :]

    @pl.when(t == nt - 1)
    def _():
        for bi in streams:
            bufo_ref[bi] = a_scr[bi, 8 - keep:8, :]


def _out_ffn(x, ol, ob, buf, w, final_norm, *, n_rows, pad_front, last):
    b, t, d = x.shape
    d_ff = buf.shape[-1]
    tm = _row_tile(n_rows, FFN_ROWS)
    assert tm >= 8 and n_rows <= t
    bb = FFN_STREAMS if b % FFN_STREAMS == 0 else 1
    row = lambda width: pl.BlockSpec((bb, tm, width), lambda i, j: (i, j, 0))
    per_b = pl.BlockSpec((bb, CONV_W - 1, d_ff), lambda i, j: (i, 0, 0))
    weights = (w['w_uv'], w['w_out'], w['norm_mix_post'], w['norm_ffn_pre'], w['w_ffn_up'], w['ffn_conv_w'],
               w['ffn_conv_b'], w['w_ffn_down'], w['norm_ffn_post'], final_norm)
    return pl.pallas_call(
        functools.partial(_out_ffn_body, bb=bb, tm=tm, pad_front=pad_front, last=last),
        grid=(b // bb, n_rows // tm),
        in_specs=[row(d), row(MLA_HEADS * KV_LORA), row(HG_WIDTH), per_b] + [_const_spec(a.shape) for a in weights],
        out_specs=(row(d), per_b),
        out_shape=(jax.ShapeDtypeStruct((b, t, d), F32), jax.ShapeDtypeStruct((b, CONV_W - 1, d_ff), F32)),
        scratch_shapes=[pltpu.VMEM((bb, tm + 8, d_ff), F32)],
        compiler_params=pltpu.CompilerParams(dimension_semantics=("parallel", "arbitrary"),
                                             vmem_limit_bytes=VMEM_LIMIT),
        name="out_ffn",
    )(x, ol, ob, buf, *weights)


def _rope_tables(pos):
    half = QK_ROPE // 2
    inv = ROPE_THETA ** (-jnp.arange(half, dtype=F32) / half)
    ang = pos.astype(F32)[:, None] * inv
    cos, sin = jnp.cos(ang), jnp.sin(ang)
    zero = jnp.zeros_like(cos)
    pad = jnp.zeros((pos.shape[0], LANES - QK_ROPE), F32)
    return (jnp.concatenate([cos, cos, pad], axis=1),
            jnp.concatenate([-sin, zero, pad], axis=1),
            jnp.concatenate([zero, sin, pad], axis=1),
            jnp.tile(cos, (1, MLA_HEADS)), jnp.tile(sin, (1, MLA_HEADS)))


def _layer_weights(l, w_in, norm_mix_pre, norm_mix_post, q_norm, kv_norm, w_uq, w_uk, w_uv, hg_out_norm, w_out,
                   norm_ffn_pre, norm_ffn_post, w_ffn_up, ffn_conv_w, ffn_conv_b, w_ffn_down):
    d = w_in.shape[1]
    head_w = Q_LORA + KV_LORA + QK_ROPE
    w_in_r = jnp.concatenate([w_in[l][:, :head_w], jnp.zeros((d, LANES - QK_ROPE), F32), w_in[l][:, head_w:]], axis=1)
    uq = w_uq[l].reshape(Q_LORA, MLA_HEADS, QK_NOPE + QK_ROPE)
    half = QK_ROPE // 2
    w_uq_r = jnp.concatenate([uq[:, :, :QK_NOPE].reshape(Q_LORA, -1),
                              uq[:, :, QK_NOPE:QK_NOPE + half].reshape(Q_LORA, -1),
                              uq[:, :, QK_NOPE + half:].reshape(Q_LORA, -1)], axis=1)
    src = jnp.arange(2 * MLA_HEADS * half)
    dst = (src % (MLA_HEADS * half)) // half * LANES + src // (MLA_HEADS * half) * half + src % half
    pe_spread = jnp.zeros((2 * MLA_HEADS * half, MLA_HEADS * LANES), F32).at[src, dst].set(1.0)
    uk = jnp.transpose(w_uk[l], (1, 2, 0))
    uk_pair = jnp.zeros((MLA_HEADS // 2, 2 * QK_NOPE, 2 * KV_LORA), F32)
    uk_pair = uk_pair.at[:, :QK_NOPE, :KV_LORA].set(uk[0::2]).at[:, QK_NOPE:, KV_LORA:].set(uk[1::2])
    uv = jnp.transpose(w_uv[l], (1, 0, 2))
    uv_pair = jnp.zeros((MLA_HEADS // 2, 2 * KV_LORA, 2 * V_HEAD), F32)
    uv_pair = uv_pair.at[:, :KV_LORA, :V_HEAD].set(uv[0::2]).at[:, KV_LORA:, V_HEAD:].set(uv[1::2])
    row = lambda a: a.reshape(1, -1)
    return {
        'w_in': w_in_r.astype(BF16), 'norm_mix_pre': row(norm_mix_pre[l]), 'norm_mix_post': row(norm_mix_post[l]),
        'q_norm': row(q_norm[l]), 'kv_norm': row(kv_norm[l]), 'w_uq': w_uq_r.astype(BF16),
        'w_uk': uk_pair.astype(BF16), 'w_uv': uv_pair.astype(BF16), 'pe_spread': pe_spread.astype(BF16),
        'hg_gain': row(hg_out_norm[l]), 'w_out': w_out[l].astype(BF16),
        'norm_ffn_pre': row(norm_ffn_pre[l]), 'norm_ffn_post': row(norm_ffn_post[l]),
        'w_ffn_up': w_ffn_up[l].astype(BF16), 'ffn_conv_w': ffn_conv_w[l], 'ffn_conv_b': row(ffn_conv_b[l]),
        'w_ffn_down': w_ffn_down[l].astype(BF16),
    }


def kernel(x_prompt, x_sample, cache_kv_latent, cache_k_rope, state_hgrn, state_ffn_conv, meta_tokens, w_in, norm_mix_pre, norm_mix_post, q_norm, kv_norm, w_uq, w_uk, w_uv, hg_lower_bounds, hg_out_norm, w_out, norm_ffn_pre, norm_ffn_post, w_ffn_up, ffn_conv_w, ffn_conv_b, w_ffn_down, final_norm):
    depth = w_in.shape[0]
    b, seq, d = x_prompt.shape
    bs, ts, _ = x_sample.shape
    past = cache_kv_latent.shape[2]
    d_ff = ffn_conv_w.shape[-1]
    assert seq % CHUNK == 0

    lb_sm = jax.nn.softmax(hg_lower_bounds.astype(F32), axis=0)
    lb_all = jnp.cumsum(lb_sm, axis=0) - lb_sm[0]

    unit = IN_PROJ_ROWS * KEY_BLOCK // 128
    lm = PAD_FRONT + N_META
    lm_alloc = -(-lm // unit) * unit
    xm = jnp.concatenate([jnp.zeros((1, PAD_FRONT, d), F32), meta_tokens[None],
                          jnp.zeros((1, lm_alloc - lm, d), F32)], axis=1)
    tabs_m = _rope_tables(jnp.arange(lm_alloc, dtype=jnp.int32) - lm)
    seq_alloc = -(-seq // KEY_BLOCK) * KEY_BLOCK
    xp = x_prompt if seq_alloc == seq else jnp.pad(x_prompt, ((0, 0), (0, seq_alloc - seq), (0, 0)))
    tabs_p = _rope_tables(jnp.arange(seq_alloc, dtype=jnp.int32))
    xs = x_sample
    tabs_s = tuple(jnp.tile(a, (bs, 1)) for a in _rope_tables(past + jnp.arange(ts, dtype=jnp.int32)))
    fin = final_norm.reshape(1, -1)
    s0_m = jnp.zeros((1, HG_HEADS, HG_DK, HG_DV), F32)
    buf0_m = jnp.zeros((1, CONV_W - 1, d_ff), F32)
    meta_rows = slice(PAD_FRONT, lm)

    outs = [[] for _ in range(8)]
    for l in range(depth):
        w = _layer_weights(l, w_in, norm_mix_pre, norm_mix_post, q_norm, kv_norm, w_uq, w_uk, w_uv, hg_out_norm,
                           w_out, norm_ffn_pre, norm_ffn_post, w_ffn_up, ffn_conv_w, ffn_conv_b, w_ffn_down)
        w['hg_lb'] = lb_all[l].reshape(1, -1)
        gain = w['hg_gain']
        last = l == depth - 1

        qcat, kcat, kvt, kv_m, pe_m, zh = _in_proj(xm, tabs_m, w, n_valid=lm, pad_front=PAD_FRONT)
        ol = _mla_prompt(qcat, kcat, kvt, None, lp=lm, dummy=PAD_FRONT)
        ob, s_m = _hgrn(zh, s0_m, gain, n_rows=lm, cs=CHUNK)
        xm, buf_m = _out_ffn(xm, ol, ob, buf0_m, w, fin, n_rows=lm, pad_front=PAD_FRONT, last=last)
        prefix = (jnp.pad(kcat[0, meta_rows], ((0, LANES - N_META), (0, 0))),
                  jnp.pad(kvt[0, :, meta_rows], ((0, 0), (0, LANES - N_META))))

        qcat, kcat, kvt, kv, pe, zh = _in_proj(xp, tabs_p, w, n_valid=seq, pad_front=0)
        ol = _mla_prompt(qcat, kcat, kvt, prefix, lp=seq, dummy=0)
        ob, s_new = _hgrn(zh, jnp.broadcast_to(s_m, (b,) + s_m.shape[1:]), gain, n_rows=seq, cs=CHUNK)
        xp, buf = _out_ffn(xp, ol, ob, jnp.broadcast_to(buf_m, (b,) + buf_m.shape[1:]), w, fin, n_rows=seq,
                           pad_front=0, last=last)
        with_meta = lambda m, p: jnp.concatenate(
            [jnp.broadcast_to(m[:, meta_rows], (b, N_META, m.shape[-1])), p[:, :seq]], axis=1)
        for lst, a in zip(outs[:4], (with_meta(kv_m, kv), with_meta(pe_m, pe), s_new, buf)):
            lst.append(a)

        qcat, kcat, _, kv, pe, zh = _in_proj(xs.reshape(1, bs * ts, d), tabs_s, w, n_valid=bs * ts, pad_front=0)
        qcat = jnp.swapaxes(qcat.reshape(MLA_HEADS, bs, ts, QCAT), 0, 1)
        kcat, kv, pe, zh = (a.reshape(bs, ts, -1) for a in (kcat, kv, pe, zh))
        ol = _mla_sample(qcat, kcat, cache_kv_latent[l], cache_k_rope[l])
        ob, s_new = _hgrn(zh, state_hgrn[l], gain, n_rows=ts, cs=ts)
        xs, buf = _out_ffn(xs, ol, ob, state_ffn_conv[l], w, fin, n_rows=ts, pad_front=0, last=last)
        for lst, a in zip(outs[4:], (kv, pe, s_new, buf)):
            lst.append(a)

    y_prompt = xp if seq_alloc == seq else xp[:, :seq]
    return (y_prompt, xs) + tuple(jnp.stack(o) for o in outs)
```

```python
import functools

import jax
import jax.numpy as jnp
from jax import lax
from jax.experimental import pallas as pl
from jax.experimental.pallas import tpu as pltpu

F32 = jnp.float32
BF16 = jnp.bfloat16

CHUNK = 64
N_META = 16
PAD_FRONT = CHUNK - N_META
EPS = 1e-6
MLA_HEADS = 8
QK_NOPE = 64
QK_ROPE = 32
V_HEAD = 64
Q_LORA = 256
KV_LORA = 128
ROPE_THETA = 10000.0
HG_HEADS = 4
HG_DK = 128
HG_DV = 128
CONV_W = 3

HG_KW = HG_HEADS * HG_DK
HG_WIDTH = HG_HEADS * HG_DV
MLA_WIDTH = MLA_HEADS * V_HEAD
LANES = 128
KEY_BLOCK = 256
QCAT = KV_LORA + LANES
VT_ROWS = KV_LORA + 16
PREFIX_KEYS = N_META
IN_PROJ_ROWS = 256
IN_PROJ_STREAMS = 2
Q_SCALE = (QK_NOPE + QK_ROPE) ** -0.5 * 1.4426950408889634
HGRN_FAST_RANGE = 80.0
FFN_ROWS = 256
FFN_STREAMS = 2
HGRN_STREAMS = 8
VMEM_LIMIT = 56 * 1024 * 1024


def _row_tile(n, target):
    if n <= target:
        return n
    best = None
    for t in range(16, target + 1, 16):
        if n % t == 0:
            best = t
    assert best is not None, (n, target)
    return best


def _const_spec(shape):
    nd = len(shape)
    return pl.BlockSpec(shape, lambda *_: (0,) * nd, pipeline_mode=pl.Buffered(1))


def _rms(x, w):
    return x * lax.rsqrt(jnp.mean(x * x, axis=-1, keepdims=True) + EPS) * w


def _silu(x):
    return x * jax.nn.sigmoid(x)


def _mm(a, b):
    return jnp.dot(a.astype(BF16), b.astype(BF16), preferred_element_type=F32)


def _mm_nt(a, b):
    return lax.dot_general(a.astype(BF16), b.astype(BF16), (((1,), (1,)), ((), ())),
                           preferred_element_type=F32)


def _mm_tn(a, b):
    return lax.dot_general(a.astype(BF16), b.astype(BF16), (((0,), (0,)), ((), ())),
                           preferred_element_type=F32)


def _in_proj_body(x_ref, cos_ref, sn_ref, sp_ref, cosq_ref, sinq_ref, npre_ref, win_ref, qn_ref, kvn_ref, wuq_ref,
                  wuk_ref, spread_ref, lb_ref, qcat_ref, kcat_ref, kvt_ref, kv_ref, pe_ref, zh_ref,
                  *, bb, tm, n_valid, pad_front):
    cos, sn, sp = cos_ref[...], sn_ref[...], sp_ref[...]
    lb = lb_ref[...]

    def rope(v):
        return v * cos + pltpu.roll(v, LANES - QK_ROPE // 2, 1) * sn + pltpu.roll(v, QK_ROPE // 2, 1) * sp

    row = pl.program_id(1) * tm + lax.broadcasted_iota(jnp.int32, (tm, 1), 0)
    live = row < n_valid
    real = row >= pad_front
    h = [_rms(jnp.where(live, x_ref[bi], 0.0), npre_ref[...]) for bi in range(bb)]
    zs = [_mm(h[bi], win_ref[...]) for bi in range(bb)]
    nope_w = MLA_HEADS * QK_NOPE
    for bi, z in enumerate(zs):
        off = Q_LORA + KV_LORA + LANES
        hf = z[:, off + HG_KW:off + 2 * HG_KW]
        e = jnp.exp(-jnp.abs(hf))
        r = 1.0 / (1.0 + e)
        f = lb + (1.0 - lb) * jnp.where(hf >= 0.0, r, e * r)
        zh_ref[bi, :, :HG_KW] = jnp.where(real, _silu(z[:, off:off + HG_KW]), 0.0)
        zh_ref[bi, :, HG_KW:2 * HG_KW] = jnp.where(real, jnp.log(f), 0.0)
        zh_ref[bi, :, 2 * HG_KW:3 * HG_KW] = jnp.where(real, (1.0 - lb) * jnp.where(hf >= 0.0, e * r, r), 0.0)
        zh_ref[bi, :, 3 * HG_KW:] = z[:, off + 2 * HG_KW:]
        kv_lat = _rms(z[:, Q_LORA:Q_LORA + KV_LORA], kvn_ref[...])
        k_pe = rope(z[:, Q_LORA + KV_LORA:Q_LORA + KV_LORA + LANES])
        kv_ref[bi] = kv_lat
        pe_ref[bi] = k_pe[:, :QK_ROPE]
        kcat_ref[bi, :, :KV_LORA] = kv_lat.astype(BF16)
        kcat_ref[bi, :, KV_LORA:] = k_pe.astype(BF16)
        kvt_ref[bi, :KV_LORA, :] = kv_lat.T.astype(BF16)
        kvt_ref[bi, KV_LORA:, :] = jnp.ones((VT_ROWS - KV_LORA, tm), BF16)

        cqn = _rms(z[:, :Q_LORA], qn_ref[...])
        q = _mm(cqn, wuq_ref[...]) * Q_SCALE
        x1, x2 = q[:, nope_w:nope_w + LANES], q[:, nope_w + LANES:]
        cq, sq = cosq_ref[...], sinq_ref[...]
        roped = jnp.concatenate([x1 * cq - x2 * sq, x1 * sq + x2 * cq], axis=1)
        pe_heads = _mm(roped, spread_ref[...])
        for j in range(MLA_HEADS // 2):
            ql = _mm(q[:, 2 * QK_NOPE * j:2 * QK_NOPE * (j + 1)], wuk_ref[j])
            for hh in range(2):
                hd = 2 * j + hh
                qcat_ref[bi, hd, :, :KV_LORA] = ql[:, KV_LORA * hh:KV_LORA * (hh + 1)].astype(BF16)
                qcat_ref[bi, hd, :, KV_LORA:] = pe_heads[:, LANES * hd:LANES * (hd + 1)].astype(BF16)


def _in_proj(x, tabs, w, *, n_valid, pad_front):
    bx, t, d = x.shape
    tm = _row_tile(t, IN_PROJ_ROWS)
    assert tm % LANES == 0 or tm == t
    nt = t // tm
    bb = IN_PROJ_STREAMS if bx % IN_PROJ_STREAMS == 0 else 1
    row = lambda shape: pl.BlockSpec((bb, tm) + shape, lambda b, i: (b, i) + (0,) * len(shape))
    tab = pl.BlockSpec((tm, LANES), lambda b, i: (i, 0))
    zw = 3 * HG_KW + 2 * HG_WIDTH
    out_shape = (
        jax.ShapeDtypeStruct((bx, MLA_HEADS, t, QCAT), BF16),
        jax.ShapeDtypeStruct((bx, t, QCAT), BF16),
        jax.ShapeDtypeStruct((bx, VT_ROWS, t), BF16),
        jax.ShapeDtypeStruct((bx, t, KV_LORA), F32),
        jax.ShapeDtypeStruct((bx, t, QK_ROPE), F32),
        jax.ShapeDtypeStruct((bx, t, zw), F32),
    )
    out_specs = (
        pl.BlockSpec((bb, MLA_HEADS, tm, QCAT), lambda b, i: (b, 0, i, 0)),
        row((QCAT,)),
        pl.BlockSpec((bb, VT_ROWS, tm), lambda b, i: (b, 0, i)),
        row((KV_LORA,)), row((QK_ROPE,)), row((zw,)),
    )
    weights = (w['norm_mix_pre'], w['w_in'], w['q_norm'], w['kv_norm'], w['w_uq'], w['w_uk'], w['pe_spread'],
               w['hg_lb'])
    return pl.pallas_call(
        functools.partial(_in_proj_body, bb=bb, tm=tm, n_valid=n_valid, pad_front=pad_front),
        grid=(bx // bb, nt),
        in_specs=[row((d,))] + [tab] * len(tabs) + [_const_spec(a.shape) for a in weights],
        out_specs=out_specs,
        out_shape=out_shape,
        compiler_params=pltpu.CompilerParams(dimension_semantics=("parallel", "parallel"),
                                             vmem_limit_bytes=VMEM_LIMIT),
        name="in_proj",
    )(x, *tabs, *weights)


def _mla_prompt_body(*refs, qt, j0, aliased, dummy, prefix):
    q_ref, k_ref, vt_ref = refs[:3]
    kpre_ref, vtpre_ref = refs[3:5] if prefix else (None, None)
    o_ref, m_scr, acc_scr = refs[-3:]
    j = pl.program_id(1) + j0
    width = MLA_HEADS * qt
    lower_half = lax.broadcasted_iota(jnp.int32, (CHUNK, LANES), 1) < CHUNK

    def hidden(rows, cols=LANES):
        return jnp.full((rows, cols), -jnp.inf, F32)

    def mask_dummy(s):
        return jnp.concatenate([hidden(dummy, s.shape[1]), s[dummy:]], axis=0) if dummy else s

    def mask_diagonal(s):
        strips = []
        for c0 in range(0, width, LANES):
            t = s[:, c0:c0 + LANES]
            if (c0 % qt) // LANES == 0:
                rows = [t[:CHUNK], jnp.where(lower_half, -jnp.inf, t[CHUNK:2 * CHUNK]), hidden(2 * CHUNK)]
            else:
                rows = [t[:3 * CHUNK], jnp.where(lower_half, -jnp.inf, t[3 * CHUNK:KEY_BLOCK])]
            if prefix:
                rows += [t[KEY_BLOCK:KEY_BLOCK + PREFIX_KEYS], hidden(LANES - PREFIX_KEYS)]
            strips.append(jnp.concatenate(rows, axis=0))
        return jnp.concatenate(strips, axis=1)

    def keys_of(kb):
        start = kb * KEY_BLOCK
        return pl.ds(start if isinstance(kb, int) else pl.multiple_of(start, KEY_BLOCK), KEY_BLOCK)

    def scores(kb, diagonal=False):
        keys = k_ref[0, keys_of(kb), :]
        if diagonal and prefix:
            keys = jnp.concatenate([keys, kpre_ref[...]], axis=0)
        return _mm_nt(keys, q_ref[0].reshape(width, QCAT))

    def absorb(s, kb, diagonal=False):
        if diagonal:
            s = mask_diagonal(s)
        if isinstance(kb, int) and kb == 0:
            s = mask_dummy(s)
        vt = vt_ref[0, :, keys_of(kb)]
        for hd in range(MLA_HEADS):
            cols = slice(qt * hd, qt * (hd + 1))
            m_new = jnp.max(s[:, cols], axis=0, keepdims=True)
            if not diagonal:
                m_old = m_scr[:, cols]
                m_new = jnp.maximum(m_old, m_new)
            p = jnp.exp2(s[:, cols] - m_new)
            pv = _mm(vt, p[:KEY_BLOCK])
            if diagonal and prefix:
                pv = pv + _mm(vtpre_ref[...], p[KEY_BLOCK:])
            acc_scr[:, cols] = pv if diagonal else jnp.exp2(m_old - m_new) * acc_scr[:, cols] + pv
            m_scr[:, cols] = m_new

    def pair(a, b, diagonal=False):
        s_a, s_b = scores(a, diagonal), scores(b)
        absorb(s_a, a, diagonal)
        absorb(s_b, b)

    def finish():
        for hd in range(MLA_HEADS):
            acc = acc_scr[:, qt * hd:qt * (hd + 1)]
            o_t = acc[:KV_LORA] * (1.0 / acc[KV_LORA:KV_LORA + 1])
            o_ref[0, :, KV_LORA * hd:KV_LORA * (hd + 1)] = o_t.T.astype(BF16)

    @pl.when(j == 0)
    def _():
        absorb(scores(0, True), 0, True)

    n_blocks = k_ref.shape[1] // KEY_BLOCK
    if n_blocks < 2:
        return finish()

    @pl.when(j == 1)
    def _():
        pair(1, 0, True)

    if n_blocks < 3:
        return finish()

    @pl.when(j > 1)
    def _():
        pair(j, j - 1, True)
        n_inner = (j - 2) // 2

        def inner(p, carry):
            pair(j - 2 * p, j - 2 * p - 1)
            return carry

        lax.fori_loop(1, n_inner + 1, inner, 0)

        @pl.when(j % 2 == 0)
        def _():
            absorb(scores(0), 0)

        @pl.when(j % 2 == 1)
        def _():
            pair(1, 0)

    finish()


def _mla_prompt_call(qcat, kcat, kvt, prefix, prev, *, qt, j0, steps, dummy):
    b, _, t, _ = qcat.shape
    per_block = KEY_BLOCK // qt
    in_specs = [pl.BlockSpec((1, MLA_HEADS, qt, QCAT), lambda i, j: (i, 0, (j + j0) * per_block, 0)),
                pl.BlockSpec((1, t, QCAT), lambda i, j: (i, 0, 0)),
                pl.BlockSpec((1, VT_ROWS, t), lambda i, j: (i, 0, 0))]
    args = (qcat, kcat, kvt)
    if prefix is not None:
        in_specs += [_const_spec(a.shape) for a in prefix]
        args += tuple(prefix)
    if prev is not None:
        in_specs.append(pl.BlockSpec(memory_space=pl.ANY))
        args += (prev,)
    return pl.pallas_call(
        functools.partial(_mla_prompt_body, qt=qt, j0=j0, aliased=prev is not None, dummy=dummy,
                          prefix=prefix is not None),
        grid=(b, steps),
        in_specs=in_specs,
        out_specs=pl.BlockSpec((1, qt, MLA_HEADS * KV_LORA), lambda i, j: (i, (j + j0) * per_block, 0)),
        out_shape=jax.ShapeDtypeStruct((b, t, MLA_HEADS * KV_LORA), BF16),
        input_output_aliases={len(args) - 1: 0} if prev is not None else {},
        scratch_shapes=[pltpu.VMEM((1, MLA_HEADS * qt), F32),
                        pltpu.VMEM((VT_ROWS, MLA_HEADS * qt), F32)],
        compiler_params=pltpu.CompilerParams(dimension_semantics=("parallel", "parallel"),
                                             vmem_limit_bytes=VMEM_LIMIT),
        name="mla_prompt",
    )(*args)


def _mla_prompt(qcat, kcat, kvt, prefix, *, lp, dummy):
    t = qcat.shape[2]
    assert t % KEY_BLOCK == 0 and lp <= t
    full, rest = divmod(lp, KEY_BLOCK)
    if rest > LANES:
        full, rest = full + 1, 0
    out = None
    if full:
        out = _mla_prompt_call(qcat, kcat, kvt, prefix, None, qt=KEY_BLOCK, j0=0, steps=full, dummy=dummy)
    if rest:
        out = _mla_prompt_call(qcat, kcat, kvt, prefix, out, qt=LANES, j0=full, steps=1, dummy=dummy)
    return out


def _mla_sample_body(q_ref, k_ref, ckv_ref, cpe_ref, o_ref, *, ts):
    rows = MLA_HEADS * ts
    q = q_ref[0].reshape(rows, QCAT)
    ckv = ckv_ref[0].astype(BF16)
    knew = k_ref[0]
    s_c = _mm_nt(q[:, :KV_LORA], ckv) + _mm_nt(q[:, KV_LORA:KV_LORA + QK_ROPE], cpe_ref[0])
    s_n = _mm_nt(q, knew)
    m = jnp.maximum(jnp.max(s_c, axis=-1, keepdims=True), jnp.max(s_n, axis=-1, keepdims=True))
    p_c = jnp.exp2(s_c - m)
    p_n = jnp.exp2(s_n - m)
    l = jnp.sum(p_c, axis=-1, keepdims=True) + jnp.sum(p_n, axis=-1, keepdims=True)
    o = (_mm(p_c, ckv) + _mm(p_n, knew[:, :KV_LORA])) / l
    for hd in range(MLA_HEADS):
        o_ref[0, :, KV_LORA * hd:KV_LORA * (hd + 1)] = o[ts * hd:ts * (hd + 1)].astype(BF16)


def _mla_sample(qcat, kcat, cache_kv, cache_pe):
    b, _, ts, _ = qcat.shape
    past = cache_kv.shape[1]
    return pl.pallas_call(
        functools.partial(_mla_sample_body, ts=ts),
        grid=(b,),
        in_specs=[pl.BlockSpec((1, MLA_HEADS, ts, QCAT), lambda i: (i, 0, 0, 0)),
                  pl.BlockSpec((1, ts, QCAT), lambda i: (i, 0, 0)),
                  pl.BlockSpec((1, past, KV_LORA), lambda i: (i, 0, 0)),
                  pl.BlockSpec((1, past, QK_ROPE), lambda i: (i, 0, 0))],
        out_specs=pl.BlockSpec((1, ts, MLA_HEADS * KV_LORA), lambda i: (i, 0, 0)),
        out_shape=jax.ShapeDtypeStruct((b, ts, MLA_HEADS * KV_LORA), BF16),
        compiler_params=pltpu.CompilerParams(dimension_semantics=("parallel",),
                                             vmem_limit_bytes=VMEM_LIMIT),
        name="mla_sample",
    )(qcat, kcat, cache_kv, cache_pe)


def _hgrn_body(zh_ref, s0_ref, gain_ref, ob_ref, sout_ref, st_ref, g_scr, q_scr, k_scr, v_scr, o_scr, oi_scr,
               *, bb, cs):
    c = pl.program_id(1)
    nc = pl.num_programs(1)
    streams = range(bb)
    heads = [slice(HG_DK * hd, HG_DK * (hd + 1)) for hd in range(HG_HEADS)]

    @pl.when(c == 0)
    def _():
        for bi in streams:
            for hd in range(HG_HEADS):
                st_ref[bi, hd] = s0_ref[bi, hd].T

    q = zh_ref[:, :, :HG_KW]
    gl = zh_ref[:, :, HG_KW:2 * HG_KW]
    k = zh_ref[:, :, 2 * HG_KW:3 * HG_KW]
    v = zh_ref[:, :, 3 * HG_KW:3 * HG_KW + HG_WIDTH]
    t_idx = lax.broadcasted_iota(jnp.int32, (cs, cs), 0)
    s_idx = lax.broadcasted_iota(jnp.int32, (cs, cs), 1)
    causal = t_idx >= s_idx
    tri = causal.astype(F32)
    g = jnp.stack([jnp.dot(tri, gl[bi], precision=lax.Precision.HIGHEST, preferred_element_type=F32)
                   for bi in streams])
    g_tot = g[:, cs - 1:cs, :]
    any_out_of_range = jnp.max(-g_tot) > HGRN_FAST_RANGE
    qt = q * jnp.exp(g)
    khat = k * jnp.exp(g_tot - g)
    kt = k * jnp.exp(-g)
    a = [[jnp.where(causal, _mm_nt(qt[bi][:, sl], kt[bi][:, sl]), 0.0) for sl in heads] for bi in streams]
    for bi in streams:
        o_scr[bi] = jnp.concatenate([_mm(a[bi][hd], v[bi][:, sl]) for hd, sl in enumerate(heads)], axis=-1)

    o_inter = [[_mm_nt(qt[bi][:, sl], st_ref[bi, hd]) for hd, sl in enumerate(heads)] for bi in streams]
    kv_new = [[_mm_tn(v[bi][:, sl], khat[bi][:, sl]) for sl in heads] for bi in streams]
    decay = jnp.exp(g_tot)
    for bi in streams:
        oi_scr[bi] = jnp.concatenate(o_inter[bi], axis=-1)
        for hd, sl in enumerate(heads):
            st_ref[bi, hd] = decay[bi][:, sl] * st_ref[bi, hd] + kv_new[bi][hd]

    @pl.when(any_out_of_range)
    def _():
        for bi in streams:
            @pl.when(jnp.max(-g_tot[bi]) > HGRN_FAST_RANGE)
            def _(bi=bi):
                g_scr[...] = g[bi]
                q_scr[...] = q[bi]
                k_scr[...] = k[bi]
                v_scr[...] = v[bi]
                o_scr[bi] = jnp.zeros((cs, HG_WIDTH), F32)
                t_col = lax.broadcasted_iota(jnp.int32, (cs, 1), 0)

                def cols(i, carry):
                    base = pl.multiple_of(i * 8, 8)
                    g8, k8, v8 = (ref[pl.ds(base, 8), :] for ref in (g_scr, k_scr, v_scr))
                    for j in range(8):
                        dec = jnp.exp(jnp.where(t_col >= base + j, g_scr[...] - g8[j:j + 1], -jnp.inf))
                        pr = q_scr[...] * dec * k8[j:j + 1]
                        for sl in heads:
                            o_scr[bi, :, sl] += jnp.sum(pr[:, sl], axis=-1, keepdims=True) * v8[j:j + 1, sl]
                    return carry

                lax.fori_loop(0, cs // 8, cols, 0)

    gain = gain_ref[...]
    for bi in streams:
        o = o_scr[bi] + oi_scr[bi]
        o = [_rms(o[:, sl], gain) for sl in heads]
        ob_ref[bi] = (jnp.concatenate(o, axis=-1) * _silu(zh_ref[bi, :, 3 * HG_KW + HG_WIDTH:])).astype(BF16)

    @pl.when(c == nc - 1)
    def _():
        for bi in streams:
            for hd in range(HG_HEADS):
                sout_ref[bi, hd] = st_ref[bi, hd].T


def _hgrn(zh, s0, gain, *, n_rows, cs):
    b, t, zw = zh.shape
    assert n_rows % cs == 0 and n_rows <= t
    bb = HGRN_STREAMS if b % HGRN_STREAMS == 0 else 1
    return pl.pallas_call(
        functools.partial(_hgrn_body, bb=bb, cs=cs),
        grid=(b // bb, n_rows // cs),
        in_specs=[pl.BlockSpec((bb, cs, zw), lambda i, c: (i, c, 0)),
                  pl.BlockSpec((bb, HG_HEADS, HG_DK, HG_DV), lambda i, c: (i, 0, 0, 0)),
                  _const_spec(gain.shape)],
        out_specs=(pl.BlockSpec((bb, cs, HG_WIDTH), lambda i, c: (i, c, 0)),
                   pl.BlockSpec((bb, HG_HEADS, HG_DK, HG_DV), lambda i, c: (i, 0, 0, 0))),
        out_shape=(jax.ShapeDtypeStruct((b, t, HG_WIDTH), BF16),
                   jax.ShapeDtypeStruct((b, HG_HEADS, HG_DK, HG_DV), F32)),
        scratch_shapes=[pltpu.VMEM((bb, HG_HEADS, HG_DV, HG_DK), F32)]
        + [pltpu.VMEM((cs, HG_KW), F32) for _ in range(4)] + [pltpu.VMEM((bb, cs, HG_KW), F32) for _ in range(2)],
        compiler_params=pltpu.CompilerParams(dimension_semantics=("parallel", "arbitrary"),
                                             vmem_limit_bytes=VMEM_LIMIT),
        name="hgrn",
    )(zh, s0, gain)


def _out_ffn_body(x_ref, ol_ref, ob_ref, buf_ref, wuv_ref, wout_ref, npost_ref, nfpre_ref, wup_ref, cw_ref,
                  cb_ref, wdn_ref, nfpost_ref, fin_ref, xo_ref, bufo_ref, a_scr, *, bb, tm, pad_front, last):
    t = pl.program_id(1)
    nt = pl.num_programs(1)
    d_ff = cw_ref.shape[1]
    keep = CONV_W - 1
    streams = range(bb)

    @pl.when(t == 0)
    def _():
        for bi in streams:
            a_scr[bi, 0:8, :] = jnp.zeros((8, d_ff), F32)
            a_scr[bi, 8 - keep:8, :] = buf_ref[bi]

    mixed = []
    for bi in streams:
        ol = ol_ref[bi]
        o_a = [_mm(ol[:, 2 * KV_LORA * j:2 * KV_LORA * (j + 1)], wuv_ref[j]) for j in range(MLA_HEADS // 2)]
        mixed.append(jnp.concatenate([o.astype(BF16) for o in o_a] + [ob_ref[bi]], axis=-1))
    y = [_mm(mixed[bi], wout_ref[...]) for bi in streams]
    x1 = [x_ref[bi] + _rms(y[bi], npost_ref[...]) for bi in streams]
    u = [_mm(_rms(x1[bi], nfpre_ref[...]), wup_ref[...]) for bi in streams]
    cw = cw_ref[...]
    if pad_front:
        live = (t * tm + lax.broadcasted_iota(jnp.int32, (tm, 1), 0)) >= pad_front
    f = []
    for bi in streams:
        a = u[bi][:, :d_ff]
        if pad_front:
            a = jnp.where(live, a, 0.0)
        a_scr[bi, 8:8 + tm, :] = a
        conv = cb_ref[...] + a * cw[keep:keep + 1]
        for j in range(keep):
            conv = conv + a_scr[bi, 8 - keep + j:8 - keep + j + tm, :] * cw[j:j + 1]
        f.append(_mm(_silu(conv) * u[bi][:, d_ff:], wdn_ref[...]))
    for bi in streams:
        x2 = x1[bi] + _rms(f[bi], nfpost_ref[...])
        xo_ref[bi] = _rms(x2, fin_ref[...]) if last else x2
        a_scr[bi, 0:8, :] = a_scr[bi, tm:tm + 8, :]

    @pl.when(t == nt - 1)
    def _():
        for bi in streams:
            bufo_ref[bi] = a_scr[bi, 8 - keep:8, :]


def _out_ffn(x, ol, ob, buf, w, final_norm, *, n_rows, pad_front, last):
    b, t, d = x.shape
    d_ff = buf.shape[-1]
    tm = _row_tile(n_rows, FFN_ROWS)
    assert tm >= 8 and n_rows <= t
    bb = FFN_STREAMS if b % FFN_STREAMS == 0 else 1
    row = lambda width: pl.BlockSpec((bb, tm, width), lambda i, j: (i, j, 0))
    per_b = pl.BlockSpec((bb, CONV_W - 1, d_ff), lambda i, j: (i, 0, 0))
    weights = (w['w_uv'], w['w_out'], w['norm_mix_post'], w['norm_ffn_pre'], w['w_ffn_up'], w['ffn_conv_w'],
               w['ffn_conv_b'], w['w_ffn_down'], w['norm_ffn_post'], final_norm)
    return pl.pallas_call(
        functools.partial(_out_ffn_body, bb=bb, tm=tm, pad_front=pad_front, last=last),
        grid=(b // bb, n_rows // tm),
        in_specs=[row(d), row(MLA_HEADS * KV_LORA), row(HG_WIDTH), per_b] + [_const_spec(a.shape) for a in weights],
        out_specs=(row(d), per_b),
        out_shape=(jax.ShapeDtypeStruct((b, t, d), F32), jax.ShapeDtypeStruct((b, CONV_W - 1, d_ff), F32)),
        scratch_shapes=[pltpu.VMEM((bb, tm + 8, d_ff), F32)],
        compiler_params=pltpu.CompilerParams(dimension_semantics=("parallel", "arbitrary"),
                                             vmem_limit_bytes=VMEM_LIMIT),
        name="out_ffn",
    )(x, ol, ob, buf, *weights)


def _rope_tables(pos):
    half = QK_ROPE // 2
    inv = ROPE_THETA ** (-jnp.arange(half, dtype=F32) / half)
    ang = pos.astype(F32)[:, None] * inv
    cos, sin = jnp.cos(ang), jnp.sin(ang)
    zero = jnp.zeros_like(cos)
    pad = jnp.zeros((pos.shape[0], LANES - QK_ROPE), F32)
    return (jnp.concatenate([cos, cos, pad], axis=1),
            jnp.concatenate([-sin, zero, pad], axis=1),
            jnp.concatenate([zero, sin, pad], axis=1),
            jnp.tile(cos, (1, MLA_HEADS)), jnp.tile(sin, (1, MLA_HEADS)))


def _layer_weights(l, w_in, norm_mix_pre, norm_mix_post, q_norm, kv_norm, w_uq, w_uk, w_uv, hg_out_norm, w_out,
                   norm_ffn_pre, norm_ffn_post, w_ffn_up, ffn_conv_w, ffn_conv_b, w_ffn_down):
    d = w_in.shape[1]
    head_w = Q_LORA + KV_LORA + QK_ROPE
    w_in_r = jnp.concatenate([w_in[l][:, :head_w], jnp.zeros((d, LANES - QK_ROPE), F32), w_in[l][:, head_w:]], axis=1)
    uq = w_uq[l].reshape(Q_LORA, MLA_HEADS, QK_NOPE + QK_ROPE)
    half = QK_ROPE // 2
    w_uq_r = jnp.concatenate([uq[:, :, :QK_NOPE].reshape(Q_LORA, -1),
                              uq[:, :, QK_NOPE:QK_NOPE + half].reshape(Q_LORA, -1),
                              uq[:, :, QK_NOPE + half:].reshape(Q_LORA, -1)], axis=1)
    src = jnp.arange(2 * MLA_HEADS * half)
    dst = (src % (MLA_HEADS * half)) // half * LANES + src // (MLA_HEADS * half) * half + src % half
    pe_spread = jnp.zeros((2 * MLA_HEADS * half, MLA_HEADS * LANES), F32).at[src, dst].set(1.0)
    uk = jnp.transpose(w_uk[l], (1, 2, 0))
    uk_pair = jnp.zeros((MLA_HEADS // 2, 2 * QK_NOPE, 2 * KV_LORA), F32)
    uk_pair = uk_pair.at[:, :QK_NOPE, :KV_LORA].set(uk[0::2]).at[:, QK_NOPE:, KV_LORA:].set(uk[1::2])
    uv = jnp.transpose(w_uv[l], (1, 0, 2))
    uv_pair = jnp.zeros((MLA_HEADS // 2, 2 * KV_LORA, 2 * V_HEAD), F32)
    uv_pair = uv_pair.at[:, :KV_LORA, :V_HEAD].set(uv[0::2]).at[:, KV_LORA:, V_HEAD:].set(uv[1::2])
    row = lambda a: a.reshape(1, -1)
    return {
        'w_in': w_in_r.astype(BF16), 'norm_mix_pre': row(norm_mix_pre[l]), 'norm_mix_post': row(norm_mix_post[l]),
        'q_norm': row(q_norm[l]), 'kv_norm': row(kv_norm[l]), 'w_uq': w_uq_r.astype(BF16),
        'w_uk': uk_pair.astype(BF16), 'w_uv': uv_pair.astype(BF16), 'pe_spread': pe_spread.astype(BF16),
        'hg_gain': row(hg_out_norm[l]), 'w_out': w_out[l].astype(BF16),
        'norm_ffn_pre': row(norm_ffn_pre[l]), 'norm_ffn_post': row(norm_ffn_post[l]),
        'w_ffn_up': w_ffn_up[l].astype(BF16), 'ffn_conv_w': ffn_conv_w[l], 'ffn_conv_b': row(ffn_conv_b[l]),
        'w_ffn_down': w_ffn_down[l].astype(BF16),
    }


def kernel(x_prompt, x_sample, cache_kv_latent, cache_k_rope, state_hgrn, state_ffn_conv, meta_tokens, w_in, norm_mix_pre, norm_mix_post, q_norm, kv_norm, w_uq, w_uk, w_uv, hg_lower_bounds, hg_out_norm, w_out, norm_ffn_pre, norm_ffn_post, w_ffn_up, ffn_conv_w, ffn_conv_b, w_ffn_down, final_norm):
    depth = w_in.shape[0]
    b, seq, d = x_prompt.shape
    bs, ts, _ = x_sample.shape
    past = cache_kv_latent.shape[2]
    d_ff = ffn_conv_w.shape[-1]
    assert seq % CHUNK == 0

    lb_sm = jax.nn.softmax(hg_lower_bounds.astype(F32), axis=0)
    lb_all = jnp.cumsum(lb_sm, axis=0) - lb_sm[0]

    unit = IN_PROJ_ROWS * KEY_BLOCK // 128
    lm = PAD_FRONT + N_META
    lm_alloc = -(-lm // unit) * unit
    xm = jnp.concatenate([jnp.zeros((1, PAD_FRONT, d), F32), meta_tokens[None],
                          jnp.zeros((1, lm_alloc - lm, d), F32)], axis=1)
    tabs_m = _rope_tables(jnp.arange(lm_alloc, dtype=jnp.int32) - lm)
    seq_alloc = -(-seq // KEY_BLOCK) * KEY_BLOCK
    xp = x_prompt if seq_alloc == seq else jnp.pad(x_prompt, ((0, 0), (0, seq_alloc - seq), (0, 0)))
    tabs_p = _rope_tables(jnp.arange(seq_alloc, dtype=jnp.int32))
    xs = x_sample
    tabs_s = tuple(jnp.tile(a, (bs, 1)) for a in _rope_tables(past + jnp.arange(ts, dtype=jnp.int32)))
    fin = final_norm.reshape(1, -1)
    s0_m = jnp.zeros((1, HG_HEADS, HG_DK, HG_DV), F32)
    buf0_m = jnp.zeros((1, CONV_W - 1, d_ff), F32)
    meta_rows = slice(PAD_FRONT, lm)

    outs = [[] for _ in range(8)]
    for l in range(depth):
        w = _layer_weights(l, w_in, norm_mix_pre, norm_mix_post, q_norm, kv_norm, w_uq, w_uk, w_uv, hg_out_norm,
                           w_out, norm_ffn_pre, norm_ffn_post, w_ffn_up, ffn_conv_w, ffn_conv_b, w_ffn_down)
        w['hg_lb'] = lb_all[l].reshape(1, -1)
        gain = w['hg_gain']
        last = l == depth - 1

        qcat, kcat, kvt, kv_m, pe_m, zh = _in_proj(xm, tabs_m, w, n_valid=lm, pad_front=PAD_FRONT)
        ol = _mla_prompt(qcat, kcat, kvt, None, lp=lm, dummy=PAD_FRONT)
        ob, s_m = _hgrn(zh, s0_m, gain, n_rows=lm, cs=CHUNK)
        xm, buf_m = _out_ffn(xm, ol, ob, buf0_m, w, fin, n_rows=lm, pad_front=PAD_FRONT, last=last)
        prefix = (jnp.pad(kcat[0, meta_rows], ((0, LANES - N_META), (0, 0))),
                  jnp.pad(kvt[0, :, meta_rows], ((0, 0), (0, LANES - N_META))))

        qcat, kcat, kvt, kv, pe, zh = _in_proj(xp, tabs_p, w, n_valid=seq, pad_front=0)
        ol = _mla_prompt(qcat, kcat, kvt, prefix, lp=seq, dummy=0)
        ob, s_new = _hgrn(zh, jnp.broadcast_to(s_m, (b,) + s_m.shape[1:]), gain, n_rows=seq, cs=CHUNK)
        xp, buf = _out_ffn(xp, ol, ob, jnp.broadcast_to(buf_m, (b,) + buf_m.shape[1:]), w, fin, n_rows=seq,
                           pad_front=0, last=last)
        with_meta = lambda m, p: jnp.concatenate(
            [jnp.broadcast_to(m[:, meta_rows], (b, N_META, m.shape[-1])), p[:, :seq]], axis=1)
        for lst, a in zip(outs[:4], (with_meta(kv_m, kv), with_meta(pe_m, pe), s_new, buf)):
            lst.append(a)

        qcat, kcat, _, kv, pe, zh = _in_proj(xs.reshape(1, bs * ts, d), tabs_s, w, n_valid=bs * ts, pad_front=0)
        qcat = jnp.swapaxes(qcat.reshape(MLA_HEADS, bs, ts, QCAT), 0, 1)
        kcat, kv, pe, zh = (a.reshape(bs, ts, -1) for a in (kcat, kv, pe, zh))
        ol = _mla_sample(qcat, kcat, cache_kv_latent[l], cache_k_rope[l])
        ob, s_new = _hgrn(zh, state_hgrn[l], gain, n_rows=ts, cs=ts)
        xs, buf = _out_ffn(xs, ol, ob, state_ffn_conv[l], w, fin, n_rows=ts, pad_front=0, last=last)
        for lst, a in zip(outs[4:], (kv, pe, s_new, buf)):
            lst.append(a)

    y_prompt = xp if seq_alloc == seq else xp[:, :seq]
    return (y_prompt, xs) + tuple(jnp.stack(o) for o in outs)
```

```python
import functools

import jax
import jax.numpy as jnp
from jax import lax
from jax.experimental import pallas as pl
from jax.experimental.pallas import tpu as pltpu

F32 = jnp.float32
BF16 = jnp.bfloat16

CHUNK = 64
N_META = 16
PAD_FRONT = CHUNK - N_META
EPS = 1e-6
MLA_HEADS = 8
QK_NOPE = 64
QK_ROPE = 32
V_HEAD = 64
Q_LORA = 256
KV_LORA = 128
ROPE_THETA = 10000.0
HG_HEADS = 4
HG_DK = 128
HG_DV = 128
CONV_W = 3

HG_KW = HG_HEADS * HG_DK
HG_WIDTH = HG_HEADS * HG_DV
MLA_WIDTH = MLA_HEADS * V_HEAD
LANES = 128
KEY_BLOCK = 256
QCAT = KV_LORA + LANES
VT_ROWS = KV_LORA + 16
SOFTMAX_FAST_RANGE = 100.0
PREFIX_KEYS = N_META
IN_PROJ_ROWS = 256
IN_PROJ_STREAMS = 2
Q_SCALE = (QK_NOPE + QK_ROPE) ** -0.5 * 1.4426950408889634
HGRN_FAST_RANGE = 80.0
FFN_ROWS = 256
FFN_STREAMS = 2
HGRN_STREAMS = 8
VMEM_LIMIT = 56 * 1024 * 1024


def _row_tile(n, target):
    if n <= target:
        return n
    best = None
    for t in range(16, target + 1, 16):
        if n % t == 0:
            best = t
    assert best is not None, (n, target)
    return best


def _const_spec(shape):
    nd = len(shape)
    return pl.BlockSpec(shape, lambda *_: (0,) * nd, pipeline_mode=pl.Buffered(1))


def _rms(x, w):
    return x * lax.rsqrt(jnp.mean(x * x, axis=-1, keepdims=True) + EPS) * w


def _silu(x):
    return x * jax.nn.sigmoid(x)


def _mm(a, b):
    return jnp.dot(a.astype(BF16), b.astype(BF16), preferred_element_type=F32)


def _mm_nt(a, b):
    return lax.dot_general(a.astype(BF16), b.astype(BF16), (((1,), (1,)), ((), ())),
                           preferred_element_type=F32)


def _mm_tn(a, b):
    return lax.dot_general(a.astype(BF16), b.astype(BF16), (((0,), (0,)), ((), ())),
                           preferred_element_type=F32)


def _in_proj_body(x_ref, cos_ref, sn_ref, sp_ref, cosq_ref, sinq_ref, npre_ref, win_ref, qn_ref, kvn_ref, wuq_ref,
                  wuk_ref, spread_ref, lb_ref, qcat_ref, kcat_ref, kvt_ref, kv_ref, pe_ref, zh_ref,
                  *, bb, tm, n_valid, pad_front):
    cos, sn, sp = cos_ref[...], sn_ref[...], sp_ref[...]
    lb = lb_ref[...]

    def rope(v):
        return v * cos + pltpu.roll(v, LANES - QK_ROPE // 2, 1) * sn + pltpu.roll(v, QK_ROPE // 2, 1) * sp

    row = pl.program_id(1) * tm + lax.broadcasted_iota(jnp.int32, (tm, 1), 0)
    live = row < n_valid
    real = row >= pad_front
    h = [_rms(jnp.where(live, x_ref[bi], 0.0), npre_ref[...]) for bi in range(bb)]
    zs = [_mm(h[bi], win_ref[...]) for bi in range(bb)]
    nope_w = MLA_HEADS * QK_NOPE
    for bi, z in enumerate(zs):
        off = Q_LORA + KV_LORA + LANES
        hf = z[:, off + HG_KW:off + 2 * HG_KW]
        e = jnp.exp(-jnp.abs(hf))
        r = 1.0 / (1.0 + e)
        f = lb + (1.0 - lb) * jnp.where(hf >= 0.0, r, e * r)
        zh_ref[bi, :, :HG_KW] = jnp.where(real, _silu(z[:, off:off + HG_KW]), 0.0)
        zh_ref[bi, :, HG_KW:2 * HG_KW] = jnp.where(real, jnp.log(f), 0.0)
        zh_ref[bi, :, 2 * HG_KW:3 * HG_KW] = jnp.where(real, (1.0 - lb) * jnp.where(hf >= 0.0, e * r, r), 0.0)
        zh_ref[bi, :, 3 * HG_KW:] = z[:, off + 2 * HG_KW:]
        kv_lat = _rms(z[:, Q_LORA:Q_LORA + KV_LORA], kvn_ref[...])
        k_pe = rope(z[:, Q_LORA + KV_LORA:Q_LORA + KV_LORA + LANES])
        kv_ref[bi] = kv_lat
        pe_ref[bi] = k_pe[:, :QK_ROPE]
        kcat_ref[bi, :, :KV_LORA] = kv_lat.astype(BF16)
        kcat_ref[bi, :, KV_LORA:] = k_pe.astype(BF16)
        kvt_ref[bi, :KV_LORA, :] = kv_lat.T.astype(BF16)
        kvt_ref[bi, KV_LORA:, :] = jnp.ones((VT_ROWS - KV_LORA, tm), BF16)

        cqn = _rms(z[:, :Q_LORA], qn_ref[...])
        q = _mm(cqn, wuq_ref[...]) * Q_SCALE
        x1, x2 = q[:, nope_w:nope_w + LANES], q[:, nope_w + LANES:]
        cq, sq = cosq_ref[...], sinq_ref[...]
        roped = jnp.concatenate([x1 * cq - x2 * sq, x1 * sq + x2 * cq], axis=1)
        pe_heads = _mm(roped, spread_ref[...])
        for j in range(MLA_HEADS // 2):
            ql = _mm(q[:, 2 * QK_NOPE * j:2 * QK_NOPE * (j + 1)], wuk_ref[j])
            for hh in range(2):
                hd = 2 * j + hh
                qcat_ref[bi, hd, :, :KV_LORA] = ql[:, KV_LORA * hh:KV_LORA * (hh + 1)].astype(BF16)
                qcat_ref[bi, hd, :, KV_LORA:] = pe_heads[:, LANES * hd:LANES * (hd + 1)].astype(BF16)


def _in_proj(x, tabs, w, *, n_valid, pad_front):
    bx, t, d = x.shape
    tm = _row_tile(t, IN_PROJ_ROWS)
    assert tm % LANES == 0 or tm == t
    nt = t // tm
    bb = IN_PROJ_STREAMS if bx % IN_PROJ_STREAMS == 0 else 1
    row = lambda shape: pl.BlockSpec((bb, tm) + shape, lambda b, i: (b, i) + (0,) * len(shape))
    tab = pl.BlockSpec((tm, LANES), lambda b, i: (i, 0))
    zw = 3 * HG_KW + 2 * HG_WIDTH
    out_shape = (
        jax.ShapeDtypeStruct((bx, MLA_HEADS, t, QCAT), BF16),
        jax.ShapeDtypeStruct((bx, t, QCAT), BF16),
        jax.ShapeDtypeStruct((bx, VT_ROWS, t), BF16),
        jax.ShapeDtypeStruct((bx, t, KV_LORA), F32),
        jax.ShapeDtypeStruct((bx, t, QK_ROPE), F32),
        jax.ShapeDtypeStruct((bx, t, zw), F32),
    )
    out_specs = (
        pl.BlockSpec((bb, MLA_HEADS, tm, QCAT), lambda b, i: (b, 0, i, 0)),
        row((QCAT,)),
        pl.BlockSpec((bb, VT_ROWS, tm), lambda b, i: (b, 0, i)),
        row((KV_LORA,)), row((QK_ROPE,)), row((zw,)),
    )
    weights = (w['norm_mix_pre'], w['w_in'], w['q_norm'], w['kv_norm'], w['w_uq'], w['w_uk'], w['pe_spread'],
               w['hg_lb'])
    return pl.pallas_call(
        functools.partial(_in_proj_body, bb=bb, tm=tm, n_valid=n_valid, pad_front=pad_front),
        grid=(bx // bb, nt),
        in_specs=[row((d,))] + [tab] * len(tabs) + [_const_spec(a.shape) for a in weights],
        out_specs=out_specs,
        out_shape=out_shape,
        compiler_params=pltpu.CompilerParams(dimension_semantics=("parallel", "parallel"),
                                             vmem_limit_bytes=VMEM_LIMIT),
        name="in_proj",
    )(x, *tabs, *weights)


def _mla_prompt_body(*refs, qt, j0, aliased, dummy, prefix):
    q_ref, k_ref, vt_ref = refs[:3]
    kpre_ref, vtpre_ref = refs[3:5] if prefix else (None, None)
    o_ref, m_scr, top_scr, acc_scr = refs[-4:]
    j = pl.program_id(1) + j0
    width = MLA_HEADS * qt
    lower_half = lax.broadcasted_iota(jnp.int32, (CHUNK, LANES), 1) < CHUNK

    def hidden(rows, cols=LANES):
        return jnp.full((rows, cols), -jnp.inf, F32)

    def mask_dummy(s):
        return jnp.concatenate([hidden(dummy, s.shape[1]), s[dummy:]], axis=0) if dummy else s

    def mask_diagonal(s):
        strips = []
        for c0 in range(0, width, LANES):
            t = s[:, c0:c0 + LANES]
            if (c0 % qt) // LANES == 0:
                rows = [t[:CHUNK], jnp.where(lower_half, -jnp.inf, t[CHUNK:2 * CHUNK]), hidden(2 * CHUNK)]
            else:
                rows = [t[:3 * CHUNK], jnp.where(lower_half, -jnp.inf, t[3 * CHUNK:KEY_BLOCK])]
            if prefix:
                rows += [t[KEY_BLOCK:KEY_BLOCK + PREFIX_KEYS], hidden(LANES - PREFIX_KEYS)]
            strips.append(jnp.concatenate(rows, axis=0))
        return jnp.concatenate(strips, axis=1)

    def keys_of(kb):
        start = kb * KEY_BLOCK
        return pl.ds(start if isinstance(kb, int) else pl.multiple_of(start, KEY_BLOCK), KEY_BLOCK)

    def scores(kb, diagonal=False):
        keys = k_ref[0, keys_of(kb), :]
        if diagonal and prefix:
            keys = jnp.concatenate([keys, kpre_ref[...]], axis=0)
        return _mm_nt(keys, q_ref[0].reshape(width, QCAT))

    def absorb(s, kb, diagonal=False):
        if diagonal:
            s = mask_diagonal(s)
        if isinstance(kb, int) and kb == 0:
            s = mask_dummy(s)
        vt = vt_ref[0, :, keys_of(kb)]
        for hd in range(MLA_HEADS):
            cols = slice(qt * hd, qt * (hd + 1))
            m_new = jnp.max(s[:, cols], axis=0, keepdims=True)
            if not diagonal:
                m_old = m_scr[:, cols]
                m_new = jnp.maximum(m_old, m_new)
            p = jnp.exp2(s[:, cols] - m_new)
            pv = _mm(vt, p[:KEY_BLOCK])
            if diagonal and prefix:
                pv = pv + _mm(vtpre_ref[...], p[KEY_BLOCK:])
            acc_scr[:, cols] = pv if diagonal else jnp.exp2(m_old - m_new) * acc_scr[:, cols] + pv
            m_scr[:, cols] = m_new

    def absorb_all(blocks):
        for s, kb, diagonal in blocks:
            absorb(s, kb, diagonal)

    def absorb_unscaled(blocks):
        blocks = [(mask_diagonal(s) if diagonal else s, kb, diagonal) for s, kb, diagonal in blocks]
        blocks = [(mask_dummy(s) if isinstance(kb, int) and kb == 0 else s, kb, d) for s, kb, d in blocks]
        values = [vt_ref[0, :, keys_of(kb)] for _, kb, _ in blocks]
        first = blocks[0][2]
        for hd in range(MLA_HEADS):
            cols = slice(qt * hd, qt * (hd + 1))
            tops = [jnp.max(s[:, cols], axis=0, keepdims=True) for s, _, _ in blocks]
            if first:
                shift = tops[0]
                m_scr[:, cols] = shift
                top_scr[:, cols] = functools.reduce(jnp.maximum, tops)
            else:
                shift = m_scr[:, cols]
                top_scr[:, cols] = functools.reduce(jnp.maximum, tops, top_scr[:, cols])
            pv = None
            for (s, _, diagonal), vt in zip(blocks, values):
                p = jnp.exp2(s[:, cols] - shift)
                term = _mm(vt, p[:KEY_BLOCK])
                if diagonal and prefix:
                    term = term + _mm(vtpre_ref[...], p[KEY_BLOCK:])
                pv = term if pv is None else pv + term
            acc_scr[:, cols] = pv if first else acc_scr[:, cols] + pv

    def visit_blocks(absorb_blocks):
        def pair(a, b, diagonal=False):
            s_a, s_b = scores(a, diagonal), scores(b)
            absorb_blocks([(s_a, a, diagonal), (s_b, b, False)])

        def absorb_block(s, kb, diagonal=False):
            absorb_blocks([(s, kb, diagonal)])

        @pl.when(j == 0)
        def _():
            absorb_block(scores(0, True), 0, True)

        n_blocks = k_ref.shape[1] // KEY_BLOCK
        if n_blocks < 2:
            return

        @pl.when(j == 1)
        def _():
            pair(1, 0, True)

        if n_blocks < 3:
            return

        @pl.when(j > 1)
        def _():
            pair(j, j - 1, True)
            n_inner = (j - 2) // 2

            def inner(p, carry):
                pair(j - 2 * p, j - 2 * p - 1)
                return carry

            lax.fori_loop(1, n_inner + 1, inner, 0)

            @pl.when(j % 2 == 0)
            def _():
                absorb_block(scores(0), 0)

            @pl.when(j % 2 == 1)
            def _():
                pair(1, 0)

    def finish():
        for hd in range(MLA_HEADS):
            acc = acc_scr[:, qt * hd:qt * (hd + 1)]
            o_t = acc[:KV_LORA] * (1.0 / acc[KV_LORA:KV_LORA + 1])
            o_ref[0, :, KV_LORA * hd:KV_LORA * (hd + 1)] = o_t.T.astype(BF16)

    visit_blocks(absorb_unscaled)
    runaway = jnp.max(top_scr[...] - m_scr[...]) > SOFTMAX_FAST_RANGE
    finish()

    @pl.when(runaway)
    def _():
        visit_blocks(absorb_all)
        finish()


def _mla_prompt_call(qcat, kcat, kvt, prefix, prev, *, qt, j0, steps, dummy):
    b, _, t, _ = qcat.shape
    per_block = KEY_BLOCK // qt
    in_specs = [pl.BlockSpec((1, MLA_HEADS, qt, QCAT), lambda i, j: (i, 0, (j + j0) * per_block, 0)),
                pl.BlockSpec((1, t, QCAT), lambda i, j: (i, 0, 0)),
                pl.BlockSpec((1, VT_ROWS, t), lambda i, j: (i, 0, 0))]
    args = (qcat, kcat, kvt)
    if prefix is not None:
        in_specs += [_const_spec(a.shape) for a in prefix]
        args += tuple(prefix)
    if prev is not None:
        in_specs.append(pl.BlockSpec(memory_space=pl.ANY))
        args += (prev,)
    return pl.pallas_call(
        functools.partial(_mla_prompt_body, qt=qt, j0=j0, aliased=prev is not None, dummy=dummy,
                          prefix=prefix is not None),
        grid=(b, steps),
        in_specs=in_specs,
        out_specs=pl.BlockSpec((1, qt, MLA_HEADS * KV_LORA), lambda i, j: (i, (j + j0) * per_block, 0)),
        out_shape=jax.ShapeDtypeStruct((b, t, MLA_HEADS * KV_LORA), BF16),
        input_output_aliases={len(args) - 1: 0} if prev is not None else {},
        scratch_shapes=[pltpu.VMEM((1, MLA_HEADS * qt), F32), pltpu.VMEM((1, MLA_HEADS * qt), F32),
                        pltpu.VMEM((VT_ROWS, MLA_HEADS * qt), F32)],
        compiler_params=pltpu.CompilerParams(dimension_semantics=("parallel", "parallel"),
                                             vmem_limit_bytes=VMEM_LIMIT),
        name="mla_prompt",
    )(*args)


def _mla_prompt(qcat, kcat, kvt, prefix, *, lp, dummy):
    t = qcat.shape[2]
    assert t % KEY_BLOCK == 0 and lp <= t
    full, rest = divmod(lp, KEY_BLOCK)
    if rest > LANES:
        full, rest = full + 1, 0
    out = None
    if full:
        out = _mla_prompt_call(qcat, kcat, kvt, prefix, None, qt=KEY_BLOCK, j0=0, steps=full, dummy=dummy)
    if rest:
        out = _mla_prompt_call(qcat, kcat, kvt, prefix, out, qt=LANES, j0=full, steps=1, dummy=dummy)
    return out


def _mla_sample_body(q_ref, k_ref, ckv_ref, cpe_ref, o_ref, *, ts):
    rows = MLA_HEADS * ts
    q = q_ref[0].reshape(rows, QCAT)
    ckv = ckv_ref[0].astype(BF16)
    knew = k_ref[0]
    s_c = _mm_nt(q[:, :KV_LORA], ckv) + _mm_nt(q[:, KV_LORA:KV_LORA + QK_ROPE], cpe_ref[0])
    s_n = _mm_nt(q, knew)
    m = jnp.maximum(jnp.max(s_c, axis=-1, keepdims=True), jnp.max(s_n, axis=-1, keepdims=True))
    p_c = jnp.exp2(s_c - m)
    p_n = jnp.exp2(s_n - m)
    l = jnp.sum(p_c, axis=-1, keepdims=True) + jnp.sum(p_n, axis=-1, keepdims=True)
    o = (_mm(p_c, ckv) + _mm(p_n, knew[:, :KV_LORA])) / l
    for hd in range(MLA_HEADS):
        o_ref[0, :, KV_LORA * hd:KV_LORA * (hd + 1)] = o[ts * hd:ts * (hd + 1)].astype(BF16)


def _mla_sample(qcat, kcat, cache_kv, cache_pe):
    b, _, ts, _ = qcat.shape
    past = cache_kv.shape[1]
    return pl.pallas_call(
        functools.partial(_mla_sample_body, ts=ts),
        grid=(b,),
        in_specs=[pl.BlockSpec((1, MLA_HEADS, ts, QCAT), lambda i: (i, 0, 0, 0)),
                  pl.BlockSpec((1, ts, QCAT), lambda i: (i, 0, 0)),
                  pl.BlockSpec((1, past, KV_LORA), lambda i: (i, 0, 0)),
                  pl.BlockSpec((1, past, QK_ROPE), lambda i: (i, 0, 0))],
        out_specs=pl.BlockSpec((1, ts, MLA_HEADS * KV_LORA), lambda i: (i, 0, 0)),
        out_shape=jax.ShapeDtypeStruct((b, ts, MLA_HEADS * KV_LORA), BF16),
        compiler_params=pltpu.CompilerParams(dimension_semantics=("parallel",),
                                             vmem_limit_bytes=VMEM_LIMIT),
        name="mla_sample",
    )(qcat, kcat, cache_kv, cache_pe)


def _hgrn_body(zh_ref, s0_ref, gain_ref, ob_ref, sout_ref, st_ref, g_scr, q_scr, k_scr, v_scr, o_scr, oi_scr,
               *, bb, cs):
    c = pl.program_id(1)
    nc = pl.num_programs(1)
    streams = range(bb)
    heads = [slice(HG_DK * hd, HG_DK * (hd + 1)) for hd in range(HG_HEADS)]

    @pl.when(c == 0)
    def _():
        for bi in streams:
            for hd in range(HG_HEADS):
                st_ref[bi, hd] = s0_ref[bi, hd].T

    q = zh_ref[:, :, :HG_KW]
    gl = zh_ref[:, :, HG_KW:2 * HG_KW]
    k = zh_ref[:, :, 2 * HG_KW:3 * HG_KW]
    v = zh_ref[:, :, 3 * HG_KW:3 * HG_KW + HG_WIDTH]
    t_idx = lax.broadcasted_iota(jnp.int32, (cs, cs), 0)
    s_idx = lax.broadcasted_iota(jnp.int32, (cs, cs), 1)
    causal = t_idx >= s_idx
    tri = causal.astype(F32)
    g = jnp.stack([jnp.dot(tri, gl[bi], precision=lax.Precision.HIGHEST, preferred_element_type=F32)
                   for bi in streams])
    g_tot = g[:, cs - 1:cs, :]
    any_out_of_range = jnp.max(-g_tot) > HGRN_FAST_RANGE
    qt = q * jnp.exp(g)
    khat = k * jnp.exp(g_tot - g)
    kt = k * jnp.exp(-g)
    a = [[jnp.where(causal, _mm_nt(qt[bi][:, sl], kt[bi][:, sl]), 0.0) for sl in heads] for bi in streams]
    for bi in streams:
        o_scr[bi] = jnp.concatenate([_mm(a[bi][hd], v[bi][:, sl]) for hd, sl in enumerate(heads)], axis=-1)

    o_inter = [[_mm_nt(qt[bi][:, sl], st_ref[bi, hd]) for hd, sl in enumerate(heads)] for bi in streams]
    kv_new = [[_mm_tn(v[bi][:, sl], khat[bi][:, sl]) for sl in heads] for bi in streams]
    decay = jnp.exp(g_tot)
    for bi in streams:
        oi_scr[bi] = jnp.concatenate(o_inter[bi], axis=-1)
        for hd, sl in enumerate(heads):
            st_ref[bi, hd] = decay[bi][:, sl] * st_ref[bi, hd] + kv_new[bi][hd]

    @pl.when(any_out_of_range)
    def _():
        for bi in streams:
            @pl.when(jnp.max(-g_tot[bi]) > HGRN_FAST_RANGE)
            def _(bi=bi):
                g_scr[...] = g[bi]
                q_scr[...] = q[bi]
                k_scr[...] = k[bi]
                v_scr[...] = v[bi]
                o_scr[bi] = jnp.zeros((cs, HG_WIDTH), F32)
                t_col = lax.broadcasted_iota(jnp.int32, (cs, 1), 0)

                def cols(i, carry):
                    base = pl.multiple_of(i * 8, 8)
                    g8, k8, v8 = (ref[pl.ds(base, 8), :] for ref in (g_scr, k_scr, v_scr))
                    for j in range(8):
                        dec = jnp.exp(jnp.where(t_col >= base + j, g_scr[...] - g8[j:j + 1], -jnp.inf))
                        pr = q_scr[...] * dec * k8[j:j + 1]
                        for sl in heads:
                            o_scr[bi, :, sl] += jnp.sum(pr[:, sl], axis=-1, keepdims=True) * v8[j:j + 1, sl]
                    return carry

                lax.fori_loop(0, cs // 8, cols, 0)

    gain = gain_ref[...]
    for bi in streams:
        o = o_scr[bi] + oi_scr[bi]
        o = [_rms(o[:, sl], gain) for sl in heads]
        ob_ref[bi] = (jnp.concatenate(o, axis=-1) * _silu(zh_ref[bi, :, 3 * HG_KW + HG_WIDTH:])).astype(BF16)

    @pl.when(c == nc - 1)
    def _():
        for bi in streams:
            for hd in range(HG_HEADS):
                sout_ref[bi, hd] = st_ref[bi, hd].T


def _hgrn(zh, s0, gain, *, n_rows, cs):
    b, t, zw = zh.shape
    assert n_rows % cs == 0 and n_rows <= t
    bb = HGRN_STREAMS if b % HGRN_STREAMS == 0 else 1
    return pl.pallas_call(
        functools.partial(_hgrn_body, bb=bb, cs=cs),
        grid=(b // bb, n_rows // cs),
        in_specs=[pl.BlockSpec((bb, cs, zw), lambda i, c: (i, c, 0)),
                  pl.BlockSpec((bb, HG_HEADS, HG_DK, HG_DV), lambda i, c: (i, 0, 0, 0)),
                  _const_spec(gain.shape)],
        out_specs=(pl.BlockSpec((bb, cs, HG_WIDTH), lambda i, c: (i, c, 0)),
                   pl.BlockSpec((bb, HG_HEADS, HG_DK, HG_DV), lambda i, c: (i, 0, 0, 0))),
        out_shape=(jax.ShapeDtypeStruct((b, t, HG_WIDTH), BF16),
                   jax.ShapeDtypeStruct((b, HG_HEADS, HG_DK, HG_DV), F32)),
        scratch_shapes=[pltpu.VMEM((bb, HG_HEADS, HG_DV, HG_DK), F32)]
        + [pltpu.VMEM((cs, HG_KW), F32) for _ in range(4)] + [pltpu.VMEM((bb, cs, HG_KW), F32) for _ in range(2)],
        compiler_params=pltpu.CompilerParams(dimension_semantics=("parallel", "arbitrary"),
                                             vmem_limit_bytes=VMEM_LIMIT),
        name="hgrn",
    )(zh, s0, gain)


def _out_ffn_body(x_ref, ol_ref, ob_ref, buf_ref, wuv_ref, wout_ref, npost_ref, nfpre_ref, wup_ref, cw_ref,
                  cb_ref, wdn_ref, nfpost_ref, fin_ref, xo_ref, bufo_ref, a_scr, *, bb, tm, pad_front, last):
    t = pl.program_id(1)
    nt = pl.num_programs(1)
    d_ff = cw_ref.shape[1]
    keep = CONV_W - 1
    streams = range(bb)

    @pl.when(t == 0)
    def _():
        for bi in streams:
            a_scr[bi, 0:8, :] = jnp.zeros((8, d_ff), F32)
            a_scr[bi, 8 - keep:8, :] = buf_ref[bi]

    mixed = []
    for bi in streams:
        ol = ol_ref[bi]
        o_a = [_mm(ol[:, 2 * KV_LORA * j:2 * KV_LORA * (j + 1)], wuv_ref[j]) for j in range(MLA_HEADS // 2)]
        mixed.append(jnp.concatenate([o.astype(BF16) for o in o_a] + [ob_ref[bi]], axis=-1))
    y = [_mm(mixed[bi], wout_ref[...]) for bi in streams]
    x1 = [x_ref[bi] + _rms(y[bi], npost_ref[...]) for bi in streams]
    u = [_mm(_rms(x1[bi], nfpre_ref[...]), wup_ref[...]) for bi in streams]
    cw = cw_ref[...]
    if pad_front:
        live = (t * tm + lax.broadcasted_iota(jnp.int32, (tm, 1), 0)) >= pad_front
    f = []
    for bi in streams:
        a = u[bi][:, :d_ff]
        if pad_front:
            a = jnp.where(live, a, 0.0)
        a_scr[bi, 8:8 + tm, :] = a
        conv = cb_ref[...] + a * cw[keep:keep + 1]
        for j in range(keep):
            conv = conv + a_scr[bi, 8 - keep + j:8 - keep + j + tm, :] * cw[j:j + 1]
        f.append(_mm(_silu(conv) * u[bi][:, d_ff:], wdn_ref[...]))
    for bi in streams:
        x2 = x1[bi] + _rms(f[bi], nfpost_ref[...])
        xo_ref[bi] = _rms(x2, fin_ref[...]) if last else x2
        a_scr[bi, 0:8, :] = a_scr[bi, tm:tm + 8, :]

    @pl.when(t == nt - 1)
    def _():
        for bi in streams:
            bufo_ref[bi] = a_scr[bi, 8 - keep:8, :]


def _out_ffn(x, ol, ob, buf, w, final_norm, *, n_rows, pad_front, last):
    b, t, d = x.shape
    d_ff = buf.shape[-1]
    tm = _row_tile(n_rows, FFN_ROWS)
    assert tm >= 8 and n_rows <= t
    bb = FFN_STREAMS if b % FFN_STREAMS == 0 else 1
    row = lambda width: pl.BlockSpec((bb, tm, width), lambda i, j: (i, j, 0))
    per_b = pl.BlockSpec((bb, CONV_W - 1, d_ff), lambda i, j: (i, 0, 0))
    weights = (w['w_uv'], w['w_out'], w['norm_mix_post'], w['norm_ffn_pre'], w['w_ffn_up'], w['ffn_conv_w'],
               w['ffn_conv_b'], w['w_ffn_down'], w['norm_ffn_post'], final_norm)
    return pl.pallas_call(
        functools.partial(_out_ffn_body, bb=bb, tm=tm, pad_front=pad_front, last=last),
        grid=(b // bb, n_rows // tm),
        in_specs=[row(d), row(MLA_HEADS * KV_LORA), row(HG_WIDTH), per_b] + [_const_spec(a.shape) for a in weights],
        out_specs=(row(d), per_b),
        out_shape=(jax.ShapeDtypeStruct((b, t, d), F32), jax.ShapeDtypeStruct((b, CONV_W - 1, d_ff), F32)),
        scratch_shapes=[pltpu.VMEM((bb, tm + 8, d_ff), F32)],
        compiler_params=pltpu.CompilerParams(dimension_semantics=("parallel", "arbitrary"),
                                             vmem_limit_bytes=VMEM_LIMIT),
        name="out_ffn",
    )(x, ol, ob, buf, *weights)


def _rope_tables(pos):
    half = QK_ROPE // 2
    inv = ROPE_THETA ** (-jnp.arange(half, dtype=F32) / half)
    ang = pos.astype(F32)[:, None] * inv
    cos, sin = jnp.cos(ang), jnp.sin(ang)
    zero = jnp.zeros_like(cos)
    pad = jnp.zeros((pos.shape[0], LANES - QK_ROPE), F32)
    return (jnp.concatenate([cos, cos, pad], axis=1),
            jnp.concatenate([-sin, zero, pad], axis=1),
            jnp.concatenate([zero, sin, pad], axis=1),
            jnp.tile(cos, (1, MLA_HEADS)), jnp.tile(sin, (1, MLA_HEADS)))


def _layer_weights(l, w_in, norm_mix_pre, norm_mix_post, q_norm, kv_norm, w_uq, w_uk, w_uv, hg_out_norm, w_out,
                   norm_ffn_pre, norm_ffn_post, w_ffn_up, ffn_conv_w, ffn_conv_b, w_ffn_down):
    d = w_in.shape[1]
    head_w = Q_LORA + KV_LORA + QK_ROPE
    w_in_r = jnp.concatenate([w_in[l][:, :head_w], jnp.zeros((d, LANES - QK_ROPE), F32), w_in[l][:, head_w:]], axis=1)
    uq = w_uq[l].reshape(Q_LORA, MLA_HEADS, QK_NOPE + QK_ROPE)
    half = QK_ROPE // 2
    w_uq_r = jnp.concatenate([uq[:, :, :QK_NOPE].reshape(Q_LORA, -1),
                              uq[:, :, QK_NOPE:QK_NOPE + half].reshape(Q_LORA, -1),
                              uq[:, :, QK_NOPE + half:].reshape(Q_LORA, -1)], axis=1)
    src = jnp.arange(2 * MLA_HEADS * half)
    dst = (src % (MLA_HEADS * half)) // half * LANES + src // (MLA_HEADS * half) * half + src % half
    pe_spread = jnp.zeros((2 * MLA_HEADS * half, MLA_HEADS * LANES), F32).at[src, dst].set(1.0)
    uk = jnp.transpose(w_uk[l], (1, 2, 0))
    uk_pair = jnp.zeros((MLA_HEADS // 2, 2 * QK_NOPE, 2 * KV_LORA), F32)
    uk_pair = uk_pair.at[:, :QK_NOPE, :KV_LORA].set(uk[0::2]).at[:, QK_NOPE:, KV_LORA:].set(uk[1::2])
    uv = jnp.transpose(w_uv[l], (1, 0, 2))
    uv_pair = jnp.zeros((MLA_HEADS // 2, 2 * KV_LORA, 2 * V_HEAD), F32)
    uv_pair = uv_pair.at[:, :KV_LORA, :V_HEAD].set(uv[0::2]).at[:, KV_LORA:, V_HEAD:].set(uv[1::2])
    row = lambda a: a.reshape(1, -1)
    return {
        'w_in': w_in_r.astype(BF16), 'norm_mix_pre': row(norm_mix_pre[l]), 'norm_mix_post': row(norm_mix_post[l]),
        'q_norm': row(q_norm[l]), 'kv_norm': row(kv_norm[l]), 'w_uq': w_uq_r.astype(BF16),
        'w_uk': uk_pair.astype(BF16), 'w_uv': uv_pair.astype(BF16), 'pe_spread': pe_spread.astype(BF16),
        'hg_gain': row(hg_out_norm[l]), 'w_out': w_out[l].astype(BF16),
        'norm_ffn_pre': row(norm_ffn_pre[l]), 'norm_ffn_post': row(norm_ffn_post[l]),
        'w_ffn_up': w_ffn_up[l].astype(BF16), 'ffn_conv_w': ffn_conv_w[l], 'ffn_conv_b': row(ffn_conv_b[l]),
        'w_ffn_down': w_ffn_down[l].astype(BF16),
    }


def kernel(x_prompt, x_sample, cache_kv_latent, cache_k_rope, state_hgrn, state_ffn_conv, meta_tokens, w_in, norm_mix_pre, norm_mix_post, q_norm, kv_norm, w_uq, w_uk, w_uv, hg_lower_bounds, hg_out_norm, w_out, norm_ffn_pre, norm_ffn_post, w_ffn_up, ffn_conv_w, ffn_conv_b, w_ffn_down, final_norm):
    depth = w_in.shape[0]
    b, seq, d = x_prompt.shape
    bs, ts, _ = x_sample.shape
    past = cache_kv_latent.shape[2]
    d_ff = ffn_conv_w.shape[-1]
    assert seq % CHUNK == 0

    lb_sm = jax.nn.softmax(hg_lower_bounds.astype(F32), axis=0)
    lb_all = jnp.cumsum(lb_sm, axis=0) - lb_sm[0]

    unit = IN_PROJ_ROWS * KEY_BLOCK // 128
    lm = PAD_FRONT + N_META
    lm_alloc = -(-lm // unit) * unit
    xm = jnp.concatenate([jnp.zeros((1, PAD_FRONT, d), F32), meta_tokens[None],
                          jnp.zeros((1, lm_alloc - lm, d), F32)], axis=1)
    tabs_m = _rope_tables(jnp.arange(lm_alloc, dtype=jnp.int32) - lm)
    seq_alloc = -(-seq // KEY_BLOCK) * KEY_BLOCK
    xp = x_prompt if seq_alloc == seq else jnp.pad(x_prompt, ((0, 0), (0, seq_alloc - seq), (0, 0)))
    tabs_p = _rope_tables(jnp.arange(seq_alloc, dtype=jnp.int32))
    xs = x_sample
    tabs_s = tuple(jnp.tile(a, (bs, 1)) for a in _rope_tables(past + jnp.arange(ts, dtype=jnp.int32)))
    fin = final_norm.reshape(1, -1)
    s0_m = jnp.zeros((1, HG_HEADS, HG_DK, HG_DV), F32)
    buf0_m = jnp.zeros((1, CONV_W - 1, d_ff), F32)
    meta_rows = slice(PAD_FRONT, lm)

    outs = [[] for _ in range(8)]
    for l in range(depth):
        w = _layer_weights(l, w_in, norm_mix_pre, norm_mix_post, q_norm, kv_norm, w_uq, w_uk, w_uv, hg_out_norm,
                           w_out, norm_ffn_pre, norm_ffn_post, w_ffn_up, ffn_conv_w, ffn_conv_b, w_ffn_down)
        w['hg_lb'] = lb_all[l].reshape(1, -1)
        gain = w['hg_gain']
        last = l == depth - 1

        qcat, kcat, kvt, kv_m, pe_m, zh = _in_proj(xm, tabs_m, w, n_valid=lm, pad_front=PAD_FRONT)
        ol = _mla_prompt(qcat, kcat, kvt, None, lp=lm, dummy=PAD_FRONT)
        ob, s_m = _hgrn(zh, s0_m, gain, n_rows=lm, cs=CHUNK)
        xm, buf_m = _out_ffn(xm, ol, ob, buf0_m, w, fin, n_rows=lm, pad_front=PAD_FRONT, last=last)
        prefix = (jnp.pad(kcat[0, meta_rows], ((0, LANES - N_META), (0, 0))),
                  jnp.pad(kvt[0, :, meta_rows], ((0, 0), (0, LANES - N_META))))

        qcat, kcat, kvt, kv, pe, zh = _in_proj(xp, tabs_p, w, n_valid=seq, pad_front=0)
        ol = _mla_prompt(qcat, kcat, kvt, prefix, lp=seq, dummy=0)
        ob, s_new = _hgrn(zh, jnp.broadcast_to(s_m, (b,) + s_m.shape[1:]), gain, n_rows=seq, cs=CHUNK)
        xp, buf = _out_ffn(xp, ol, ob, jnp.broadcast_to(buf_m, (b,) + buf_m.shape[1:]), w, fin, n_rows=seq,
                           pad_front=0, last=last)
        with_meta = lambda m, p: jnp.concatenate(
            [jnp.broadcast_to(m[:, meta_rows], (b, N_META, m.shape[-1])), p[:, :seq]], axis=1)
        for lst, a in zip(outs[:4], (with_meta(kv_m, kv), with_meta(pe_m, pe), s_new, buf)):
            lst.append(a)

        qcat, kcat, _, kv, pe, zh = _in_proj(xs.reshape(1, bs * ts, d), tabs_s, w, n_valid=bs * ts, pad_front=0)
        qcat = jnp.swapaxes(qcat.reshape(MLA_HEADS, bs, ts, QCAT), 0, 1)
        kcat, kv, pe, zh = (a.reshape(bs, ts, -1) for a in (kcat, kv, pe, zh))
        ol = _mla_sample(qcat, kcat, cache_kv_latent[l], cache_k_rope[l])
        ob, s_new = _hgrn(zh, state_hgrn[l], gain, n_rows=ts, cs=ts)
        xs, buf = _out_ffn(xs, ol, ob, state_ffn_conv[l], w, fin, n_rows=ts, pad_front=0, last=last)
        for lst, a in zip(outs[4:], (kv, pe, s_new, buf)):
            lst.append(a)

    y_prompt = xp if seq_alloc == seq else xp[:, :seq]
    return (y_prompt, xs) + tuple(jnp.stack(o) for o in outs)
```

```python
import functools

import jax
import jax.numpy as jnp
from jax import lax
from jax.experimental import pallas as pl
from jax.experimental.pallas import tpu as pltpu

F32 = jnp.float32
BF16 = jnp.bfloat16

CHUNK = 64
N_META = 16
PAD_FRONT = CHUNK - N_META
EPS = 1e-6
MLA_HEADS = 8
QK_NOPE = 64
QK_ROPE = 32
V_HEAD = 64
Q_LORA = 256
KV_LORA = 128
ROPE_THETA = 10000.0
HG_HEADS = 4
HG_DK = 128
HG_DV = 128
CONV_W = 3

HG_KW = HG_HEADS * HG_DK
HG_WIDTH = HG_HEADS * HG_DV
MLA_WIDTH = MLA_HEADS * V_HEAD
LANES = 128
KEY_BLOCK = 256
QCAT = KV_LORA + LANES
VT_ROWS = KV_LORA + 16
SOFTMAX_FAST_RANGE = 100.0
PREFIX_KEYS = N_META
IN_PROJ_ROWS = 256
IN_PROJ_STREAMS = 2
Q_SCALE = (QK_NOPE + QK_ROPE) ** -0.5 * 1.4426950408889634
HGRN_FAST_RANGE = 80.0
FFN_ROWS = 256
FFN_STREAMS = 2
HGRN_STREAMS = 8
VMEM_LIMIT = 56 * 1024 * 1024


def _row_tile(n, target):
    if n <= target:
        return n
    best = None
    for t in range(16, target + 1, 16):
        if n % t == 0:
            best = t
    assert best is not None, (n, target)
    return best


def _const_spec(shape):
    nd = len(shape)
    return pl.BlockSpec(shape, lambda *_: (0,) * nd, pipeline_mode=pl.Buffered(1))


def _rms(x, w):
    return x * lax.rsqrt(jnp.mean(x * x, axis=-1, keepdims=True) + EPS) * w


def _silu(x):
    return x * jax.nn.sigmoid(x)


def _mm(a, b):
    return jnp.dot(a.astype(BF16), b.astype(BF16), preferred_element_type=F32)


def _mm_nt(a, b):
    return lax.dot_general(a.astype(BF16), b.astype(BF16), (((1,), (1,)), ((), ())),
                           preferred_element_type=F32)


def _mm_tn(a, b):
    return lax.dot_general(a.astype(BF16), b.astype(BF16), (((0,), (0,)), ((), ())),
                           preferred_element_type=F32)


def _in_proj_body(x_ref, cos_ref, sn_ref, sp_ref, cosq_ref, sinq_ref, npre_ref, win_ref, qn_ref, kvn_ref, wuq_ref,
                  wuk_ref, spread_ref, lb_ref, qcat_ref, kcat_ref, kvt_ref, kv_ref, pe_ref, zh_ref,
                  *, bb, tm, n_valid, pad_front):
    cos, sn, sp = cos_ref[...], sn_ref[...], sp_ref[...]
    lb = lb_ref[...]

    def rope(v):
        return v * cos + pltpu.roll(v, LANES - QK_ROPE // 2, 1) * sn + pltpu.roll(v, QK_ROPE // 2, 1) * sp

    row = pl.program_id(1) * tm + lax.broadcasted_iota(jnp.int32, (tm, 1), 0)
    live = row < n_valid
    real = row >= pad_front
    h = [_rms(jnp.where(live, x_ref[bi], 0.0), npre_ref[...]) for bi in range(bb)]
    zs = [_mm(h[bi], win_ref[...]) for bi in range(bb)]
    nope_w = MLA_HEADS * QK_NOPE
    for bi, z in enumerate(zs):
        off = Q_LORA + KV_LORA + LANES
        hf = z[:, off + HG_KW:off + 2 * HG_KW]
        e = jnp.exp(-jnp.abs(hf))
        r = 1.0 / (1.0 + e)
        f = lb + (1.0 - lb) * jnp.where(hf >= 0.0, r, e * r)
        zh_ref[bi, :, :HG_KW] = jnp.where(real, _silu(z[:, off:off + HG_KW]), 0.0)
        zh_ref[bi, :, HG_KW:2 * HG_KW] = jnp.where(real, jnp.log(f), 0.0)
        zh_ref[bi, :, 2 * HG_KW:3 * HG_KW] = jnp.where(real, (1.0 - lb) * jnp.where(hf >= 0.0, e * r, r), 0.0)
        zh_ref[bi, :, 3 * HG_KW:] = z[:, off + 2 * HG_KW:]
        kv_lat = _rms(z[:, Q_LORA:Q_LORA + KV_LORA], kvn_ref[...])
        k_pe = rope(z[:, Q_LORA + KV_LORA:Q_LORA + KV_LORA + LANES])
        kv_ref[bi] = kv_lat
        pe_ref[bi] = k_pe[:, :QK_ROPE]
        kcat_ref[bi, :, :KV_LORA] = kv_lat.astype(BF16)
        kcat_ref[bi, :, KV_LORA:] = k_pe.astype(BF16)
        kvt_ref[bi, :KV_LORA, :] = kv_lat.T.astype(BF16)
        kvt_ref[bi, KV_LORA:, :] = jnp.ones((VT_ROWS - KV_LORA, tm), BF16)

        cqn = _rms(z[:, :Q_LORA], qn_ref[...])
        q = _mm(cqn, wuq_ref[...]) * Q_SCALE
        x1, x2 = q[:, nope_w:nope_w + LANES], q[:, nope_w + LANES:]
        cq, sq = cosq_ref[...], sinq_ref[...]
        roped = jnp.concatenate([x1 * cq - x2 * sq, x1 * sq + x2 * cq], axis=1)
        pe_heads = _mm(roped, spread_ref[...])
        for j in range(MLA_HEADS // 2):
            ql = _mm(q[:, 2 * QK_NOPE * j:2 * QK_NOPE * (j + 1)], wuk_ref[j])
            for hh in range(2):
                hd = 2 * j + hh
                qcat_ref[bi, hd, :, :KV_LORA] = ql[:, KV_LORA * hh:KV_LORA * (hh + 1)].astype(BF16)
                qcat_ref[bi, hd, :, KV_LORA:] = pe_heads[:, LANES * hd:LANES * (hd + 1)].astype(BF16)


def _in_proj(x, tabs, w, *, n_valid, pad_front):
    bx, t, d = x.shape
    tm = _row_tile(t, IN_PROJ_ROWS)
    assert tm % LANES == 0 or tm == t
    nt = t // tm
    bb = IN_PROJ_STREAMS if bx % IN_PROJ_STREAMS == 0 else 1
    row = lambda shape: pl.BlockSpec((bb, tm) + shape, lambda b, i: (b, i) + (0,) * len(shape))
    tab = pl.BlockSpec((tm, LANES), lambda b, i: (i, 0))
    zw = 3 * HG_KW + 2 * HG_WIDTH
    out_shape = (
        jax.ShapeDtypeStruct((bx, MLA_HEADS, t, QCAT), BF16),
        jax.ShapeDtypeStruct((bx, t, QCAT), BF16),
        jax.ShapeDtypeStruct((bx, VT_ROWS, t), BF16),
        jax.ShapeDtypeStruct((bx, t, KV_LORA), F32),
        jax.ShapeDtypeStruct((bx, t, QK_ROPE), F32),
        jax.ShapeDtypeStruct((bx, t, zw), F32),
    )
    out_specs = (
        pl.BlockSpec((bb, MLA_HEADS, tm, QCAT), lambda b, i: (b, 0, i, 0)),
        row((QCAT,)),
        pl.BlockSpec((bb, VT_ROWS, tm), lambda b, i: (b, 0, i)),
        row((KV_LORA,)), row((QK_ROPE,)), row((zw,)),
    )
    weights = (w['norm_mix_pre'], w['w_in'], w['q_norm'], w['kv_norm'], w['w_uq'], w['w_uk'], w['pe_spread'],
               w['hg_lb'])
    return pl.pallas_call(
        functools.partial(_in_proj_body, bb=bb, tm=tm, n_valid=n_valid, pad_front=pad_front),
        grid=(bx // bb, nt),
        in_specs=[row((d,))] + [tab] * len(tabs) + [_const_spec(a.shape) for a in weights],
        out_specs=out_specs,
        out_shape=out_shape,
        compiler_params=pltpu.CompilerParams(dimension_semantics=("parallel", "parallel"),
                                             vmem_limit_bytes=VMEM_LIMIT),
        name="in_proj",
    )(x, *tabs, *weights)


def _mla_prompt_body(*refs, qt, j0, aliased, dummy, prefix):
    q_ref, k_ref, vt_ref = refs[:3]
    kpre_ref, vtpre_ref = refs[3:5] if prefix else (None, None)
    o_ref, m_scr, top_scr, acc_scr = refs[-4:]
    j = pl.program_id(1) + j0
    width = MLA_HEADS * qt
    pre = PREFIX_KEYS if prefix else 0
    lower_half = lax.broadcasted_iota(jnp.int32, (CHUNK, LANES), 1) < CHUNK

    def hidden(rows, cols=LANES):
        return jnp.full((rows, cols), -jnp.inf, F32)

    def mask_dummy(s):
        return jnp.concatenate([hidden(dummy, s.shape[1]), s[dummy:]], axis=0) if dummy else s

    def mask_diagonal(s):
        strips = []
        for c0 in range(0, width, LANES):
            rows = [s[:pre, c0:c0 + LANES]] if prefix else []
            t = s[pre:, c0:c0 + LANES]
            if (c0 % qt) // LANES == 0:
                rows += [t[:CHUNK], jnp.where(lower_half, -jnp.inf, t[CHUNK:2 * CHUNK]), hidden(2 * CHUNK)]
            else:
                rows += [t[:3 * CHUNK], jnp.where(lower_half, -jnp.inf, t[3 * CHUNK:])]
            strips.append(jnp.concatenate(rows, axis=0))
        return jnp.concatenate(strips, axis=1)

    def keys_of(kb):
        start = kb * KEY_BLOCK
        return pl.ds(start if isinstance(kb, int) else pl.multiple_of(start, KEY_BLOCK), KEY_BLOCK)

    def scores(kb, diagonal=False):
        keys = k_ref[0, keys_of(kb), :]
        if diagonal and prefix:
            keys = jnp.concatenate([kpre_ref[...], keys], axis=0)
        return _mm_nt(keys, q_ref[0].reshape(width, QCAT))

    def weighted_values(p, kb, diagonal):
        if diagonal and prefix:
            return _mm(vtpre_ref[...], p[:pre]) + _mm(vt_ref[0, :, keys_of(kb)], p[pre:])
        return _mm(vt_ref[0, :, keys_of(kb)], p)

    def absorb(s, kb, diagonal=False):
        if diagonal:
            s = mask_diagonal(s)
        if isinstance(kb, int) and kb == 0:
            s = mask_dummy(s)
        for hd in range(MLA_HEADS):
            cols = slice(qt * hd, qt * (hd + 1))
            m_new = jnp.max(s[:, cols], axis=0, keepdims=True)
            if not diagonal:
                m_old = m_scr[:, cols]
                m_new = jnp.maximum(m_old, m_new)
            pv = weighted_values(jnp.exp2(s[:, cols] - m_new), kb, diagonal)
            acc_scr[:, cols] = pv if diagonal else jnp.exp2(m_old - m_new) * acc_scr[:, cols] + pv
            m_scr[:, cols] = m_new

    def absorb_all(blocks):
        for s, kb, diagonal in blocks:
            absorb(s, kb, diagonal)

    def absorb_unscaled(blocks):
        blocks = [(mask_diagonal(s) if diagonal else s, kb, diagonal) for s, kb, diagonal in blocks]
        blocks = [(mask_dummy(s) if isinstance(kb, int) and kb == 0 else s, kb, d) for s, kb, d in blocks]
        first = blocks[0][2]
        for hd in range(MLA_HEADS):
            cols = slice(qt * hd, qt * (hd + 1))
            tops = [jnp.max(s[:, cols], axis=0, keepdims=True) for s, _, _ in blocks]
            if first:
                shift = jnp.max(blocks[0][0][:PREFIX_KEYS, cols], axis=0, keepdims=True) if prefix else tops[0]
                m_scr[:, cols] = shift
                top_scr[:, cols] = functools.reduce(jnp.maximum, tops)
            else:
                shift = m_scr[:, cols]
                top_scr[:, cols] = functools.reduce(jnp.maximum, tops, top_scr[:, cols])
            pv = None
            for s, kb, diagonal in blocks:
                term = weighted_values(jnp.exp2(s[:, cols] - shift), kb, diagonal)
                pv = term if pv is None else pv + term
            acc_scr[:, cols] = pv if first else acc_scr[:, cols] + pv

    def visit_blocks(absorb_blocks):
        def pair(a, b, diagonal=False):
            s_a, s_b = scores(a, diagonal), scores(b)
            absorb_blocks([(s_a, a, diagonal), (s_b, b, False)])

        def absorb_block(s, kb, diagonal=False):
            absorb_blocks([(s, kb, diagonal)])

        @pl.when(j == 0)
        def _():
            absorb_block(scores(0, True), 0, True)

        n_blocks = k_ref.shape[1] // KEY_BLOCK
        if n_blocks < 2:
            return

        @pl.when(j == 1)
        def _():
            pair(1, 0, True)

        if n_blocks < 3:
            return

        @pl.when(j > 1)
        def _():
            pair(j, j - 1, True)
            n_inner = (j - 2) // 2

            def inner(p, carry):
                pair(j - 2 * p, j - 2 * p - 1)
                return carry

            lax.fori_loop(1, n_inner + 1, inner, 0)

            @pl.when(j % 2 == 0)
            def _():
                absorb_block(scores(0), 0)

            @pl.when(j % 2 == 1)
            def _():
                pair(1, 0)

    def finish():
        for hd in range(MLA_HEADS):
            acc = acc_scr[:, qt * hd:qt * (hd + 1)]
            o_t = acc[:KV_LORA] * (1.0 / acc[KV_LORA:KV_LORA + 1])
            o_ref[0, :, KV_LORA * hd:KV_LORA * (hd + 1)] = o_t.T.astype(BF16)

    visit_blocks(absorb_unscaled)
    runaway = jnp.max(top_scr[...] - m_scr[...]) > SOFTMAX_FAST_RANGE
    finish()

    @pl.when(runaway)
    def _():
        visit_blocks(absorb_all)
        finish()


def _mla_prompt_call(qcat, kcat, kvt, prefix, prev, *, qt, j0, steps, dummy):
    b, _, t, _ = qcat.shape
    per_block = KEY_BLOCK // qt
    in_specs = [pl.BlockSpec((1, MLA_HEADS, qt, QCAT), lambda i, j: (i, 0, (j + j0) * per_block, 0)),
                pl.BlockSpec((1, t, QCAT), lambda i, j: (i, 0, 0)),
                pl.BlockSpec((1, VT_ROWS, t), lambda i, j: (i, 0, 0))]
    args = (qcat, kcat, kvt)
    if prefix is not None:
        in_specs += [_const_spec(a.shape) for a in prefix]
        args += tuple(prefix)
    if prev is not None:
        in_specs.append(pl.BlockSpec(memory_space=pl.ANY))
        args += (prev,)
    return pl.pallas_call(
        functools.partial(_mla_prompt_body, qt=qt, j0=j0, aliased=prev is not None, dummy=dummy,
                          prefix=prefix is not None),
        grid=(b, steps),
        in_specs=in_specs,
        out_specs=pl.BlockSpec((1, qt, MLA_HEADS * KV_LORA), lambda i, j: (i, (j + j0) * per_block, 0)),
        out_shape=jax.ShapeDtypeStruct((b, t, MLA_HEADS * KV_LORA), BF16),
        input_output_aliases={len(args) - 1: 0} if prev is not None else {},
        scratch_shapes=[pltpu.VMEM((1, MLA_HEADS * qt), F32), pltpu.VMEM((1, MLA_HEADS * qt), F32),
                        pltpu.VMEM((VT_ROWS, MLA_HEADS * qt), F32)],
        compiler_params=pltpu.CompilerParams(dimension_semantics=("parallel", "parallel"),
                                             vmem_limit_bytes=VMEM_LIMIT),
        name="mla_prompt",
    )(*args)


def _mla_prompt(qcat, kcat, kvt, prefix, *, lp, dummy):
    t = qcat.shape[2]
    assert t % KEY_BLOCK == 0 and lp <= t
    full, rest = divmod(lp, KEY_BLOCK)
    if rest > LANES:
        full, rest = full + 1, 0
    out = None
    if full:
        out = _mla_prompt_call(qcat, kcat, kvt, prefix, None, qt=KEY_BLOCK, j0=0, steps=full, dummy=dummy)
    if rest:
        out = _mla_prompt_call(qcat, kcat, kvt, prefix, out, qt=LANES, j0=full, steps=1, dummy=dummy)
    return out


def _mla_sample_body(q_ref, k_ref, ckv_ref, cpe_ref, o_ref, *, ts):
    rows = MLA_HEADS * ts
    q = q_ref[0].reshape(rows, QCAT)
    ckv = ckv_ref[0].astype(BF16)
    knew = k_ref[0]
    s_c = _mm_nt(q[:, :KV_LORA], ckv) + _mm_nt(q[:, KV_LORA:KV_LORA + QK_ROPE], cpe_ref[0])
    s_n = _mm_nt(q, knew)
    m = jnp.maximum(jnp.max(s_c, axis=-1, keepdims=True), jnp.max(s_n, axis=-1, keepdims=True))
    p_c = jnp.exp2(s_c - m)
    p_n = jnp.exp2(s_n - m)
    l = jnp.sum(p_c, axis=-1, keepdims=True) + jnp.sum(p_n, axis=-1, keepdims=True)
    o = (_mm(p_c, ckv) + _mm(p_n, knew[:, :KV_LORA])) / l
    for hd in range(MLA_HEADS):
        o_ref[0, :, KV_LORA * hd:KV_LORA * (hd + 1)] = o[ts * hd:ts * (hd + 1)].astype(BF16)


def _mla_sample(qcat, kcat, cache_kv, cache_pe):
    b, _, ts, _ = qcat.shape
    past = cache_kv.shape[1]
    return pl.pallas_call(
        functools.partial(_mla_sample_body, ts=ts),
        grid=(b,),
        in_specs=[pl.BlockSpec((1, MLA_HEADS, ts, QCAT), lambda i: (i, 0, 0, 0)),
                  pl.BlockSpec((1, ts, QCAT), lambda i: (i, 0, 0)),
                  pl.BlockSpec((1, past, KV_LORA), lambda i: (i, 0, 0)),
                  pl.BlockSpec((1, past, QK_ROPE), lambda i: (i, 0, 0))],
        out_specs=pl.BlockSpec((1, ts, MLA_HEADS * KV_LORA), lambda i: (i, 0, 0)),
        out_shape=jax.ShapeDtypeStruct((b, ts, MLA_HEADS * KV_LORA), BF16),
        compiler_params=pltpu.CompilerParams(dimension_semantics=("parallel",),
                                             vmem_limit_bytes=VMEM_LIMIT),
        name="mla_sample",
    )(qcat, kcat, cache_kv, cache_pe)


def _hgrn_body(zh_ref, s0_ref, gain_ref, ob_ref, sout_ref, st_ref, g_scr, q_scr, k_scr, v_scr, o_scr, oi_scr,
               *, bb, cs):
    c = pl.program_id(1)
    nc = pl.num_programs(1)
    streams = range(bb)
    heads = [slice(HG_DK * hd, HG_DK * (hd + 1)) for hd in range(HG_HEADS)]

    @pl.when(c == 0)
    def _():
        for bi in streams:
            for hd in range(HG_HEADS):
                st_ref[bi, hd] = s0_ref[bi, hd].T

    q = zh_ref[:, :, :HG_KW]
    gl = zh_ref[:, :, HG_KW:2 * HG_KW]
    k = zh_ref[:, :, 2 * HG_KW:3 * HG_KW]
    v = zh_ref[:, :, 3 * HG_KW:3 * HG_KW + HG_WIDTH]
    t_idx = lax.broadcasted_iota(jnp.int32, (cs, cs), 0)
    s_idx = lax.broadcasted_iota(jnp.int32, (cs, cs), 1)
    causal = t_idx >= s_idx
    tri = causal.astype(F32)
    g = jnp.stack([jnp.dot(tri, gl[bi], precision=lax.Precision.HIGHEST, preferred_element_type=F32)
                   for bi in streams])
    g_tot = g[:, cs - 1:cs, :]
    any_out_of_range = jnp.max(-g_tot) > HGRN_FAST_RANGE
    qt = q * jnp.exp(g)
    khat = k * jnp.exp(g_tot - g)
    kt = k * jnp.exp(-g)
    a = [[jnp.where(causal, _mm_nt(qt[bi][:, sl], kt[bi][:, sl]), 0.0) for sl in heads] for bi in streams]
    for bi in streams:
        o_scr[bi] = jnp.concatenate([_mm(a[bi][hd], v[bi][:, sl]) for hd, sl in enumerate(heads)], axis=-1)

    o_inter = [[_mm_nt(qt[bi][:, sl], st_ref[bi, hd]) for hd, sl in enumerate(heads)] for bi in streams]
    kv_new = [[_mm_tn(v[bi][:, sl], khat[bi][:, sl]) for sl in heads] for bi in streams]
    decay = jnp.exp(g_tot)
    for bi in streams:
        oi_scr[bi] = jnp.concatenate(o_inter[bi], axis=-1)
        for hd, sl in enumerate(heads):
            st_ref[bi, hd] = decay[bi][:, sl] * st_ref[bi, hd] + kv_new[bi][hd]

    @pl.when(any_out_of_range)
    def _():
        for bi in streams:
            @pl.when(jnp.max(-g_tot[bi]) > HGRN_FAST_RANGE)
            def _(bi=bi):
                g_scr[...] = g[bi]
                q_scr[...] = q[bi]
                k_scr[...] = k[bi]
                v_scr[...] = v[bi]
                o_scr[bi] = jnp.zeros((cs, HG_WIDTH), F32)
                t_col = lax.broadcasted_iota(jnp.int32, (cs, 1), 0)

                def cols(i, carry):
                    base = pl.multiple_of(i * 8, 8)
                    g8, k8, v8 = (ref[pl.ds(base, 8), :] for ref in (g_scr, k_scr, v_scr))
                    for j in range(8):
                        dec = jnp.exp(jnp.where(t_col >= base + j, g_scr[...] - g8[j:j + 1], -jnp.inf))
                        pr = q_scr[...] * dec * k8[j:j + 1]
                        for sl in heads:
                            o_scr[bi, :, sl] += jnp.sum(pr[:, sl], axis=-1, keepdims=True) * v8[j:j + 1, sl]
                    return carry

                lax.fori_loop(0, cs // 8, cols, 0)

    gain = gain_ref[...]
    for bi in streams:
        o = o_scr[bi] + oi_scr[bi]
        o = [_rms(o[:, sl], gain) for sl in heads]
        ob_ref[bi] = (jnp.concatenate(o, axis=-1) * _silu(zh_ref[bi, :, 3 * HG_KW + HG_WIDTH:])).astype(BF16)

    @pl.when(c == nc - 1)
    def _():
        for bi in streams:
            for hd in range(HG_HEADS):
                sout_ref[bi, hd] = st_ref[bi, hd].T


def _hgrn(zh, s0, gain, *, n_rows, cs):
    b, t, zw = zh.shape
    assert n_rows % cs == 0 and n_rows <= t
    bb = HGRN_STREAMS if b % HGRN_STREAMS == 0 else 1
    return pl.pallas_call(
        functools.partial(_hgrn_body, bb=bb, cs=cs),
        grid=(b // bb, n_rows // cs),
        in_specs=[pl.BlockSpec((bb, cs, zw), lambda i, c: (i, c, 0)),
                  pl.BlockSpec((bb, HG_HEADS, HG_DK, HG_DV), lambda i, c: (i, 0, 0, 0)),
                  _const_spec(gain.shape)],
        out_specs=(pl.BlockSpec((bb, cs, HG_WIDTH), lambda i, c: (i, c, 0)),
                   pl.BlockSpec((bb, HG_HEADS, HG_DK, HG_DV), lambda i, c: (i, 0, 0, 0))),
        out_shape=(jax.ShapeDtypeStruct((b, t, HG_WIDTH), BF16),
                   jax.ShapeDtypeStruct((b, HG_HEADS, HG_DK, HG_DV), F32)),
        scratch_shapes=[pltpu.VMEM((bb, HG_HEADS, HG_DV, HG_DK), F32)]
        + [pltpu.VMEM((cs, HG_KW), F32) for _ in range(4)] + [pltpu.VMEM((bb, cs, HG_KW), F32) for _ in range(2)],
        compiler_params=pltpu.CompilerParams(dimension_semantics=("parallel", "arbitrary"),
                                             vmem_limit_bytes=VMEM_LIMIT),
        name="hgrn",
    )(zh, s0, gain)


def _out_ffn_body(x_ref, ol_ref, ob_ref, buf_ref, wuv_ref, wout_ref, npost_ref, nfpre_ref, wup_ref, cw_ref,
                  cb_ref, wdn_ref, nfpost_ref, fin_ref, xo_ref, bufo_ref, a_scr, *, bb, tm, pad_front, last):
    t = pl.program_id(1)
    nt = pl.num_programs(1)
    d_ff = cw_ref.shape[1]
    keep = CONV_W - 1
    streams = range(bb)

    @pl.when(t == 0)
    def _():
        for bi in streams:
            a_scr[bi, 0:8, :] = jnp.zeros((8, d_ff), F32)
            a_scr[bi, 8 - keep:8, :] = buf_ref[bi]

    mixed = []
    for bi in streams:
        ol = ol_ref[bi]
        o_a = [_mm(ol[:, 2 * KV_LORA * j:2 * KV_LORA * (j + 1)], wuv_ref[j]) for j in range(MLA_HEADS // 2)]
        mixed.append(jnp.concatenate([o.astype(BF16) for o in o_a] + [ob_ref[bi]], axis=-1))
    y = [_mm(mixed[bi], wout_ref[...]) for bi in streams]
    x1 = [x_ref[bi] + _rms(y[bi], npost_ref[...]) for bi in streams]
    u = [_mm(_rms(x1[bi], nfpre_ref[...]), wup_ref[...]) for bi in streams]
    cw = cw_ref[...]
    if pad_front:
        live = (t * tm + lax.broadcasted_iota(jnp.int32, (tm, 1), 0)) >= pad_front
    f = []
    for bi in streams:
        a = u[bi][:, :d_ff]
        if pad_front:
            a = jnp.where(live, a, 0.0)
        a_scr[bi, 8:8 + tm, :] = a
        conv = cb_ref[...] + a * cw[keep:keep + 1]
        for j in range(keep):
            conv = conv + a_scr[bi, 8 - keep + j:8 - keep + j + tm, :] * cw[j:j + 1]
        f.append(_mm(_silu(conv) * u[bi][:, d_ff:], wdn_ref[...]))
    for bi in streams:
        x2 = x1[bi] + _rms(f[bi], nfpost_ref[...])
        xo_ref[bi] = _rms(x2, fin_ref[...]) if last else x2
        a_scr[bi, 0:8, :] = a_scr[bi, tm:tm + 8, :]

    @pl.when(t == nt - 1)
    def _():
        for bi in streams:
            bufo_ref[bi] = a_scr[bi, 8 - keep:8, :]


def _out_ffn(x, ol, ob, buf, w, final_norm, *, n_rows, pad_front, last):
    b, t, d = x.shape
    d_ff = buf.shape[-1]
    tm = _row_tile(n_rows, FFN_ROWS)
    assert tm >= 8 and n_rows <= t
    bb = FFN_STREAMS if b % FFN_STREAMS == 0 else 1
    row = lambda width: pl.BlockSpec((bb, tm, width), lambda i, j: (i, j, 0))
    per_b = pl.BlockSpec((bb, CONV_W - 1, d_ff), lambda i, j: (i, 0, 0))
    weights = (w['w_uv'], w['w_out'], w['norm_mix_post'], w['norm_ffn_pre'], w['w_ffn_up'], w['ffn_conv_w'],
               w['ffn_conv_b'], w['w_ffn_down'], w['norm_ffn_post'], final_norm)
    return pl.pallas_call(
        functools.partial(_out_ffn_body, bb=bb, tm=tm, pad_front=pad_front, last=last),
        grid=(b // bb, n_rows // tm),
        in_specs=[row(d), row(MLA_HEADS * KV_LORA), row(HG_WIDTH), per_b] + [_const_spec(a.shape) for a in weights],
        out_specs=(row(d), per_b),
        out_shape=(jax.ShapeDtypeStruct((b, t, d), F32), jax.ShapeDtypeStruct((b, CONV_W - 1, d_ff), F32)),
        scratch_shapes=[pltpu.VMEM((bb, tm + 8, d_ff), F32)],
        compiler_params=pltpu.CompilerParams(dimension_semantics=("parallel", "arbitrary"),
                                             vmem_limit_bytes=VMEM_LIMIT),
        name="out_ffn",
    )(x, ol, ob, buf, *weights)


def _rope_tables(pos):
    half = QK_ROPE // 2
    inv = ROPE_THETA ** (-jnp.arange(half, dtype=F32) / half)
    ang = pos.astype(F32)[:, None] * inv
    cos, sin = jnp.cos(ang), jnp.sin(ang)
    zero = jnp.zeros_like(cos)
    pad = jnp.zeros((pos.shape[0], LANES - QK_ROPE), F32)
    return (jnp.concatenate([cos, cos, pad], axis=1),
            jnp.concatenate([-sin, zero, pad], axis=1),
            jnp.concatenate([zero, sin, pad], axis=1),
            jnp.tile(cos, (1, MLA_HEADS)), jnp.tile(sin, (1, MLA_HEADS)))


def _layer_weights(l, w_in, norm_mix_pre, norm_mix_post, q_norm, kv_norm, w_uq, w_uk, w_uv, hg_out_norm, w_out,
                   norm_ffn_pre, norm_ffn_post, w_ffn_up, ffn_conv_w, ffn_conv_b, w_ffn_down):
    d = w_in.shape[1]
    head_w = Q_LORA + KV_LORA + QK_ROPE
    w_in_r = jnp.concatenate([w_in[l][:, :head_w], jnp.zeros((d, LANES - QK_ROPE), F32), w_in[l][:, head_w:]], axis=1)
    uq = w_uq[l].reshape(Q_LORA, MLA_HEADS, QK_NOPE + QK_ROPE)
    half = QK_ROPE // 2
    w_uq_r = jnp.concatenate([uq[:, :, :QK_NOPE].reshape(Q_LORA, -1),
                              uq[:, :, QK_NOPE:QK_NOPE + half].reshape(Q_LORA, -1),
                              uq[:, :, QK_NOPE + half:].reshape(Q_LORA, -1)], axis=1)
    src = jnp.arange(2 * MLA_HEADS * half)
    dst = (src % (MLA_HEADS * half)) // half * LANES + src // (MLA_HEADS * half) * half + src % half
    pe_spread = jnp.zeros((2 * MLA_HEADS * half, MLA_HEADS * LANES), F32).at[src, dst].set(1.0)
    uk = jnp.transpose(w_uk[l], (1, 2, 0))
    uk_pair = jnp.zeros((MLA_HEADS // 2, 2 * QK_NOPE, 2 * KV_LORA), F32)
    uk_pair = uk_pair.at[:, :QK_NOPE, :KV_LORA].set(uk[0::2]).at[:, QK_NOPE:, KV_LORA:].set(uk[1::2])
    uv = jnp.transpose(w_uv[l], (1, 0, 2))
    uv_pair = jnp.zeros((MLA_HEADS // 2, 2 * KV_LORA, 2 * V_HEAD), F32)
    uv_pair = uv_pair.at[:, :KV_LORA, :V_HEAD].set(uv[0::2]).at[:, KV_LORA:, V_HEAD:].set(uv[1::2])
    row = lambda a: a.reshape(1, -1)
    return {
        'w_in': w_in_r.astype(BF16), 'norm_mix_pre': row(norm_mix_pre[l]), 'norm_mix_post': row(norm_mix_post[l]),
        'q_norm': row(q_norm[l]), 'kv_norm': row(kv_norm[l]), 'w_uq': w_uq_r.astype(BF16),
        'w_uk': uk_pair.astype(BF16), 'w_uv': uv_pair.astype(BF16), 'pe_spread': pe_spread.astype(BF16),
        'hg_gain': row(hg_out_norm[l]), 'w_out': w_out[l].astype(BF16),
        'norm_ffn_pre': row(norm_ffn_pre[l]), 'norm_ffn_post': row(norm_ffn_post[l]),
        'w_ffn_up': w_ffn_up[l].astype(BF16), 'ffn_conv_w': ffn_conv_w[l], 'ffn_conv_b': row(ffn_conv_b[l]),
        'w_ffn_down': w_ffn_down[l].astype(BF16),
    }


def kernel(x_prompt, x_sample, cache_kv_latent, cache_k_rope, state_hgrn, state_ffn_conv, meta_tokens, w_in, norm_mix_pre, norm_mix_post, q_norm, kv_norm, w_uq, w_uk, w_uv, hg_lower_bounds, hg_out_norm, w_out, norm_ffn_pre, norm_ffn_post, w_ffn_up, ffn_conv_w, ffn_conv_b, w_ffn_down, final_norm):
    depth = w_in.shape[0]
    b, seq, d = x_prompt.shape
    bs, ts, _ = x_sample.shape
    past = cache_kv_latent.shape[2]
    d_ff = ffn_conv_w.shape[-1]
    assert seq % CHUNK == 0

    lb_sm = jax.nn.softmax(hg_lower_bounds.astype(F32), axis=0)
    lb_all = jnp.cumsum(lb_sm, axis=0) - lb_sm[0]

    unit = IN_PROJ_ROWS * KEY_BLOCK // 128
    lm = PAD_FRONT + N_META
    lm_alloc = -(-lm // unit) * unit
    xm = jnp.concatenate([jnp.zeros((1, PAD_FRONT, d), F32), meta_tokens[None],
                          jnp.zeros((1, lm_alloc - lm, d), F32)], axis=1)
    tabs_m = _rope_tables(jnp.arange(lm_alloc, dtype=jnp.int32) - lm)
    seq_alloc = -(-seq // KEY_BLOCK) * KEY_BLOCK
    xp = x_prompt if seq_alloc == seq else jnp.pad(x_prompt, ((0, 0), (0, seq_alloc - seq), (0, 0)))
    tabs_p = _rope_tables(jnp.arange(seq_alloc, dtype=jnp.int32))
    xs = x_sample
    tabs_s = tuple(jnp.tile(a, (bs, 1)) for a in _rope_tables(past + jnp.arange(ts, dtype=jnp.int32)))
    fin = final_norm.reshape(1, -1)
    s0_m = jnp.zeros((1, HG_HEADS, HG_DK, HG_DV), F32)
    buf0_m = jnp.zeros((1, CONV_W - 1, d_ff), F32)
    meta_rows = slice(PAD_FRONT, lm)

    outs = [[] for _ in range(8)]
    for l in range(depth):
        w = _layer_weights(l, w_in, norm_mix_pre, norm_mix_post, q_norm, kv_norm, w_uq, w_uk, w_uv, hg_out_norm,
                           w_out, norm_ffn_pre, norm_ffn_post, w_ffn_up, ffn_conv_w, ffn_conv_b, w_ffn_down)
        w['hg_lb'] = lb_all[l].reshape(1, -1)
        gain = w['hg_gain']
        last = l == depth - 1

        qcat, kcat, kvt, kv_m, pe_m, zh = _in_proj(xm, tabs_m, w, n_valid=lm, pad_front=PAD_FRONT)
        ol = _mla_prompt(qcat, kcat, kvt, None, lp=lm, dummy=PAD_FRONT)
        ob, s_m = _hgrn(zh, s0_m, gain, n_rows=lm, cs=CHUNK)
        xm, buf_m = _out_ffn(xm, ol, ob, buf0_m, w, fin, n_rows=lm, pad_front=PAD_FRONT, last=last)
        prefix = (kcat[0, meta_rows], kvt[0, :, meta_rows])

        qcat, kcat, kvt, kv, pe, zh = _in_proj(xp, tabs_p, w, n_valid=seq, pad_front=0)
        ol = _mla_prompt(qcat, kcat, kvt, prefix, lp=seq, dummy=0)
        ob, s_new = _hgrn(zh, jnp.broadcast_to(s_m, (b,) + s_m.shape[1:]), gain, n_rows=seq, cs=CHUNK)
        xp, buf = _out_ffn(xp, ol, ob, jnp.broadcast_to(buf_m, (b,) + buf_m.shape[1:]), w, fin, n_rows=seq,
                           pad_front=0, last=last)
        with_meta = lambda m, p: jnp.concatenate(
            [jnp.broadcast_to(m[:, meta_rows], (b, N_META, m.shape[-1])), p[:, :seq]], axis=1)
        for lst, a in zip(outs[:4], (with_meta(kv_m, kv), with_meta(pe_m, pe), s_new, buf)):
            lst.append(a)

        qcat, kcat, _, kv, pe, zh = _in_proj(xs.reshape(1, bs * ts, d), tabs_s, w, n_valid=bs * ts, pad_front=0)
        qcat = jnp.swapaxes(qcat.reshape(MLA_HEADS, bs, ts, QCAT), 0, 1)
        kcat, kv, pe, zh = (a.reshape(bs, ts, -1) for a in (kcat, kv, pe, zh))
        ol = _mla_sample(qcat, kcat, cache_kv_latent[l], cache_k_rope[l])
        ob, s_new = _hgrn(zh, state_hgrn[l], gain, n_rows=ts, cs=ts)
        xs, buf = _out_ffn(xs, ol, ob, state_ffn_conv[l], w, fin, n_rows=ts, pad_front=0, last=last)
        for lst, a in zip(outs[4:], (kv, pe, s_new, buf)):
            lst.append(a)

    y_prompt = xp if seq_alloc == seq else xp[:, :seq]
    return (y_prompt, xs) + tuple(jnp.stack(o) for o in outs)
```

```python
import functools

import jax
import jax.numpy as jnp
from jax import lax
from jax.experimental import pallas as pl
from jax.experimental.pallas import tpu as pltpu

F32 = jnp.float32
BF16 = jnp.bfloat16

CHUNK = 64
N_META = 16
PAD_FRONT = CHUNK - N_META
EPS = 1e-6
MLA_HEADS = 8
QK_NOPE = 64
QK_ROPE = 32
V_HEAD = 64
Q_LORA = 256
KV_LORA = 128
ROPE_THETA = 10000.0
HG_HEADS = 4
HG_DK = 128
HG_DV = 128
CONV_W = 3

HG_KW = HG_HEADS * HG_DK
HG_WIDTH = HG_HEADS * HG_DV
MLA_WIDTH = MLA_HEADS * V_HEAD
LANES = 128
KEY_BLOCK = 256
QCAT = KV_LORA + LANES
VT_ROWS = KV_LORA + 16
SOFTMAX_FAST_RANGE = 100.0
PREFIX_KEYS = N_META
IN_PROJ_ROWS = 256
IN_PROJ_STREAMS = 2
Q_SCALE = (QK_NOPE + QK_ROPE) ** -0.5 * 1.4426950408889634
HGRN_FAST_RANGE = 80.0
FFN_ROWS = 256
FFN_STREAMS = 2
HGRN_STREAMS = 8
VMEM_LIMIT = 56 * 1024 * 1024


def _row_tile(n, target):
    if n <= target:
        return n
    best = None
    for t in range(16, target + 1, 16):
        if n % t == 0:
            best = t
    assert best is not None, (n, target)
    return best


def _const_spec(shape):
    nd = len(shape)
    return pl.BlockSpec(shape, lambda *_: (0,) * nd, pipeline_mode=pl.Buffered(1))


def _rms(x, w):
    return x * lax.rsqrt(jnp.mean(x * x, axis=-1, keepdims=True) + EPS) * w


def _silu(x):
    return x * jax.nn.sigmoid(x)


def _mm(a, b):
    return jnp.dot(a.astype(BF16), b.astype(BF16), preferred_element_type=F32)


def _mm_nt(a, b):
    return lax.dot_general(a.astype(BF16), b.astype(BF16), (((1,), (1,)), ((), ())),
                           preferred_element_type=F32)


def _mm_tn(a, b):
    return lax.dot_general(a.astype(BF16), b.astype(BF16), (((0,), (0,)), ((), ())),
                           preferred_element_type=F32)


def _in_proj_body(x_ref, cos_ref, sn_ref, sp_ref, cosq_ref, sinq_ref, npre_ref, win_ref, qn_ref, kvn_ref, wuq_ref,
                  wuk_ref, spread_ref, lb_ref, qcat_ref, kcat_ref, kvt_ref, kv_ref, pe_ref, zh_ref,
                  *, bb, tm, n_valid, pad_front):
    cos, sn, sp = cos_ref[...], sn_ref[...], sp_ref[...]
    lb = lb_ref[...]

    def rope(v):
        return v * cos + pltpu.roll(v, LANES - QK_ROPE // 2, 1) * sn + pltpu.roll(v, QK_ROPE // 2, 1) * sp

    row = pl.program_id(1) * tm + lax.broadcasted_iota(jnp.int32, (tm, 1), 0)
    live = row < n_valid
    real = row >= pad_front
    h = [_rms(jnp.where(live, x_ref[bi], 0.0), npre_ref[...]) for bi in range(bb)]
    zs = [_mm(h[bi], win_ref[...]) for bi in range(bb)]
    nope_w = MLA_HEADS * QK_NOPE
    for bi, z in enumerate(zs):
        off = Q_LORA + KV_LORA + LANES
        hf = z[:, off + HG_KW:off + 2 * HG_KW]
        e = jnp.exp(-jnp.abs(hf))
        r = 1.0 / (1.0 + e)
        f = lb + (1.0 - lb) * jnp.where(hf >= 0.0, r, e * r)
        zh_ref[bi, :, :HG_KW] = jnp.where(real, _silu(z[:, off:off + HG_KW]), 0.0)
        zh_ref[bi, :, HG_KW:2 * HG_KW] = jnp.where(real, jnp.log(f), 0.0)
        zh_ref[bi, :, 2 * HG_KW:3 * HG_KW] = jnp.where(real, (1.0 - lb) * jnp.where(hf >= 0.0, e * r, r), 0.0)
        zh_ref[bi, :, 3 * HG_KW:] = z[:, off + 2 * HG_KW:]
        kv_lat = _rms(z[:, Q_LORA:Q_LORA + KV_LORA], kvn_ref[...])
        k_pe = rope(z[:, Q_LORA + KV_LORA:Q_LORA + KV_LORA + LANES])
        kv_ref[bi] = kv_lat
        pe_ref[bi] = k_pe[:, :QK_ROPE]
        kcat_ref[bi, :, :KV_LORA] = kv_lat.astype(BF16)
        kcat_ref[bi, :, KV_LORA:] = k_pe.astype(BF16)
        kvt_ref[bi, :KV_LORA, :] = kv_lat.T.astype(BF16)
        kvt_ref[bi, KV_LORA:, :] = jnp.ones((VT_ROWS - KV_LORA, tm), BF16)

        cqn = _rms(z[:, :Q_LORA], qn_ref[...])
        q = _mm(cqn, wuq_ref[...]) * Q_SCALE
        x1, x2 = q[:, nope_w:nope_w + LANES], q[:, nope_w + LANES:]
        cq, sq = cosq_ref[...], sinq_ref[...]
        roped = jnp.concatenate([x1 * cq - x2 * sq, x1 * sq + x2 * cq], axis=1)
        pe_heads = _mm(roped, spread_ref[...])
        for j in range(MLA_HEADS // 2):
            ql = _mm(q[:, 2 * QK_NOPE * j:2 * QK_NOPE * (j + 1)], wuk_ref[j])
            for hh in range(2):
                hd = 2 * j + hh
                qcat_ref[bi, hd, :, :KV_LORA] = ql[:, KV_LORA * hh:KV_LORA * (hh + 1)].astype(BF16)
                qcat_ref[bi, hd, :, KV_LORA:] = pe_heads[:, LANES * hd:LANES * (hd + 1)].astype(BF16)


def _in_proj(x, tabs, w, *, n_valid, pad_front):
    bx, t, d = x.shape
    tm = _row_tile(t, IN_PROJ_ROWS)
    assert tm % LANES == 0 or tm == t
    nt = t // tm
    bb = IN_PROJ_STREAMS if bx % IN_PROJ_STREAMS == 0 else 1
    row = lambda shape: pl.BlockSpec((bb, tm) + shape, lambda b, i: (b, i) + (0,) * len(shape))
    tab = pl.BlockSpec((tm, LANES), lambda b, i: (i, 0))
    zw = 3 * HG_KW + 2 * HG_WIDTH
    out_shape = (
        jax.ShapeDtypeStruct((bx, MLA_HEADS, t, QCAT), BF16),
        jax.ShapeDtypeStruct((bx, t, QCAT), BF16),
        jax.ShapeDtypeStruct((bx, VT_ROWS, t), BF16),
        jax.ShapeDtypeStruct((bx, t, KV_LORA), F32),
        jax.ShapeDtypeStruct((bx, t, QK_ROPE), F32),
        jax.ShapeDtypeStruct((bx, t, zw), F32),
    )
    out_specs = (
        pl.BlockSpec((bb, MLA_HEADS, tm, QCAT), lambda b, i: (b, 0, i, 0)),
        row((QCAT,)),
        pl.BlockSpec((bb, VT_ROWS, tm), lambda b, i: (b, 0, i)),
        row((KV_LORA,)), row((QK_ROPE,)), row((zw,)),
    )
    weights = (w['norm_mix_pre'], w['w_in'], w['q_norm'], w['kv_norm'], w['w_uq'], w['w_uk'], w['pe_spread'],
               w['hg_lb'])
    return pl.pallas_call(
        functools.partial(_in_proj_body, bb=bb, tm=tm, n_valid=n_valid, pad_front=pad_front),
        grid=(bx // bb, nt),
        in_specs=[row((d,))] + [tab] * len(tabs) + [_const_spec(a.shape) for a in weights],
        out_specs=out_specs,
        out_shape=out_shape,
        compiler_params=pltpu.CompilerParams(dimension_semantics=("parallel", "parallel"),
                                             vmem_limit_bytes=VMEM_LIMIT),
        name="in_proj",
    )(x, *tabs, *weights)


def _mla_prompt_body(*refs, qt, j0, aliased, dummy, prefix):
    q_ref, k_ref, vt_ref = refs[:3]
    kpre_ref, vtpre_ref = refs[3:5] if prefix else (None, None)
    o_ref, m_scr, top_scr, acc_scr = refs[-4:]
    j = pl.program_id(1) + j0
    width = MLA_HEADS * qt
    pre = PREFIX_KEYS if prefix else 0
    lower_half = lax.broadcasted_iota(jnp.int32, (CHUNK, LANES), 1) < CHUNK

    def hidden(rows, cols=LANES):
        return jnp.full((rows, cols), -jnp.inf, F32)

    def mask_dummy(s):
        return jnp.concatenate([hidden(dummy, s.shape[1]), s[dummy:]], axis=0) if dummy else s

    def mask_diagonal(s):
        strips = []
        for c0 in range(0, width, LANES):
            rows = [s[:pre, c0:c0 + LANES]] if prefix else []
            t = s[pre:, c0:c0 + LANES]
            if (c0 % qt) // LANES == 0:
                rows += [t[:CHUNK], jnp.where(lower_half, -jnp.inf, t[CHUNK:2 * CHUNK]), hidden(2 * CHUNK)]
            else:
                rows += [t[:3 * CHUNK], jnp.where(lower_half, -jnp.inf, t[3 * CHUNK:])]
            strips.append(jnp.concatenate(rows, axis=0))
        return jnp.concatenate(strips, axis=1)

    def keys_of(kb):
        start = kb * KEY_BLOCK
        return pl.ds(start if isinstance(kb, int) else pl.multiple_of(start, KEY_BLOCK), KEY_BLOCK)

    def scores(kb, diagonal=False):
        keys = k_ref[0, keys_of(kb), :]
        if diagonal and prefix:
            keys = jnp.concatenate([kpre_ref[...], keys], axis=0)
        return _mm_nt(keys, q_ref[0].reshape(width, QCAT))

    def weighted_values(p, kb, diagonal):
        if diagonal and prefix:
            return _mm(vtpre_ref[...], p[:pre]) + _mm(vt_ref[0, :, keys_of(kb)], p[pre:])
        return _mm(vt_ref[0, :, keys_of(kb)], p)

    def absorb(s, kb, diagonal=False):
        if diagonal:
            s = mask_diagonal(s)
        if isinstance(kb, int) and kb == 0:
            s = mask_dummy(s)
        for hd in range(MLA_HEADS):
            cols = slice(qt * hd, qt * (hd + 1))
            m_new = jnp.max(s[:, cols], axis=0, keepdims=True)
            if not diagonal:
                m_old = m_scr[:, cols]
                m_new = jnp.maximum(m_old, m_new)
            pv = weighted_values(jnp.exp2(s[:, cols] - m_new), kb, diagonal)
            acc_scr[:, cols] = pv if diagonal else jnp.exp2(m_old - m_new) * acc_scr[:, cols] + pv
            m_scr[:, cols] = m_new

    def absorb_all(blocks):
        for s, kb, diagonal in blocks:
            absorb(s, kb, diagonal)

    def absorb_unscaled(blocks):
        blocks = [(mask_diagonal(s) if diagonal else s, kb, diagonal) for s, kb, diagonal in blocks]
        blocks = [(mask_dummy(s) if isinstance(kb, int) and kb == 0 else s, kb, d) for s, kb, d in blocks]
        first = blocks[0][2]
        for hd in range(MLA_HEADS):
            cols = slice(qt * hd, qt * (hd + 1))
            tops = [jnp.max(s[:, cols], axis=0, keepdims=True) for s, _, _ in blocks]
            if first:
                shift = jnp.max(blocks[0][0][:PREFIX_KEYS, cols], axis=0, keepdims=True) if prefix else tops[0]
                m_scr[:, cols] = shift
                top_scr[:, cols] = functools.reduce(jnp.maximum, tops)
            else:
                shift = m_scr[:, cols]
                top_scr[:, cols] = functools.reduce(jnp.maximum, tops, top_scr[:, cols])
            pv = None
            for s, kb, diagonal in blocks:
                term = weighted_values(jnp.exp2(s[:, cols] - shift), kb, diagonal)
                pv = term if pv is None else pv + term
            acc_scr[:, cols] = pv if first else acc_scr[:, cols] + pv

    def visit_blocks(absorb_blocks):
        def pair(a, b, diagonal=False):
            s_a, s_b = scores(a, diagonal), scores(b)
            absorb_blocks([(s_a, a, diagonal), (s_b, b, False)])

        def absorb_block(s, kb, diagonal=False):
            absorb_blocks([(s, kb, diagonal)])

        @pl.when(j == 0)
        def _():
            absorb_block(scores(0, True), 0, True)

        n_blocks = k_ref.shape[1] // KEY_BLOCK
        if n_blocks < 2:
            return

        @pl.when(j == 1)
        def _():
            pair(1, 0, True)

        if n_blocks < 3:
            return

        @pl.when(j > 1)
        def _():
            pair(j, j - 1, True)
            n_inner = (j - 2) // 2

            def inner(p, carry):
                pair(j - 2 * p, j - 2 * p - 1)
                return carry

            lax.fori_loop(1, n_inner + 1, inner, 0)

            @pl.when(j % 2 == 0)
            def _():
                absorb_block(scores(0), 0)

            @pl.when(j % 2 == 1)
            def _():
                pair(1, 0)

    def finish():
        for hd in range(MLA_HEADS):
            acc = acc_scr[:, qt * hd:qt * (hd + 1)]
            o_t = acc[:KV_LORA] * (1.0 / acc[KV_LORA:KV_LORA + 1])
            o_ref[0, :, KV_LORA * hd:KV_LORA * (hd + 1)] = o_t.T.astype(BF16)

    visit_blocks(absorb_unscaled)
    runaway = jnp.max(top_scr[...] - m_scr[...]) > SOFTMAX_FAST_RANGE
    finish()

    @pl.when(runaway)
    def _():
        visit_blocks(absorb_all)
        finish()


def _mla_prompt_call(qcat, kcat, kvt, prefix, prev, *, qt, j0, steps, dummy):
    b, _, t, _ = qcat.shape
    per_block = KEY_BLOCK // qt
    in_specs = [pl.BlockSpec((1, MLA_HEADS, qt, QCAT), lambda i, j: (i, 0, (j + j0) * per_block, 0)),
                pl.BlockSpec((1, t, QCAT), lambda i, j: (i, 0, 0)),
                pl.BlockSpec((1, VT_ROWS, t), lambda i, j: (i, 0, 0))]
    args = (qcat, kcat, kvt)
    if prefix is not None:
        in_specs += [_const_spec(a.shape) for a in prefix]
        args += tuple(prefix)
    if prev is not None:
        in_specs.append(pl.BlockSpec(memory_space=pl.ANY))
        args += (prev,)
    return pl.pallas_call(
        functools.partial(_mla_prompt_body, qt=qt, j0=j0, aliased=prev is not None, dummy=dummy,
                          prefix=prefix is not None),
        grid=(b, steps),
        in_specs=in_specs,
        out_specs=pl.BlockSpec((1, qt, MLA_HEADS * KV_LORA), lambda i, j: (i, (j + j0) * per_block, 0)),
        out_shape=jax.ShapeDtypeStruct((b, t, MLA_HEADS * KV_LORA), BF16),
        input_output_aliases={len(args) - 1: 0} if prev is not None else {},
        scratch_shapes=[pltpu.VMEM((1, MLA_HEADS * qt), F32), pltpu.VMEM((1, MLA_HEADS * qt), F32),
                        pltpu.VMEM((VT_ROWS, MLA_HEADS * qt), F32)],
        compiler_params=pltpu.CompilerParams(dimension_semantics=("parallel", "parallel"),
                                             vmem_limit_bytes=VMEM_LIMIT),
        name="mla_prompt",
    )(*args)


def _mla_prompt(qcat, kcat, kvt, prefix, *, lp, dummy):
    t = qcat.shape[2]
    assert t % KEY_BLOCK == 0 and lp <= t
    full, rest = divmod(lp, KEY_BLOCK)
    if rest > LANES:
        full, rest = full + 1, 0
    out = None
    if full:
        out = _mla_prompt_call(qcat, kcat, kvt, prefix, None, qt=KEY_BLOCK, j0=0, steps=full, dummy=dummy)
    if rest:
        out = _mla_prompt_call(qcat, kcat, kvt, prefix, out, qt=LANES, j0=full, steps=1, dummy=dummy)
    return out


def _mla_sample_body(q_ref, k_ref, ckv_ref, cpe_ref, o_ref, *, ts):
    rows = MLA_HEADS * ts
    q = q_ref[0].reshape(rows, QCAT)
    ckv = ckv_ref[0].astype(BF16)
    knew = k_ref[0]
    s_c = _mm_nt(q[:, :KV_LORA], ckv) + _mm_nt(q[:, KV_LORA:KV_LORA + QK_ROPE], cpe_ref[0])
    s_n = _mm_nt(q, knew)
    m = jnp.maximum(jnp.max(s_c, axis=-1, keepdims=True), jnp.max(s_n, axis=-1, keepdims=True))
    p_c = jnp.exp2(s_c - m)
    p_n = jnp.exp2(s_n - m)
    l = jnp.sum(p_c, axis=-1, keepdims=True) + jnp.sum(p_n, axis=-1, keepdims=True)
    o = (_mm(p_c, ckv) + _mm(p_n, knew[:, :KV_LORA])) / l
    for hd in range(MLA_HEADS):
        o_ref[0, :, KV_LORA * hd:KV_LORA * (hd + 1)] = o[ts * hd:ts * (hd + 1)].astype(BF16)


def _mla_sample(qcat, kcat, cache_kv, cache_pe):
    b, _, ts, _ = qcat.shape
    past = cache_kv.shape[1]
    return pl.pallas_call(
        functools.partial(_mla_sample_body, ts=ts),
        grid=(b,),
        in_specs=[pl.BlockSpec((1, MLA_HEADS, ts, QCAT), lambda i: (i, 0, 0, 0)),
                  pl.BlockSpec((1, ts, QCAT), lambda i: (i, 0, 0)),
                  pl.BlockSpec((1, past, KV_LORA), lambda i: (i, 0, 0)),
                  pl.BlockSpec((1, past, QK_ROPE), lambda i: (i, 0, 0))],
        out_specs=pl.BlockSpec((1, ts, MLA_HEADS * KV_LORA), lambda i: (i, 0, 0)),
        out_shape=jax.ShapeDtypeStruct((b, ts, MLA_HEADS * KV_LORA), BF16),
        compiler_params=pltpu.CompilerParams(dimension_semantics=("parallel",),
                                             vmem_limit_bytes=VMEM_LIMIT),
        name="mla_sample",
    )(qcat, kcat, cache_kv, cache_pe)


def _hgrn_body(zh_ref, s0_ref, ob_ref, sout_ref, st_ref, g_scr, q_scr, k_scr, v_scr, o_scr, oi_scr, *, bb, cs):
    c = pl.program_id(1)
    nc = pl.num_programs(1)
    streams = range(bb)
    heads = [slice(HG_DK * hd, HG_DK * (hd + 1)) for hd in range(HG_HEADS)]

    @pl.when(c == 0)
    def _():
        for bi in streams:
            for hd in range(HG_HEADS):
                st_ref[bi, hd] = s0_ref[bi, hd].T

    q = zh_ref[:, :, :HG_KW]
    gl = zh_ref[:, :, HG_KW:2 * HG_KW]
    k = zh_ref[:, :, 2 * HG_KW:3 * HG_KW]
    v = zh_ref[:, :, 3 * HG_KW:3 * HG_KW + HG_WIDTH]
    t_idx = lax.broadcasted_iota(jnp.int32, (cs, cs), 0)
    s_idx = lax.broadcasted_iota(jnp.int32, (cs, cs), 1)
    causal = t_idx >= s_idx
    row = lax.broadcasted_iota(jnp.int32, (cs, 1), 0)
    g = [gl[bi] for bi in streams]
    shift = 1
    while shift < cs:
        g = [x + jnp.where(row >= shift, pltpu.roll(x, shift, 0), 0.0) for x in g]
        shift *= 2
    g = jnp.stack(g)
    g_tot = g[:, cs - 1:cs, :]
    any_out_of_range = jnp.max(-g_tot) > HGRN_FAST_RANGE
    qt = q * jnp.exp(g)
    khat = k * jnp.exp(g_tot - g)
    kt = k * jnp.exp(-g)
    a = [[jnp.where(causal, _mm_nt(qt[bi][:, sl], kt[bi][:, sl]), 0.0) for sl in heads] for bi in streams]
    for bi in streams:
        o_scr[bi] = jnp.concatenate([_mm(a[bi][hd], v[bi][:, sl]) for hd, sl in enumerate(heads)], axis=-1)

    o_inter = [[_mm_nt(qt[bi][:, sl], st_ref[bi, hd]) for hd, sl in enumerate(heads)] for bi in streams]
    kv_new = [[_mm_tn(v[bi][:, sl], khat[bi][:, sl]) for sl in heads] for bi in streams]
    decay = jnp.exp(g_tot)
    for bi in streams:
        oi_scr[bi] = jnp.concatenate(o_inter[bi], axis=-1)
        for hd, sl in enumerate(heads):
            st_ref[bi, hd] = decay[bi][:, sl] * st_ref[bi, hd] + kv_new[bi][hd]

    @pl.when(any_out_of_range)
    def _():
        for bi in streams:
            @pl.when(jnp.max(-g_tot[bi]) > HGRN_FAST_RANGE)
            def _(bi=bi):
                g_scr[...] = g[bi]
                q_scr[...] = q[bi]
                k_scr[...] = k[bi]
                v_scr[...] = v[bi]
                o_scr[bi] = jnp.zeros((cs, HG_WIDTH), F32)
                t_col = lax.broadcasted_iota(jnp.int32, (cs, 1), 0)

                def cols(i, carry):
                    base = pl.multiple_of(i * 8, 8)
                    g8, k8, v8 = (ref[pl.ds(base, 8), :] for ref in (g_scr, k_scr, v_scr))
                    for j in range(8):
                        dec = jnp.exp(jnp.where(t_col >= base + j, g_scr[...] - g8[j:j + 1], -jnp.inf))
                        pr = q_scr[...] * dec * k8[j:j + 1]
                        for sl in heads:
                            o_scr[bi, :, sl] += jnp.sum(pr[:, sl], axis=-1, keepdims=True) * v8[j:j + 1, sl]
                    return carry

                lax.fori_loop(0, cs // 8, cols, 0)

    for bi in streams:
        ob_ref[bi] = o_scr[bi] + oi_scr[bi]

    @pl.when(c == nc - 1)
    def _():
        for bi in streams:
            for hd in range(HG_HEADS):
                sout_ref[bi, hd] = st_ref[bi, hd].T


def _hgrn(zh, s0, *, n_rows, cs):
    b, t, zw = zh.shape
    assert n_rows % cs == 0 and n_rows <= t
    bb = HGRN_STREAMS if b % HGRN_STREAMS == 0 else 1
    return pl.pallas_call(
        functools.partial(_hgrn_body, bb=bb, cs=cs),
        grid=(b // bb, n_rows // cs),
        in_specs=[pl.BlockSpec((bb, cs, zw), lambda i, c: (i, c, 0)),
                  pl.BlockSpec((bb, HG_HEADS, HG_DK, HG_DV), lambda i, c: (i, 0, 0, 0))],
        out_specs=(pl.BlockSpec((bb, cs, HG_WIDTH), lambda i, c: (i, c, 0)),
                   pl.BlockSpec((bb, HG_HEADS, HG_DK, HG_DV), lambda i, c: (i, 0, 0, 0))),
        out_shape=(jax.ShapeDtypeStruct((b, t, HG_WIDTH), F32),
                   jax.ShapeDtypeStruct((b, HG_HEADS, HG_DK, HG_DV), F32)),
        scratch_shapes=[pltpu.VMEM((bb, HG_HEADS, HG_DV, HG_DK), F32)]
        + [pltpu.VMEM((cs, HG_KW), F32) for _ in range(4)] + [pltpu.VMEM((bb, cs, HG_KW), F32) for _ in range(2)],
        compiler_params=pltpu.CompilerParams(dimension_semantics=("parallel", "arbitrary"),
                                             vmem_limit_bytes=VMEM_LIMIT),
        name="hgrn",
    )(zh, s0)


def _out_ffn_body(x_ref, ol_ref, ob_ref, gate_ref, buf_ref, gain_ref, wuv_ref, wout_ref, npost_ref, nfpre_ref,
                  wup_ref, cw_ref, cb_ref, wdn_ref, nfpost_ref, fin_ref, xo_ref, bufo_ref, a_scr,
                  *, bb, tm, pad_front, last):
    t = pl.program_id(1)
    nt = pl.num_programs(1)
    d_ff = cw_ref.shape[1]
    keep = CONV_W - 1
    streams = range(bb)

    @pl.when(t == 0)
    def _():
        for bi in streams:
            a_scr[bi, 0:8, :] = jnp.zeros((8, d_ff), F32)
            a_scr[bi, 8 - keep:8, :] = buf_ref[bi]

    mixed = []
    for bi in streams:
        ol = ol_ref[bi]
        o_a = [_mm(ol[:, 2 * KV_LORA * j:2 * KV_LORA * (j + 1)], wuv_ref[j]) for j in range(MLA_HEADS // 2)]
        ob = ob_ref[bi]
        o_b = [_rms(ob[:, HG_DV * hd:HG_DV * (hd + 1)], gain_ref[...]) for hd in range(HG_HEADS)]
        o_b = (jnp.concatenate(o_b, axis=-1) * _silu(gate_ref[bi])).astype(BF16)
        mixed.append(jnp.concatenate([o.astype(BF16) for o in o_a] + [o_b], axis=-1))
    y = [_mm(mixed[bi], wout_ref[...]) for bi in streams]
    x1 = [x_ref[bi] + _rms(y[bi], npost_ref[...]) for bi in streams]
    u = [_mm(_rms(x1[bi], nfpre_ref[...]), wup_ref[...]) for bi in streams]
    cw = cw_ref[...]
    if pad_front:
        live = (t * tm + lax.broadcasted_iota(jnp.int32, (tm, 1), 0)) >= pad_front
    f = []
    for bi in streams:
        a = u[bi][:, :d_ff]
        if pad_front:
            a = jnp.where(live, a, 0.0)
        a_scr[bi, 8:8 + tm, :] = a
        conv = cb_ref[...] + a * cw[keep:keep + 1]
        for j in range(keep):
            conv = conv + a_scr[bi, 8 - keep + j:8 - keep + j + tm, :] * cw[j:j + 1]
        f.append(_mm(_silu(conv) * u[bi][:, d_ff:], wdn_ref[...]))
    for bi in streams:
        x2 = x1[bi] + _rms(f[bi], nfpost_ref[...])
        xo_ref[bi] = _rms(x2, fin_ref[...]) if last else x2
        a_scr[bi, 0:8, :] = a_scr[bi, tm:tm + 8, :]

    @pl.when(t == nt - 1)
    def _():
        for bi in streams:
            bufo_ref[bi] = a_scr[bi, 8 - keep:8, :]


def _out_ffn(x, ol, ob, zh, buf, w, final_norm, *, n_rows, pad_front, last):
    b, t, d = x.shape
    d_ff = buf.shape[-1]
    tm = _row_tile(n_rows, FFN_ROWS)
    assert tm >= 8 and n_rows <= t and zh.shape[-1] % HG_WIDTH == 0
    bb = FFN_STREAMS if b % FFN_STREAMS == 0 else 1
    row = lambda width: pl.BlockSpec((bb, tm, width), lambda i, j: (i, j, 0))
    gate = pl.BlockSpec((bb, tm, HG_WIDTH), lambda i, j: (i, j, zh.shape[-1] // HG_WIDTH - 1))
    per_b = pl.BlockSpec((bb, CONV_W - 1, d_ff), lambda i, j: (i, 0, 0))
    weights = (w['hg_gain'], w['w_uv'], w['w_out'], w['norm_mix_post'], w['norm_ffn_pre'], w['w_ffn_up'],
               w['ffn_conv_w'], w['ffn_conv_b'], w['w_ffn_down'], w['norm_ffn_post'], final_norm)
    return pl.pallas_call(
        functools.partial(_out_ffn_body, bb=bb, tm=tm, pad_front=pad_front, last=last),
        grid=(b // bb, n_rows // tm),
        in_specs=[row(d), row(MLA_HEADS * KV_LORA), row(HG_WIDTH), gate, per_b]
        + [_const_spec(a.shape) for a in weights],
        out_specs=(row(d), per_b),
        out_shape=(jax.ShapeDtypeStruct((b, t, d), F32), jax.ShapeDtypeStruct((b, CONV_W - 1, d_ff), F32)),
        scratch_shapes=[pltpu.VMEM((bb, tm + 8, d_ff), F32)],
        compiler_params=pltpu.CompilerParams(dimension_semantics=("parallel", "arbitrary"),
                                             vmem_limit_bytes=VMEM_LIMIT),
        name="out_ffn",
    )(x, ol, ob, zh, buf, *weights)


def _rope_tables(pos):
    half = QK_ROPE // 2
    inv = ROPE_THETA ** (-jnp.arange(half, dtype=F32) / half)
    ang = pos.astype(F32)[:, None] * inv
    cos, sin = jnp.cos(ang), jnp.sin(ang)
    zero = jnp.zeros_like(cos)
    pad = jnp.zeros((pos.shape[0], LANES - QK_ROPE), F32)
    return (jnp.concatenate([cos, cos, pad], axis=1),
            jnp.concatenate([-sin, zero, pad], axis=1),
            jnp.concatenate([zero, sin, pad], axis=1),
            jnp.tile(cos, (1, MLA_HEADS)), jnp.tile(sin, (1, MLA_HEADS)))


def _layer_weights(l, w_in, norm_mix_pre, norm_mix_post, q_norm, kv_norm, w_uq, w_uk, w_uv, hg_out_norm, w_out,
                   norm_ffn_pre, norm_ffn_post, w_ffn_up, ffn_conv_w, ffn_conv_b, w_ffn_down):
    d = w_in.shape[1]
    head_w = Q_LORA + KV_LORA + QK_ROPE
    w_in_r = jnp.concatenate([w_in[l][:, :head_w], jnp.zeros((d, LANES - QK_ROPE), F32), w_in[l][:, head_w:]], axis=1)
    uq = w_uq[l].reshape(Q_LORA, MLA_HEADS, QK_NOPE + QK_ROPE)
    half = QK_ROPE // 2
    w_uq_r = jnp.concatenate([uq[:, :, :QK_NOPE].reshape(Q_LORA, -1),
                              uq[:, :, QK_NOPE:QK_NOPE + half].reshape(Q_LORA, -1),
                              uq[:, :, QK_NOPE + half:].reshape(Q_LORA, -1)], axis=1)
    src = jnp.arange(2 * MLA_HEADS * half)
    dst = (src % (MLA_HEADS * half)) // half * LANES + src // (MLA_HEADS * half) * half + src % half
    pe_spread = jnp.zeros((2 * MLA_HEADS * half, MLA_HEADS * LANES), F32).at[src, dst].set(1.0)
    uk = jnp.transpose(w_uk[l], (1, 2, 0))
    uk_pair = jnp.zeros((MLA_HEADS // 2, 2 * QK_NOPE, 2 * KV_LORA), F32)
    uk_pair = uk_pair.at[:, :QK_NOPE, :KV_LORA].set(uk[0::2]).at[:, QK_NOPE:, KV_LORA:].set(uk[1::2])
    uv = jnp.transpose(w_uv[l], (1, 0, 2))
    uv_pair = jnp.zeros((MLA_HEADS // 2, 2 * KV_LORA, 2 * V_HEAD), F32)
    uv_pair = uv_pair.at[:, :KV_LORA, :V_HEAD].set(uv[0::2]).at[:, KV_LORA:, V_HEAD:].set(uv[1::2])
    row = lambda a: a.reshape(1, -1)
    return {
        'w_in': w_in_r.astype(BF16), 'norm_mix_pre': row(norm_mix_pre[l]), 'norm_mix_post': row(norm_mix_post[l]),
        'q_norm': row(q_norm[l]), 'kv_norm': row(kv_norm[l]), 'w_uq': w_uq_r.astype(BF16),
        'w_uk': uk_pair.astype(BF16), 'w_uv': uv_pair.astype(BF16), 'pe_spread': pe_spread.astype(BF16),
        'hg_gain': row(hg_out_norm[l]), 'w_out': w_out[l].astype(BF16),
        'norm_ffn_pre': row(norm_ffn_pre[l]), 'norm_ffn_post': row(norm_ffn_post[l]),
        'w_ffn_up': w_ffn_up[l].astype(BF16), 'ffn_conv_w': ffn_conv_w[l], 'ffn_conv_b': row(ffn_conv_b[l]),
        'w_ffn_down': w_ffn_down[l].astype(BF16),
    }


def kernel(x_prompt, x_sample, cache_kv_latent, cache_k_rope, state_hgrn, state_ffn_conv, meta_tokens, w_in, norm_mix_pre, norm_mix_post, q_norm, kv_norm, w_uq, w_uk, w_uv, hg_lower_bounds, hg_out_norm, w_out, norm_ffn_pre, norm_ffn_post, w_ffn_up, ffn_conv_w, ffn_conv_b, w_ffn_down, final_norm):
    depth = w_in.shape[0]
    b, seq, d = x_prompt.shape
    bs, ts, _ = x_sample.shape
    past = cache_kv_latent.shape[2]
    d_ff = ffn_conv_w.shape[-1]
    assert seq % CHUNK == 0

    lb_sm = jax.nn.softmax(hg_lower_bounds.astype(F32), axis=0)
    lb_all = jnp.cumsum(lb_sm, axis=0) - lb_sm[0]

    unit = IN_PROJ_ROWS * KEY_BLOCK // 128
    lm = PAD_FRONT + N_META
    lm_alloc = -(-lm // unit) * unit
    xm = jnp.concatenate([jnp.zeros((1, PAD_FRONT, d), F32), meta_tokens[None],
                          jnp.zeros((1, lm_alloc - lm, d), F32)], axis=1)
    tabs_m = _rope_tables(jnp.arange(lm_alloc, dtype=jnp.int32) - lm)
    seq_alloc = -(-seq // KEY_BLOCK) * KEY_BLOCK
    xp = x_prompt if seq_alloc == seq else jnp.pad(x_prompt, ((0, 0), (0, seq_alloc - seq), (0, 0)))
    tabs_p = _rope_tables(jnp.arange(seq_alloc, dtype=jnp.int32))
    xs = x_sample
    tabs_s = tuple(jnp.tile(a, (bs, 1)) for a in _rope_tables(past + jnp.arange(ts, dtype=jnp.int32)))
    fin = final_norm.reshape(1, -1)
    s0_m = jnp.zeros((1, HG_HEADS, HG_DK, HG_DV), F32)
    buf0_m = jnp.zeros((1, CONV_W - 1, d_ff), F32)
    meta_rows = slice(PAD_FRONT, lm)

    outs = [[] for _ in range(8)]
    for l in range(depth):
        w = _layer_weights(l, w_in, norm_mix_pre, norm_mix_post, q_norm, kv_norm, w_uq, w_uk, w_uv, hg_out_norm,
                           w_out, norm_ffn_pre, norm_ffn_post, w_ffn_up, ffn_conv_w, ffn_conv_b, w_ffn_down)
        w['hg_lb'] = lb_all[l].reshape(1, -1)
        last = l == depth - 1

        qcat, kcat, kvt, kv_m, pe_m, zh = _in_proj(xm, tabs_m, w, n_valid=lm, pad_front=PAD_FRONT)
        ol = _mla_prompt(qcat, kcat, kvt, None, lp=lm, dummy=PAD_FRONT)
        ob, s_m = _hgrn(zh, s0_m, n_rows=lm, cs=CHUNK)
        xm, buf_m = _out_ffn(xm, ol, ob, zh, buf0_m, w, fin, n_rows=lm, pad_front=PAD_FRONT, last=last)
        prefix = (kcat[0, meta_rows], kvt[0, :, meta_rows])

        qcat, kcat, kvt, kv, pe, zh = _in_proj(xp, tabs_p, w, n_valid=seq, pad_front=0)
        ol = _mla_prompt(qcat, kcat, kvt, prefix, lp=seq, dummy=0)
        ob, s_new = _hgrn(zh, jnp.broadcast_to(s_m, (b,) + s_m.shape[1:]), n_rows=seq, cs=CHUNK)
        xp, buf = _out_ffn(xp, ol, ob, zh, jnp.broadcast_to(buf_m, (b,) + buf_m.shape[1:]), w, fin, n_rows=seq,
                           pad_front=0, last=last)
        with_meta = lambda m, p: jnp.concatenate(
            [jnp.broadcast_to(m[:, meta_rows], (b, N_META, m.shape[-1])), p[:, :seq]], axis=1)
        for lst, a in zip(outs[:4], (with_meta(kv_m, kv), with_meta(pe_m, pe), s_new, buf)):
            lst.append(a)

        qcat, kcat, _, kv, pe, zh = _in_proj(xs.reshape(1, bs * ts, d), tabs_s, w, n_valid=bs * ts, pad_front=0)
        qcat = jnp.swapaxes(qcat.reshape(MLA_HEADS, bs, ts, QCAT), 0, 1)
        kcat, kv, pe, zh = (a.reshape(bs, ts, -1) for a in (kcat, kv, pe, zh))
        ol = _mla_sample(qcat, kcat, cache_kv_latent[l], cache_k_rope[l])
        ob, s_new = _hgrn(zh, state_hgrn[l], n_rows=ts, cs=ts)
        xs, buf = _out_ffn(xs, ol, ob, zh, state_ffn_conv[l], w, fin, n_rows=ts, pad_front=0, last=last)
        for lst, a in zip(outs[4:], (kv, pe, s_new, buf)):
            lst.append(a)

    y_prompt = xp if seq_alloc == seq else xp[:, :seq]
    return (y_prompt, xs) + tuple(jnp.stack(o) for o in outs)
```

```python
import functools

import jax
import jax.numpy as jnp
from jax import lax
from jax.experimental import pallas as pl
from jax.experimental.pallas import tpu as pltpu

F32 = jnp.float32
BF16 = jnp.bfloat16

CHUNK = 64
N_META = 16
PAD_FRONT = CHUNK - N_META
EPS = 1e-6
MLA_HEADS = 8
QK_NOPE = 64
QK_ROPE = 32
V_HEAD = 64
Q_LORA = 256
KV_LORA = 128
ROPE_THETA = 10000.0
HG_HEADS = 4
HG_DK = 128
HG_DV = 128
CONV_W = 3

HG_KW = HG_HEADS * HG_DK
HG_WIDTH = HG_HEADS * HG_DV
MLA_WIDTH = MLA_HEADS * V_HEAD
LANES = 128
KEY_BLOCK = 256
QCAT = KV_LORA + LANES
VT_ROWS = KV_LORA + 16
SOFTMAX_FAST_RANGE = 100.0
PREFIX_KEYS = N_META
IN_PROJ_ROWS = 256
IN_PROJ_STREAMS = 2
Q_SCALE = (QK_NOPE + QK_ROPE) ** -0.5 * 1.4426950408889634
HGRN_FAST_RANGE = 80.0
FFN_ROWS = 256
FFN_STREAMS = 2
HGRN_STREAMS = 8
VMEM_LIMIT = 56 * 1024 * 1024


def _row_tile(n, target):
    if n <= target:
        return n
    best = None
    for t in range(16, target + 1, 16):
        if n % t == 0:
            best = t
    assert best is not None, (n, target)
    return best


def _const_spec(shape):
    nd = len(shape)
    return pl.BlockSpec(shape, lambda *_: (0,) * nd, pipeline_mode=pl.Buffered(1))


def _rms(x, w):
    return x * lax.rsqrt(jnp.mean(x * x, axis=-1, keepdims=True) + EPS) * w


def _silu(x):
    return x * jax.nn.sigmoid(x)


def _mm(a, b):
    return jnp.dot(a.astype(BF16), b.astype(BF16), preferred_element_type=F32)


def _mm_nt(a, b):
    return lax.dot_general(a.astype(BF16), b.astype(BF16), (((1,), (1,)), ((), ())),
                           preferred_element_type=F32)


def _mm_tn(a, b):
    return lax.dot_general(a.astype(BF16), b.astype(BF16), (((0,), (0,)), ((), ())),
                           preferred_element_type=F32)


def _in_proj_body(x_ref, cos_ref, sn_ref, sp_ref, cosq_ref, sinq_ref, npre_ref, win_ref, qn_ref, kvn_ref, wuq_ref,
                  wuk_ref, spread_ref, lb_ref, qcat_ref, kcat_ref, kvt_ref, kv_ref, pe_ref, zh_ref,
                  *, bb, tm, n_valid, pad_front):
    cos, sn, sp = cos_ref[...], sn_ref[...], sp_ref[...]
    lb = lb_ref[...]

    def rope(v):
        return v * cos + pltpu.roll(v, LANES - QK_ROPE // 2, 1) * sn + pltpu.roll(v, QK_ROPE // 2, 1) * sp

    row = pl.program_id(1) * tm + lax.broadcasted_iota(jnp.int32, (tm, 1), 0)
    live = row < n_valid
    real = row >= pad_front
    h = [_rms(jnp.where(live, x_ref[bi], 0.0), npre_ref[...]) for bi in range(bb)]
    zs = [_mm(h[bi], win_ref[...]) for bi in range(bb)]
    nope_w = MLA_HEADS * QK_NOPE
    for bi, z in enumerate(zs):
        off = Q_LORA + KV_LORA + LANES
        hf = z[:, off + HG_KW:off + 2 * HG_KW]
        e = jnp.exp(-jnp.abs(hf))
        r = 1.0 / (1.0 + e)
        f = lb + (1.0 - lb) * jnp.where(hf >= 0.0, r, e * r)
        zh_ref[bi, :, :HG_KW] = jnp.where(real, _silu(z[:, off:off + HG_KW]), 0.0)
        zh_ref[bi, :, HG_KW:2 * HG_KW] = jnp.where(real, jnp.log(f), 0.0)
        zh_ref[bi, :, 2 * HG_KW:3 * HG_KW] = jnp.where(real, (1.0 - lb) * jnp.where(hf >= 0.0, e * r, r), 0.0)
        zh_ref[bi, :, 3 * HG_KW:] = z[:, off + 2 * HG_KW:]
        kv_lat = _rms(z[:, Q_LORA:Q_LORA + KV_LORA], kvn_ref[...])
        k_pe = rope(z[:, Q_LORA + KV_LORA:Q_LORA + KV_LORA + LANES])
        kv_ref[bi] = kv_lat
        pe_ref[bi] = k_pe[:, :QK_ROPE]
        kcat_ref[bi, :, :KV_LORA] = kv_lat.astype(BF16)
        kcat_ref[bi, :, KV_LORA:] = k_pe.astype(BF16)
        kvt_ref[bi, :KV_LORA, :] = kv_lat.T.astype(BF16)
        kvt_ref[bi, KV_LORA:, :] = jnp.ones((VT_ROWS - KV_LORA, tm), BF16)

        cqn = _rms(z[:, :Q_LORA], qn_ref[...])
        q = _mm(cqn, wuq_ref[...]) * Q_SCALE
        x1, x2 = q[:, nope_w:nope_w + LANES], q[:, nope_w + LANES:]
        cq, sq = cosq_ref[...], sinq_ref[...]
        roped = jnp.concatenate([x1 * cq - x2 * sq, x1 * sq + x2 * cq], axis=1)
        pe_heads = _mm(roped, spread_ref[...])
        for j in range(MLA_HEADS // 2):
            ql = _mm(q[:, 2 * QK_NOPE * j:2 * QK_NOPE * (j + 1)], wuk_ref[j])
            for hh in range(2):
                hd = 2 * j + hh
                qcat_ref[bi, hd, :, :KV_LORA] = ql[:, KV_LORA * hh:KV_LORA * (hh + 1)].astype(BF16)
                qcat_ref[bi, hd, :, KV_LORA:] = pe_heads[:, LANES * hd:LANES * (hd + 1)].astype(BF16)


def _in_proj(x, tabs, w, *, n_valid, pad_front):
    bx, t, d = x.shape
    tm = _row_tile(t, IN_PROJ_ROWS)
    assert tm % LANES == 0 or tm == t
    nt = t // tm
    bb = IN_PROJ_STREAMS if bx % IN_PROJ_STREAMS == 0 else 1
    row = lambda shape: pl.BlockSpec((bb, tm) + shape, lambda b, i: (b, i) + (0,) * len(shape))
    tab = pl.BlockSpec((tm, LANES), lambda b, i: (i, 0))
    zw = 3 * HG_KW + 2 * HG_WIDTH
    out_shape = (
        jax.ShapeDtypeStruct((bx, MLA_HEADS, t, QCAT), BF16),
        jax.ShapeDtypeStruct((bx, t, QCAT), BF16),
        jax.ShapeDtypeStruct((bx, VT_ROWS, t), BF16),
        jax.ShapeDtypeStruct((bx, t, KV_LORA), F32),
        jax.ShapeDtypeStruct((bx, t, QK_ROPE), F32),
        jax.ShapeDtypeStruct((bx, t, zw), F32),
    )
    out_specs = (
        pl.BlockSpec((bb, MLA_HEADS, tm, QCAT), lambda b, i: (b, 0, i, 0)),
        row((QCAT,)),
        pl.BlockSpec((bb, VT_ROWS, tm), lambda b, i: (b, 0, i)),
        row((KV_LORA,)), row((QK_ROPE,)), row((zw,)),
    )
    weights = (w['norm_mix_pre'], w['w_in'], w['q_norm'], w['kv_norm'], w['w_uq'], w['w_uk'], w['pe_spread'],
               w['hg_lb'])
    return pl.pallas_call(
        functools.partial(_in_proj_body, bb=bb, tm=tm, n_valid=n_valid, pad_front=pad_front),
        grid=(bx // bb, nt),
        in_specs=[row((d,))] + [tab] * len(tabs) + [_const_spec(a.shape) for a in weights],
        out_specs=out_specs,
        out_shape=out_shape,
        compiler_params=pltpu.CompilerParams(dimension_semantics=("parallel", "parallel"),
                                             vmem_limit_bytes=VMEM_LIMIT),
        name="in_proj",
    )(x, *tabs, *weights)


def _mla_prompt_body(*refs, qt, j0, aliased, dummy, prefix):
    q_ref, k_ref, vt_ref = refs[:3]
    kpre_ref, vtpre_ref = refs[3:5] if prefix else (None, None)
    o_ref, m_scr, top_scr, acc_scr = refs[-4:]
    j = pl.program_id(1) + j0
    width = MLA_HEADS * qt
    pre = PREFIX_KEYS if prefix else 0
    lower_half = lax.broadcasted_iota(jnp.int32, (CHUNK, LANES), 1) < CHUNK

    def hidden(rows, cols=LANES):
        return jnp.full((rows, cols), -jnp.inf, F32)

    def mask_dummy(s):
        return jnp.concatenate([hidden(dummy, s.shape[1]), s[dummy:]], axis=0) if dummy else s

    def mask_diagonal(s):
        strips = []
        for c0 in range(0, width, LANES):
            rows = [s[:pre, c0:c0 + LANES]] if prefix else []
            t = s[pre:, c0:c0 + LANES]
            if (c0 % qt) // LANES == 0:
                rows += [t[:CHUNK], jnp.where(lower_half, -jnp.inf, t[CHUNK:2 * CHUNK]), hidden(2 * CHUNK)]
            else:
                rows += [t[:3 * CHUNK], jnp.where(lower_half, -jnp.inf, t[3 * CHUNK:])]
            strips.append(jnp.concatenate(rows, axis=0))
        return jnp.concatenate(strips, axis=1)

    def keys_of(kb):
        start = kb * KEY_BLOCK
        return pl.ds(start if isinstance(kb, int) else pl.multiple_of(start, KEY_BLOCK), KEY_BLOCK)

    def scores(kb, diagonal=False):
        keys = k_ref[0, keys_of(kb), :]
        if diagonal and prefix:
            keys = jnp.concatenate([kpre_ref[...], keys], axis=0)
        return _mm_nt(keys, q_ref[0].reshape(width, QCAT))

    def weighted_values(p, kb, diagonal):
        if diagonal and prefix:
            return _mm(vtpre_ref[...], p[:pre]) + _mm(vt_ref[0, :, keys_of(kb)], p[pre:])
        return _mm(vt_ref[0, :, keys_of(kb)], p)

    def absorb(s, kb, diagonal=False):
        if diagonal:
            s = mask_diagonal(s)
        if isinstance(kb, int) and kb == 0:
            s = mask_dummy(s)
        for hd in range(MLA_HEADS):
            cols = slice(qt * hd, qt * (hd + 1))
            m_new = jnp.max(s[:, cols], axis=0, keepdims=True)
            if not diagonal:
                m_old = m_scr[:, cols]
                m_new = jnp.maximum(m_old, m_new)
            pv = weighted_values(jnp.exp2(s[:, cols] - m_new), kb, diagonal)
            acc_scr[:, cols] = pv if diagonal else jnp.exp2(m_old - m_new) * acc_scr[:, cols] + pv
            m_scr[:, cols] = m_new

    def absorb_all(blocks):
        for s, kb, diagonal in blocks:
            absorb(s, kb, diagonal)

    def absorb_unscaled(blocks):
        blocks = [(mask_diagonal(s) if diagonal else s, kb, diagonal) for s, kb, diagonal in blocks]
        blocks = [(mask_dummy(s) if isinstance(kb, int) and kb == 0 else s, kb, d) for s, kb, d in blocks]
        first = blocks[0][2]
        for hd in range(MLA_HEADS):
            cols = slice(qt * hd, qt * (hd + 1))
            tops = [jnp.max(s[:, cols], axis=0, keepdims=True) for s, _, _ in blocks]
            if first:
                shift = jnp.max(blocks[0][0][:PREFIX_KEYS, cols], axis=0, keepdims=True) if prefix else tops[0]
                m_scr[:, cols] = shift
                top_scr[:, cols] = functools.reduce(jnp.maximum, tops)
            else:
                shift = m_scr[:, cols]
                top_scr[:, cols] = functools.reduce(jnp.maximum, tops, top_scr[:, cols])
            pv = None
            for s, kb, diagonal in blocks:
                term = weighted_values(jnp.exp2(s[:, cols] - shift), kb, diagonal)
                pv = term if pv is None else pv + term
            acc_scr[:, cols] = pv if first else acc_scr[:, cols] + pv

    def visit_blocks(absorb_blocks):
        def pair(a, b, diagonal=False):
            s_a, s_b = scores(a, diagonal), scores(b)
            absorb_blocks([(s_a, a, diagonal), (s_b, b, False)])

        def absorb_block(s, kb, diagonal=False):
            absorb_blocks([(s, kb, diagonal)])

        @pl.when(j == 0)
        def _():
            absorb_block(scores(0, True), 0, True)

        n_blocks = k_ref.shape[1] // KEY_BLOCK
        if n_blocks < 2:
            return

        @pl.when(j == 1)
        def _():
            pair(1, 0, True)

        if n_blocks < 3:
            return

        @pl.when(j > 1)
        def _():
            pair(j, j - 1, True)
            n_inner = (j - 2) // 2

            def inner(p, carry):
                pair(j - 2 * p, j - 2 * p - 1)
                return carry

            lax.fori_loop(1, n_inner + 1, inner, 0)

            @pl.when(j % 2 == 0)
            def _():
                absorb_block(scores(0), 0)

            @pl.when(j % 2 == 1)
            def _():
                pair(1, 0)

    def finish():
        for hd in range(MLA_HEADS):
            acc = acc_scr[:, qt * hd:qt * (hd + 1)]
            o_t = acc[:KV_LORA] * (1.0 / acc[KV_LORA:KV_LORA + 1])
            o_ref[0, :, KV_LORA * hd:KV_LORA * (hd + 1)] = o_t.T.astype(BF16)

    visit_blocks(absorb_unscaled)
    runaway = jnp.max(top_scr[...] - m_scr[...]) > SOFTMAX_FAST_RANGE
    finish()

    @pl.when(runaway)
    def _():
        visit_blocks(absorb_all)
        finish()


def _mla_prompt_call(qcat, kcat, kvt, prefix, prev, *, qt, j0, steps, dummy):
    b, _, t, _ = qcat.shape
    per_block = KEY_BLOCK // qt
    in_specs = [pl.BlockSpec((1, MLA_HEADS, qt, QCAT), lambda i, j: (i, 0, (j + j0) * per_block, 0)),
                pl.BlockSpec((1, t, QCAT), lambda i, j: (i, 0, 0)),
                pl.BlockSpec((1, VT_ROWS, t), lambda i, j: (i, 0, 0))]
    args = (qcat, kcat, kvt)
    if prefix is not None:
        in_specs += [_const_spec(a.shape) for a in prefix]
        args += tuple(prefix)
    if prev is not None:
        in_specs.append(pl.BlockSpec(memory_space=pl.ANY))
        args += (prev,)
    return pl.pallas_call(
        functools.partial(_mla_prompt_body, qt=qt, j0=j0, aliased=prev is not None, dummy=dummy,
                          prefix=prefix is not None),
        grid=(b, steps),
        in_specs=in_specs,
        out_specs=pl.BlockSpec((1, qt, MLA_HEADS * KV_LORA), lambda i, j: (i, (j + j0) * per_block, 0)),
        out_shape=jax.ShapeDtypeStruct((b, t, MLA_HEADS * KV_LORA), BF16),
        input_output_aliases={len(args) - 1: 0} if prev is not None else {},
        scratch_shapes=[pltpu.VMEM((1, MLA_HEADS * qt), F32), pltpu.VMEM((1, MLA_HEADS * qt), F32),
                        pltpu.VMEM((VT_ROWS, MLA_HEADS * qt), F32)],
        compiler_params=pltpu.CompilerParams(dimension_semantics=("parallel", "parallel"),
                                             vmem_limit_bytes=VMEM_LIMIT),
        name="mla_prompt",
    )(*args)


def _mla_prompt(qcat, kcat, kvt, prefix, *, lp, dummy):
    t = qcat.shape[2]
    assert t % KEY_BLOCK == 0 and lp <= t
    full, rest = divmod(lp, KEY_BLOCK)
    if rest > LANES:
        full, rest = full + 1, 0
    out = None
    if full:
        out = _mla_prompt_call(qcat, kcat, kvt, prefix, None, qt=KEY_BLOCK, j0=0, steps=full, dummy=dummy)
    if rest:
        out = _mla_prompt_call(qcat, kcat, kvt, prefix, out, qt=LANES, j0=full, steps=1, dummy=dummy)
    return out


def _mla_sample_body(q_ref, k_ref, ckv_ref, cpe_ref, o_ref, *, ts):
    rows = MLA_HEADS * ts
    q = q_ref[0].reshape(rows, QCAT)
    ckv = ckv_ref[0].astype(BF16)
    knew = k_ref[0]
    s_c = _mm_nt(q[:, :KV_LORA], ckv) + _mm_nt(q[:, KV_LORA:KV_LORA + QK_ROPE], cpe_ref[0])
    s_n = _mm_nt(q, knew)
    m = jnp.maximum(jnp.max(s_c, axis=-1, keepdims=True), jnp.max(s_n, axis=-1, keepdims=True))
    p_c = jnp.exp2(s_c - m)
    p_n = jnp.exp2(s_n - m)
    l = jnp.sum(p_c, axis=-1, keepdims=True) + jnp.sum(p_n, axis=-1, keepdims=True)
    o = (_mm(p_c, ckv) + _mm(p_n, knew[:, :KV_LORA])) / l
    for hd in range(MLA_HEADS):
        o_ref[0, :, KV_LORA * hd:KV_LORA * (hd + 1)] = o[ts * hd:ts * (hd + 1)].astype(BF16)


def _mla_sample(qcat, kcat, cache_kv, cache_pe):
    b, _, ts, _ = qcat.shape
    past = cache_kv.shape[1]
    return pl.pallas_call(
        functools.partial(_mla_sample_body, ts=ts),
        grid=(b,),
        in_specs=[pl.BlockSpec((1, MLA_HEADS, ts, QCAT), lambda i: (i, 0, 0, 0)),
                  pl.BlockSpec((1, ts, QCAT), lambda i: (i, 0, 0)),
                  pl.BlockSpec((1, past, KV_LORA), lambda i: (i, 0, 0)),
                  pl.BlockSpec((1, past, QK_ROPE), lambda i: (i, 0, 0))],
        out_specs=pl.BlockSpec((1, ts, MLA_HEADS * KV_LORA), lambda i: (i, 0, 0)),
        out_shape=jax.ShapeDtypeStruct((b, ts, MLA_HEADS * KV_LORA), BF16),
        compiler_params=pltpu.CompilerParams(dimension_semantics=("parallel",),
                                             vmem_limit_bytes=VMEM_LIMIT),
        name="mla_sample",
    )(qcat, kcat, cache_kv, cache_pe)


def _hgrn_body(zh_ref, s0_ref, ob_ref, sout_ref, st_ref, g_scr, q_scr, k_scr, v_scr, o_scr, oi_scr, *, bb, cs):
    c = pl.program_id(1)
    nc = pl.num_programs(1)
    streams = range(bb)
    heads = [slice(HG_DK * hd, HG_DK * (hd + 1)) for hd in range(HG_HEADS)]

    @pl.when(c == 0)
    def _():
        for bi in streams:
            for hd in range(HG_HEADS):
                st_ref[bi, hd] = s0_ref[bi, hd].T

    q = zh_ref[:, :, :HG_KW]
    gl = zh_ref[:, :, HG_KW:2 * HG_KW]
    k = zh_ref[:, :, 2 * HG_KW:3 * HG_KW]
    v = zh_ref[:, :, 3 * HG_KW:3 * HG_KW + HG_WIDTH]
    t_idx = lax.broadcasted_iota(jnp.int32, (cs, cs), 0)
    s_idx = lax.broadcasted_iota(jnp.int32, (cs, cs), 1)
    causal = t_idx >= s_idx
    row = lax.broadcasted_iota(jnp.int32, (cs, 1), 0)
    g = [gl[bi] for bi in streams]
    shift = 1
    while shift < cs:
        g = [x + jnp.where(row >= shift, pltpu.roll(x, shift, 0), 0.0) for x in g]
        shift *= 2
    g = jnp.stack(g)
    g_tot = g[:, cs - 1:cs, :]
    any_out_of_range = jnp.max(-g_tot) > HGRN_FAST_RANGE
    qt = q * jnp.exp(g)
    khat = k * jnp.exp(g_tot - g)
    kt = k * jnp.exp(-g)
    a = [[jnp.where(causal, _mm_nt(qt[bi][:, sl], kt[bi][:, sl]), 0.0) for sl in heads] for bi in streams]
    for bi in streams:
        o_scr[bi] = jnp.concatenate([_mm(a[bi][hd], v[bi][:, sl]) for hd, sl in enumerate(heads)], axis=-1)

    o_inter = [[_mm_nt(qt[bi][:, sl], st_ref[bi, hd]) for hd, sl in enumerate(heads)] for bi in streams]
    kv_new = [[_mm_tn(v[bi][:, sl], khat[bi][:, sl]) for sl in heads] for bi in streams]
    decay = jnp.exp(g_tot)
    for bi in streams:
        oi_scr[bi] = jnp.concatenate(o_inter[bi], axis=-1)
        for hd, sl in enumerate(heads):
            st_ref[bi, hd] = decay[bi][:, sl] * st_ref[bi, hd] + kv_new[bi][hd]

    @pl.when(any_out_of_range)
    def _():
        for bi in streams:
            @pl.when(jnp.max(-g_tot[bi]) > HGRN_FAST_RANGE)
            def _(bi=bi):
                g_scr[...] = g[bi]
                q_scr[...] = q[bi]
                k_scr[...] = k[bi]
                v_scr[...] = v[bi]
                o_scr[bi] = jnp.zeros((cs, HG_WIDTH), F32)
                t_col = lax.broadcasted_iota(jnp.int32, (cs, 1), 0)

                def cols(i, carry):
                    base = pl.multiple_of(i * 8, 8)
                    g8, k8, v8 = (ref[pl.ds(base, 8), :] for ref in (g_scr, k_scr, v_scr))
                    for j in range(8):
                        dec = jnp.exp(jnp.where(t_col >= base + j, g_scr[...] - g8[j:j + 1], -jnp.inf))
                        pr = q_scr[...] * dec * k8[j:j + 1]
                        for sl in heads:
                            o_scr[bi, :, sl] += jnp.sum(pr[:, sl], axis=-1, keepdims=True) * v8[j:j + 1, sl]
                    return carry

                lax.fori_loop(0, cs // 8, cols, 0)

    for bi in streams:
        ob_ref[bi] = o_scr[bi] + oi_scr[bi]

    @pl.when(c == nc - 1)
    def _():
        for bi in streams:
            for hd in range(HG_HEADS):
                sout_ref[bi, hd] = st_ref[bi, hd].T


def _hgrn(zh, s0, *, n_rows, cs):
    b, t, zw = zh.shape
    assert n_rows % cs == 0 and n_rows <= t
    bb = HGRN_STREAMS if b % HGRN_STREAMS == 0 else 1
    return pl.pallas_call(
        functools.partial(_hgrn_body, bb=bb, cs=cs),
        grid=(b // bb, n_rows // cs),
        in_specs=[pl.BlockSpec((bb, cs, zw - HG_WIDTH), lambda i, c: (i, c, 0)),
                  pl.BlockSpec((bb, HG_HEADS, HG_DK, HG_DV), lambda i, c: (i, 0, 0, 0))],
        out_specs=(pl.BlockSpec((bb, cs, HG_WIDTH), lambda i, c: (i, c, 0)),
                   pl.BlockSpec((bb, HG_HEADS, HG_DK, HG_DV), lambda i, c: (i, 0, 0, 0))),
        out_shape=(jax.ShapeDtypeStruct((b, t, HG_WIDTH), F32),
                   jax.ShapeDtypeStruct((b, HG_HEADS, HG_DK, HG_DV), F32)),
        scratch_shapes=[pltpu.VMEM((bb, HG_HEADS, HG_DV, HG_DK), F32)]
        + [pltpu.VMEM((cs, HG_KW), F32) for _ in range(4)] + [pltpu.VMEM((bb, cs, HG_KW), F32) for _ in range(2)],
        compiler_params=pltpu.CompilerParams(dimension_semantics=("parallel", "arbitrary"),
                                             vmem_limit_bytes=VMEM_LIMIT),
        name="hgrn",
    )(zh, s0)


def _out_ffn_body(x_ref, ol_ref, ob_ref, gate_ref, buf_ref, gain_ref, wuv_ref, wout_ref, npost_ref, nfpre_ref,
                  wup_ref, cw_ref, cb_ref, wdn_ref, nfpost_ref, fin_ref, xo_ref, bufo_ref, a_scr,
                  *, bb, tm, pad_front, last, flat):
    t = pl.program_id(1)
    nt = pl.num_programs(1)
    d_ff = cw_ref.shape[1]
    keep = CONV_W - 1
    streams = range(bb)

    @pl.when(t == 0)
    def _():
        for bi in streams:
            a_scr[bi, 0:8, :] = jnp.zeros((8, d_ff), F32)
            a_scr[bi, 8 - keep:8, :] = buf_ref[bi]

    def mm_streams(xs, w):
        if not flat:
            return [_mm(x, w) for x in xs]
        out = _mm(jnp.concatenate(xs, axis=0), w)
        return [out[tm * bi:tm * (bi + 1)] for bi in streams]

    ol = [ol_ref[bi] for bi in streams]
    o_a = [mm_streams([o[:, 2 * KV_LORA * j:2 * KV_LORA * (j + 1)] for o in ol], wuv_ref[j])
           for j in range(MLA_HEADS // 2)]
    mixed = []
    for bi in streams:
        ob = ob_ref[bi]
        o_b = [_rms(ob[:, HG_DV * hd:HG_DV * (hd + 1)], gain_ref[...]) for hd in range(HG_HEADS)]
        o_b = (jnp.concatenate(o_b, axis=-1) * _silu(gate_ref[bi])).astype(BF16)
        mixed.append(jnp.concatenate([o[bi].astype(BF16) for o in o_a] + [o_b], axis=-1))
    y = mm_streams(mixed, wout_ref[...])
    x1 = [x_ref[bi] + _rms(y[bi], npost_ref[...]) for bi in streams]
    u = mm_streams([_rms(x1[bi], nfpre_ref[...]).astype(BF16) for bi in streams], wup_ref[...])
    cw = cw_ref[...]
    if pad_front:
        live = (t * tm + lax.broadcasted_iota(jnp.int32, (tm, 1), 0)) >= pad_front
    gated = []
    for bi in streams:
        a = u[bi][:, :d_ff]
        if pad_front:
            a = jnp.where(live, a, 0.0)
        a_scr[bi, 8:8 + tm, :] = a
        conv = cb_ref[...] + a * cw[keep:keep + 1]
        for j in range(keep):
            conv = conv + a_scr[bi, 8 - keep + j:8 - keep + j + tm, :] * cw[j:j + 1]
        gated.append((_silu(conv) * u[bi][:, d_ff:]).astype(BF16))
    f = mm_streams(gated, wdn_ref[...])
    for bi in streams:
        x2 = x1[bi] + _rms(f[bi], nfpost_ref[...])
        xo_ref[bi] = _rms(x2, fin_ref[...]) if last else x2
        a_scr[bi, 0:8, :] = a_scr[bi, tm:tm + 8, :]

    @pl.when(t == nt - 1)
    def _():
        for bi in streams:
            bufo_ref[bi] = a_scr[bi, 8 - keep:8, :]


def _out_ffn(x, ol, ob, zh, buf, w, final_norm, *, n_rows, pad_front, last):
    b, t, d = x.shape
    d_ff = buf.shape[-1]
    tm = _row_tile(n_rows, FFN_ROWS)
    assert tm >= 8 and n_rows <= t and zh.shape[-1] % HG_WIDTH == 0
    flat = tm < LANES
    bb = b if flat else (FFN_STREAMS if b % FFN_STREAMS == 0 else 1)
    row = lambda width: pl.BlockSpec((bb, tm, width), lambda i, j: (i, j, 0))
    gate =pl.BlockSpec((bb, tm, HG_WIDTH), lambda i, j: (i, j, zh.shape[-1] // HG_WIDTH - 1))
    per_b = pl.BlockSpec((bb, CONV_W - 1, d_ff), lambda i, j: (i, 0, 0))
    weights = (w['hg_gain'], w['w_uv'], w['w_out'], w['norm_mix_post'], w['norm_ffn_pre'], w['w_ffn_up'],
               w['ffn_conv_w'], w['ffn_conv_b'], w['w_ffn_down'], w['norm_ffn_post'], final_norm)
    return pl.pallas_call(
        functools.partial(_out_ffn_body, bb=bb, tm=tm, pad_front=pad_front, last=last, flat=flat),
        grid=(b // bb, n_rows // tm),
        in_specs=[row(d), row(MLA_HEADS * KV_LORA), row(HG_WIDTH), gate, per_b]
        + [_const_spec(a.shape) for a in weights],
        out_specs=(row(d), per_b),
        out_shape=(jax.ShapeDtypeStruct((b, t, d), F32), jax.ShapeDtypeStruct((b, CONV_W - 1, d_ff), F32)),
        scratch_shapes=[pltpu.VMEM((bb, tm + 8, d_ff), F32)],
        compiler_params=pltpu.CompilerParams(dimension_semantics=("parallel", "arbitrary"),
                                             vmem_limit_bytes=VMEM_LIMIT),
        name="out_ffn",
    )(x, ol, ob, zh, buf, *weights)


def _rope_tables(pos):
    half = QK_ROPE // 2
    inv = ROPE_THETA ** (-jnp.arange(half, dtype=F32) / half)
    ang = pos.astype(F32)[:, None] * inv
    cos, sin = jnp.cos(ang), jnp.sin(ang)
    zero = jnp.zeros_like(cos)
    pad = jnp.zeros((pos.shape[0], LANES - QK_ROPE), F32)
    return (jnp.concatenate([cos, cos, pad], axis=1),
            jnp.concatenate([-sin, zero, pad], axis=1),
            jnp.concatenate([zero, sin, pad], axis=1),
            jnp.tile(cos, (1, MLA_HEADS)), jnp.tile(sin, (1, MLA_HEADS)))


def _layer_weights(l, w_in, norm_mix_pre, norm_mix_post, q_norm, kv_norm, w_uq, w_uk, w_uv, hg_out_norm, w_out,
                   norm_ffn_pre, norm_ffn_post, w_ffn_up, ffn_conv_w, ffn_conv_b, w_ffn_down):
    d = w_in.shape[1]
    head_w = Q_LORA + KV_LORA + QK_ROPE
    w_in_r = jnp.concatenate([w_in[l][:, :head_w], jnp.zeros((d, LANES - QK_ROPE), F32), w_in[l][:, head_w:]], axis=1)
    uq = w_uq[l].reshape(Q_LORA, MLA_HEADS, QK_NOPE + QK_ROPE)
    half = QK_ROPE // 2
    w_uq_r = jnp.concatenate([uq[:, :, :QK_NOPE].reshape(Q_LORA, -1),
                              uq[:, :, QK_NOPE:QK_NOPE + half].reshape(Q_LORA, -1),
                              uq[:, :, QK_NOPE + half:].reshape(Q_LORA, -1)], axis=1)
    src = jnp.arange(2 * MLA_HEADS * half)
    dst = (src % (MLA_HEADS * half)) // half * LANES + src // (MLA_HEADS * half) * half + src % half
    pe_spread = jnp.zeros((2 * MLA_HEADS * half, MLA_HEADS * LANES), F32).at[src, dst].set(1.0)
    uk = jnp.transpose(w_uk[l], (1, 2, 0))
    uk_pair = jnp.zeros((MLA_HEADS // 2, 2 * QK_NOPE, 2 * KV_LORA), F32)
    uk_pair = uk_pair.at[:, :QK_NOPE, :KV_LORA].set(uk[0::2]).at[:, QK_NOPE:, KV_LORA:].set(uk[1::2])
    uv = jnp.transpose(w_uv[l], (1, 0, 2))
    uv_pair = jnp.zeros((MLA_HEADS // 2, 2 * KV_LORA, 2 * V_HEAD), F32)
    uv_pair = uv_pair.at[:, :KV_LORA, :V_HEAD].set(uv[0::2]).at[:, KV_LORA:, V_HEAD:].set(uv[1::2])
    row = lambda a: a.reshape(1, -1)
    return {
        'w_in': w_in_r.astype(BF16), 'norm_mix_pre': row(norm_mix_pre[l]), 'norm_mix_post': row(norm_mix_post[l]),
        'q_norm': row(q_norm[l]), 'kv_norm': row(kv_norm[l]), 'w_uq': w_uq_r.astype(BF16),
        'w_uk': uk_pair.astype(BF16), 'w_uv': uv_pair.astype(BF16), 'pe_spread': pe_spread.astype(BF16),
        'hg_gain': row(hg_out_norm[l]), 'w_out': w_out[l].astype(BF16),
        'norm_ffn_pre': row(norm_ffn_pre[l]), 'norm_ffn_post': row(norm_ffn_post[l]),
        'w_ffn_up': w_ffn_up[l].astype(BF16), 'ffn_conv_w': ffn_conv_w[l], 'ffn_conv_b': row(ffn_conv_b[l]),
        'w_ffn_down': w_ffn_down[l].astype(BF16),
    }


def kernel(x_prompt, x_sample, cache_kv_latent, cache_k_rope, state_hgrn, state_ffn_conv, meta_tokens, w_in, norm_mix_pre, norm_mix_post, q_norm, kv_norm, w_uq, w_uk, w_uv, hg_lower_bounds, hg_out_norm, w_out, norm_ffn_pre, norm_ffn_post, w_ffn_up, ffn_conv_w, ffn_conv_b, w_ffn_down, final_norm):
    depth = w_in.shape[0]
    b, seq, d = x_prompt.shape
    bs, ts, _ = x_sample.shape
    past = cache_kv_latent.shape[2]
    d_ff = ffn_conv_w.shape[-1]
    assert seq % CHUNK == 0

    lb_sm = jax.nn.softmax(hg_lower_bounds.astype(F32), axis=0)
    lb_all = jnp.cumsum(lb_sm, axis=0) - lb_sm[0]

    unit = IN_PROJ_ROWS * KEY_BLOCK // 128
    lm = PAD_FRONT + N_META
    lm_alloc = -(-lm // unit) * unit
    xm = jnp.concatenate([jnp.zeros((1, PAD_FRONT, d), F32), meta_tokens[None],
                          jnp.zeros((1, lm_alloc - lm, d), F32)], axis=1)
    tabs_m = _rope_tables(jnp.arange(lm_alloc, dtype=jnp.int32) - lm)
    seq_alloc = -(-seq // KEY_BLOCK) * KEY_BLOCK
    xp = x_prompt if seq_alloc == seq else jnp.pad(x_prompt, ((0, 0), (0, seq_alloc - seq), (0, 0)))
    tabs_p = _rope_tables(jnp.arange(seq_alloc, dtype=jnp.int32))
    xs = x_sample
    tabs_s = tuple(jnp.tile(a, (bs, 1)) for a in _rope_tables(past + jnp.arange(ts, dtype=jnp.int32)))
    fin = final_norm.reshape(1, -1)
    s0_m = jnp.zeros((1, HG_HEADS, HG_DK, HG_DV), F32)
    buf0_m = jnp.zeros((1, CONV_W - 1, d_ff), F32)
    meta_rows = slice(PAD_FRONT, lm)

    outs = [[] for _ in range(8)]
    for l in range(depth):
        w = _layer_weights(l, w_in, norm_mix_pre, norm_mix_post, q_norm, kv_norm, w_uq, w_uk, w_uv, hg_out_norm,
                           w_out, norm_ffn_pre, norm_ffn_post, w_ffn_up, ffn_conv_w, ffn_conv_b, w_ffn_down)
        w['hg_lb'] = lb_all[l].reshape(1, -1)
        last = l == depth - 1

        qcat, kcat, kvt, kv_m, pe_m, zh = _in_proj(xm, tabs_m, w, n_valid=lm, pad_front=PAD_FRONT)
        ol = _mla_prompt(qcat, kcat, kvt, None, lp=lm, dummy=PAD_FRONT)
        ob, s_m = _hgrn(zh, s0_m, n_rows=lm, cs=CHUNK)
        xm, buf_m = _out_ffn(xm, ol, ob, zh, buf0_m, w, fin, n_rows=lm, pad_front=PAD_FRONT, last=last)
        prefix = (kcat[0, meta_rows], kvt[0, :, meta_rows])

        qcat, kcat, kvt, kv, pe, zh = _in_proj(xp, tabs_p, w, n_valid=seq, pad_front=0)
        ol = _mla_prompt(qcat, kcat, kvt, prefix, lp=seq, dummy=0)
        ob, s_new = _hgrn(zh, jnp.broadcast_to(s_m, (b,) + s_m.shape[1:]), n_rows=seq, cs=CHUNK)
        xp, buf = _out_ffn(xp, ol, ob, zh, jnp.broadcast_to(buf_m, (b,) + buf_m.shape[1:]), w, fin, n_rows=seq,
                           pad_front=0, last=last)
        with_meta = lambda m, p: jnp.concatenate(
            [jnp.broadcast_to(m[:, meta_rows], (b, N_META, m.shape[-1])), p[:, :seq]], axis=1)
        for lst, a in zip(outs[:4], (with_meta(kv_m, kv), with_meta(pe_m, pe), s_new, buf)):
            lst.append(a)

        qcat, kcat, _, kv, pe, zh = _in_proj(xs.reshape(1, bs * ts, d), tabs_s, w, n_valid=bs * ts, pad_front=0)
        qcat = jnp.swapaxes(qcat.reshape(MLA_HEADS, bs, ts, QCAT), 0, 1)
        kcat, kv, pe, zh = (a.reshape(bs, ts, -1) for a in (kcat, kv, pe, zh))
        ol = _mla_sample(qcat, kcat, cache_kv_latent[l], cache_k_rope[l])
        ob, s_new = _hgrn(zh, state_hgrn[l], n_rows=ts, cs=ts)
        xs, buf = _out_ffn(xs, ol, ob, zh, state_ffn_conv[l], w, fin, n_rows=ts, pad_front=0, last=last)
        for lst, a in zip(outs[4:], (kv, pe, s_new, buf)):
            lst.append(a)

    y_prompt = xp if seq_alloc == seq else xp[:, :seq]
    return (y_prompt, xs) + tuple(jnp.stack(o) for o in outs)
```

```python
import functools

import jax
import jax.numpy as jnp
from jax import lax
from jax.experimental import pallas as pl
from jax.experimental.pallas import tpu as pltpu

F32 = jnp.float32
BF16 = jnp.bfloat16

CHUNK = 64
N_META = 16
PAD_FRONT = CHUNK - N_META
EPS = 1e-6
MLA_HEADS = 8
QK_NOPE = 64
QK_ROPE = 32
V_HEAD = 64
Q_LORA = 256
KV_LORA = 128
ROPE_THETA = 10000.0
HG_HEADS = 4
HG_DK = 128
HG_DV = 128
CONV_W = 3

HG_KW = HG_HEADS * HG_DK
HG_WIDTH = HG_HEADS * HG_DV
LANES = 128
KEY_BLOCK = 256
QCAT = KV_LORA + LANES
VT_ROWS = KV_LORA + 16
SOFTMAX_FAST_RANGE = 100.0
PREFIX_KEYS = N_META
IN_PROJ_ROWS = 256
IN_PROJ_STREAMS = 2
Q_SCALE = (QK_NOPE + QK_ROPE) ** -0.5 * 1.4426950408889634
HGRN_FAST_RANGE = 80.0
FFN_ROWS = 256
FFN_STREAMS = 2
HGRN_STREAMS = 8
V7X_VMEM_BYTES = 64 * 1024 * 1024
VMEM_LIMIT = V7X_VMEM_BYTES * 7 // 8


def _row_tile(n, target):
    if n <= target:
        return n
    best = None
    for t in range(16, target + 1, 16):
        if n % t == 0:
            best = t
    assert best is not None, (n, target)
    return best


def _const_spec(shape):
    nd = len(shape)
    return pl.BlockSpec(shape, lambda *_: (0,) * nd, pipeline_mode=pl.Buffered(1))


def _rms(x, w):
    return x * lax.rsqrt(jnp.mean(x * x, axis=-1, keepdims=True) + EPS) * w


def _silu(x):
    return x * jax.nn.sigmoid(x)


def _mm(a, b):
    return jnp.dot(a.astype(BF16), b.astype(BF16), preferred_element_type=F32)


def _mm_nt(a, b):
    return lax.dot_general(a.astype(BF16), b.astype(BF16), (((1,), (1,)), ((), ())),
                           preferred_element_type=F32)


def _mm_tn(a, b):
    return lax.dot_general(a.astype(BF16), b.astype(BF16), (((0,), (0,)), ((), ())),
                           preferred_element_type=F32)


def _in_proj_body(x_ref, cos_ref, sn_ref, sp_ref, cosq_ref, sinq_ref, npre_ref, win_ref, qn_ref, kvn_ref, wuq_ref,
                  wuk_ref, spread_ref, lb_ref, qcat_ref, kcat_ref, kvt_ref, kv_ref, pe_ref, zh_ref,
                  *, bb, tm, n_valid, pad_front):
    cos, sn, sp = cos_ref[...], sn_ref[...], sp_ref[...]
    lb = lb_ref[...]

    def rope(v):
        return v * cos + pltpu.roll(v, LANES - QK_ROPE // 2, 1) * sn + pltpu.roll(v, QK_ROPE // 2, 1) * sp

    row = pl.program_id(1) * tm + lax.broadcasted_iota(jnp.int32, (tm, 1), 0)
    live = row < n_valid
    real = row >= pad_front
    h = [_rms(jnp.where(live, x_ref[bi], 0.0), npre_ref[...]) for bi in range(bb)]
    zs = [_mm(h[bi], win_ref[...]) for bi in range(bb)]
    nope_w = MLA_HEADS * QK_NOPE
    for bi, z in enumerate(zs):
        off = Q_LORA + KV_LORA + LANES
        hf = z[:, off + HG_KW:off + 2 * HG_KW]
        e = jnp.exp(-jnp.abs(hf))
        r = 1.0 / (1.0 + e)
        f = lb + (1.0 - lb) * jnp.where(hf >= 0.0, r, e * r)
        zh_ref[bi, :, :HG_KW] = jnp.where(real, _silu(z[:, off:off + HG_KW]), 0.0)
        zh_ref[bi, :, HG_KW:2 * HG_KW] = jnp.where(real, jnp.log(f), 0.0)
        zh_ref[bi, :, 2 * HG_KW:3 * HG_KW] = jnp.where(real, (1.0 - lb) * jnp.where(hf >= 0.0, e * r, r), 0.0)
        zh_ref[bi, :, 3 * HG_KW:] = z[:, off + 2 * HG_KW:]
        kv_lat = _rms(z[:, Q_LORA:Q_LORA + KV_LORA], kvn_ref[...])
        k_pe = rope(z[:, Q_LORA + KV_LORA:Q_LORA + KV_LORA + LANES])
        kv_ref[bi] = kv_lat
        pe_ref[bi] = k_pe[:, :QK_ROPE]
        kcat_ref[bi, :, :KV_LORA] = kv_lat.astype(BF16)
        kcat_ref[bi, :, KV_LORA:] = k_pe.astype(BF16)
        kvt_ref[bi, :KV_LORA, :] = kv_lat.T.astype(BF16)
        kvt_ref[bi, KV_LORA:, :] = jnp.ones((VT_ROWS - KV_LORA, tm), BF16)

        cqn = _rms(z[:, :Q_LORA], qn_ref[...])
        q = _mm(cqn, wuq_ref[...]) * Q_SCALE
        x1, x2 = q[:, nope_w:nope_w + LANES], q[:, nope_w + LANES:]
        cq, sq = cosq_ref[...], sinq_ref[...]
        roped = jnp.concatenate([x1 * cq - x2 * sq, x1 * sq + x2 * cq], axis=1)
        pe_heads = _mm(roped, spread_ref[...])
        for j in range(MLA_HEADS // 2):
            ql = _mm(q[:, 2 * QK_NOPE * j:2 * QK_NOPE * (j + 1)], wuk_ref[j])
            for hh in range(2):
                hd = 2 * j + hh
                qcat_ref[bi, hd, :, :KV_LORA] = ql[:, KV_LORA * hh:KV_LORA * (hh + 1)].astype(BF16)
                qcat_ref[bi, hd, :, KV_LORA:] = pe_heads[:, LANES * hd:LANES * (hd + 1)].astype(BF16)


def _in_proj(x, tabs, w, *, n_valid, pad_front):
    bx, t, d = x.shape
    tm = _row_tile(t, IN_PROJ_ROWS)
    assert tm % LANES == 0 or tm == t
    nt = t // tm
    bb = IN_PROJ_STREAMS if bx % IN_PROJ_STREAMS == 0 else 1
    row = lambda shape: pl.BlockSpec((bb, tm) + shape, lambda b, i: (b, i) + (0,) * len(shape))
    tab = pl.BlockSpec((tm, LANES), lambda b, i: (i, 0))
    zw = 3 * HG_KW + 2 * HG_WIDTH
    out_shape = (
        jax.ShapeDtypeStruct((bx, MLA_HEADS, t, QCAT), BF16),
        jax.ShapeDtypeStruct((bx, t, QCAT), BF16),
        jax.ShapeDtypeStruct((bx, VT_ROWS, t), BF16),
        jax.ShapeDtypeStruct((bx, t, KV_LORA), F32),
        jax.ShapeDtypeStruct((bx, t, QK_ROPE), F32),
        jax.ShapeDtypeStruct((bx, t, zw), F32),
    )
    out_specs = (
        pl.BlockSpec((bb, MLA_HEADS, tm, QCAT), lambda b, i: (b, 0, i, 0)),
        row((QCAT,)),
        pl.BlockSpec((bb, VT_ROWS, tm), lambda b, i: (b, 0, i)),
        row((KV_LORA,)), row((QK_ROPE,)), row((zw,)),
    )
    weights = (w['norm_mix_pre'], w['w_in'], w['q_norm'], w['kv_norm'], w['w_uq'], w['w_uk'], w['pe_spread'],
               w['hg_lb'])
    return pl.pallas_call(
        functools.partial(_in_proj_body, bb=bb, tm=tm, n_valid=n_valid, pad_front=pad_front),
        grid=(bx // bb, nt),
        in_specs=[row((d,))] + [tab] * len(tabs) + [_const_spec(a.shape) for a in weights],
        out_specs=out_specs,
        out_shape=out_shape,
        compiler_params=pltpu.CompilerParams(dimension_semantics=("parallel", "parallel"),
                                             vmem_limit_bytes=VMEM_LIMIT),
        name="in_proj",
    )(x, *tabs, *weights)


def _mla_prompt_body(*refs, qt, j0, aliased, dummy, prefix):
    q_ref, k_ref, vt_ref = refs[:3]
    kpre_ref, vtpre_ref = refs[3:5] if prefix else (None, None)
    o_ref, m_scr, top_scr, acc_scr = refs[-4:]
    j = pl.program_id(1) + j0
    width = MLA_HEADS * qt
    pre = PREFIX_KEYS if prefix else 0
    lower_half = lax.broadcasted_iota(jnp.int32, (CHUNK, LANES), 1) < CHUNK

    def hidden(rows, cols=LANES):
        return jnp.full((rows, cols), -jnp.inf, F32)

    def mask_dummy(s):
        return jnp.concatenate([hidden(dummy, s.shape[1]), s[dummy:]], axis=0) if dummy else s

    def mask_diagonal(s):
        strips = []
        for c0 in range(0, width, LANES):
            rows = [s[:pre, c0:c0 + LANES]] if prefix else []
            t = s[pre:, c0:c0 + LANES]
            if (c0 % qt) // LANES == 0:
                rows += [t[:CHUNK], jnp.where(lower_half, -jnp.inf, t[CHUNK:2 * CHUNK]), hidden(2 * CHUNK)]
            else:
                rows += [t[:3 * CHUNK], jnp.where(lower_half, -jnp.inf, t[3 * CHUNK:])]
            strips.append(jnp.concatenate(rows, axis=0))
        return jnp.concatenate(strips, axis=1)

    def keys_of(kb):
        start = kb * KEY_BLOCK
        return pl.ds(start if isinstance(kb, int) else pl.multiple_of(start, KEY_BLOCK), KEY_BLOCK)

    def scores(kb, diagonal=False):
        keys = k_ref[0, keys_of(kb), :]
        if diagonal and prefix:
            keys = jnp.concatenate([kpre_ref[...], keys], axis=0)
        return _mm_nt(keys, q_ref[0].reshape(width, QCAT))

    def weighted_values(p, kb, diagonal):
        if diagonal and prefix:
            return _mm(vtpre_ref[...], p[:pre]) + _mm(vt_ref[0, :, keys_of(kb)], p[pre:])
        return _mm(vt_ref[0, :, keys_of(kb)], p)

    def absorb(s, kb, diagonal=False):
        if diagonal:
            s = mask_diagonal(s)
        if isinstance(kb, int) and kb == 0:
            s = mask_dummy(s)
        for hd in range(MLA_HEADS):
            cols = slice(qt * hd, qt * (hd + 1))
            m_new = jnp.max(s[:, cols], axis=0, keepdims=True)
            if not diagonal:
                m_old = m_scr[:, cols]
                m_new = jnp.maximum(m_old, m_new)
            pv = weighted_values(jnp.exp2(s[:, cols] - m_new), kb, diagonal)
            acc_scr[:, cols] = pv if diagonal else jnp.exp2(m_old - m_new) * acc_scr[:, cols] + pv
            m_scr[:, cols] = m_new

    def absorb_all(blocks):
        for s, kb, diagonal in blocks:
            absorb(s, kb, diagonal)

    def absorb_unscaled(blocks):
        blocks = [(mask_diagonal(s) if diagonal else s, kb, diagonal) for s, kb, diagonal in blocks]
        blocks = [(mask_dummy(s) if isinstance(kb, int) and kb == 0 else s, kb, d) for s, kb, d in blocks]
        first = blocks[0][2]
        for hd in range(MLA_HEADS):
            cols = slice(qt * hd, qt * (hd + 1))
            tops = [jnp.max(s[:, cols], axis=0, keepdims=True) for s, _, _ in blocks]
            if first:
                shift = jnp.max(blocks[0][0][:PREFIX_KEYS, cols], axis=0, keepdims=True) if prefix else tops[0]
                m_scr[:, cols] = shift
                top_scr[:, cols] = functools.reduce(jnp.maximum, tops)
            else:
                shift = m_scr[:, cols]
                top_scr[:, cols] = functools.reduce(jnp.maximum, tops, top_scr[:, cols])
            pv = None
            for s, kb, diagonal in blocks:
                term = weighted_values(jnp.exp2(s[:, cols] - shift), kb, diagonal)
                pv = term if pv is None else pv + term
            acc_scr[:, cols] = pv if first else acc_scr[:, cols] + pv

    def visit_blocks(absorb_blocks):
        def pair(a, b, diagonal=False):
            s_a, s_b = scores(a, diagonal), scores(b)
            absorb_blocks([(s_a, a, diagonal), (s_b, b, False)])

        def absorb_block(s, kb, diagonal=False):
            absorb_blocks([(s, kb, diagonal)])

        @pl.when(j == 0)
        def _():
            absorb_block(scores(0, True), 0, True)

        n_blocks = k_ref.shape[1] // KEY_BLOCK
        if n_blocks < 2:
            return

        @pl.when(j == 1)
        def _():
            pair(1, 0, True)

        if n_blocks < 3:
            return

        @pl.when(j > 1)
        def _():
            pair(j, j - 1, True)
            n_inner = (j - 2) // 2

            def inner(p, carry):
                pair(j - 2 * p, j - 2 * p - 1)
                return carry

            lax.fori_loop(1, n_inner + 1, inner, 0)

            @pl.when(j % 2 == 0)
            def _():
                absorb_block(scores(0), 0)

            @pl.when(j % 2 == 1)
            def _():
                pair(1, 0)

    def finish():
        for hd in range(MLA_HEADS):
            acc = acc_scr[:, qt * hd:qt * (hd + 1)]
            o_t = acc[:KV_LORA] * (1.0 / acc[KV_LORA:KV_LORA + 1])
            o_ref[0, :, KV_LORA * hd:KV_LORA * (hd + 1)] = o_t.T.astype(BF16)

    visit_blocks(absorb_unscaled)
    runaway = jnp.max(top_scr[...] - m_scr[...]) > SOFTMAX_FAST_RANGE
    finish()

    @pl.when(runaway)
    def _():
        visit_blocks(absorb_all)
        finish()


def _mla_prompt_call(qcat, kcat, kvt, prefix, prev, *, qt, j0, steps, dummy):
    b, _, t, _ = qcat.shape
    per_block = KEY_BLOCK // qt
    in_specs = [pl.BlockSpec((1, MLA_HEADS, qt, QCAT), lambda i, j: (i, 0, (j + j0) * per_block, 0)),
                pl.BlockSpec((1, t, QCAT), lambda i, j: (i, 0, 0)),
                pl.BlockSpec((1, VT_ROWS, t), lambda i, j: (i, 0, 0))]
    args = (qcat, kcat, kvt)
    if prefix is not None:
        in_specs += [_const_spec(a.shape) for a in prefix]
        args += tuple(prefix)
    if prev is not None:
        in_specs.append(pl.BlockSpec(memory_space=pl.ANY))
        args += (prev,)
    return pl.pallas_call(
        functools.partial(_mla_prompt_body, qt=qt, j0=j0, aliased=prev is not None, dummy=dummy,
                          prefix=prefix is not None),
        grid=(b, steps),
        in_specs=in_specs,
        out_specs=pl.BlockSpec((1, qt, MLA_HEADS * KV_LORA), lambda i, j: (i, (j + j0) * per_block, 0)),
        out_shape=jax.ShapeDtypeStruct((b, t, MLA_HEADS * KV_LORA), BF16),
        input_output_aliases={len(args) - 1: 0} if prev is not None else {},
        scratch_shapes=[pltpu.VMEM((1, MLA_HEADS * qt), F32), pltpu.VMEM((1, MLA_HEADS * qt), F32),
                        pltpu.VMEM((VT_ROWS, MLA_HEADS * qt), F32)],
        compiler_params=pltpu.CompilerParams(dimension_semantics=("parallel", "parallel"),
                                             vmem_limit_bytes=VMEM_LIMIT),
        name="mla_prompt",
    )(*args)


def _mla_prompt(qcat, kcat, kvt, prefix, *, lp, dummy):
    t = qcat.shape[2]
    assert t % KEY_BLOCK == 0 and lp <= t
    full, rest = divmod(lp, KEY_BLOCK)
    if rest > LANES:
        full, rest = full + 1, 0
    out = None
    if full:
        out = _mla_prompt_call(qcat, kcat, kvt, prefix, None, qt=KEY_BLOCK, j0=0, steps=full, dummy=dummy)
    if rest:
        out = _mla_prompt_call(qcat, kcat, kvt, prefix, out, qt=LANES, j0=full, steps=1, dummy=dummy)
    return out


def _mla_sample_body(q_ref, k_ref, ckv_ref, cpe_ref, o_ref, *, ts):
    rows = MLA_HEADS * ts
    q = q_ref[0].reshape(rows, QCAT)
    ckv = ckv_ref[0].astype(BF16)
    knew = k_ref[0]
    s_c = _mm_nt(q[:, :KV_LORA], ckv) + _mm_nt(q[:, KV_LORA:KV_LORA + QK_ROPE], cpe_ref[0])
    s_n = _mm_nt(q, knew)
    m = jnp.maximum(jnp.max(s_c, axis=-1, keepdims=True), jnp.max(s_n, axis=-1, keepdims=True))
    p_c = jnp.exp2(s_c - m)
    p_n = jnp.exp2(s_n - m)
    l = jnp.sum(p_c, axis=-1, keepdims=True) + jnp.sum(p_n, axis=-1, keepdims=True)
    o = (_mm(p_c, ckv) + _mm(p_n, knew[:, :KV_LORA])) / l
    for hd in range(MLA_HEADS):
        o_ref[0, :, KV_LORA * hd:KV_LORA * (hd + 1)] = o[ts * hd:ts * (hd + 1)].astype(BF16)


def _mla_sample(qcat, kcat, cache_kv, cache_pe):
    b, _, ts, _ = qcat.shape
    past = cache_kv.shape[1]
    return pl.pallas_call(
        functools.partial(_mla_sample_body, ts=ts),
        grid=(b,),
        in_specs=[pl.BlockSpec((1, MLA_HEADS, ts, QCAT), lambda i: (i, 0, 0, 0)),
                  pl.BlockSpec((1, ts, QCAT), lambda i: (i, 0, 0)),
                  pl.BlockSpec((1, past, KV_LORA), lambda i: (i, 0, 0)),
                  pl.BlockSpec((1, past, QK_ROPE), lambda i: (i, 0, 0))],
        out_specs=pl.BlockSpec((1, ts, MLA_HEADS * KV_LORA), lambda i: (i, 0, 0)),
        out_shape=jax.ShapeDtypeStruct((b, ts, MLA_HEADS * KV_LORA), BF16),
        compiler_params=pltpu.CompilerParams(dimension_semantics=("parallel",),
                                             vmem_limit_bytes=VMEM_LIMIT),
        name="mla_sample",
    )(qcat, kcat, cache_kv, cache_pe)


def _hgrn_body(zh_ref, s0_ref, ob_ref, sout_ref, st_ref, g_scr, q_scr, k_scr, v_scr, o_scr, oi_scr, *, bb, cs):
    c = pl.program_id(1)
    nc = pl.num_programs(1)
    streams = range(bb)
    heads = [slice(HG_DK * hd, HG_DK * (hd + 1)) for hd in range(HG_HEADS)]

    @pl.when(c == 0)
    def _():
        for bi in streams:
            for hd in range(HG_HEADS):
                st_ref[bi, hd] = s0_ref[bi, hd].T

    q = zh_ref[:, :, :HG_KW]
    gl = zh_ref[:, :, HG_KW:2 * HG_KW]
    k = zh_ref[:, :, 2 * HG_KW:3 * HG_KW]
    v = zh_ref[:, :, 3 * HG_KW:3 * HG_KW + HG_WIDTH]
    t_idx = lax.broadcasted_iota(jnp.int32, (cs, cs), 0)
    s_idx = lax.broadcasted_iota(jnp.int32, (cs, cs), 1)
    causal = t_idx >= s_idx
    row = lax.broadcasted_iota(jnp.int32, (cs, 1), 0)
    g = [gl[bi] for bi in streams]
    shift = 1
    while shift < cs:
        g = [x + jnp.where(row >= shift, pltpu.roll(x, shift, 0), 0.0) for x in g]
        shift *= 2
    g = jnp.stack(g)
    g_tot = g[:, cs - 1:cs, :]
    any_out_of_range = jnp.max(-g_tot) > HGRN_FAST_RANGE
    qt = q * jnp.exp(g)
    khat = k * jnp.exp(g_tot - g)
    kt = k * jnp.exp(-g)
    a = [[jnp.where(causal, _mm_nt(qt[bi][:, sl], kt[bi][:, sl]), 0.0) for sl in heads] for bi in streams]
    for bi in streams:
        o_scr[bi] = jnp.concatenate([_mm(a[bi][hd], v[bi][:, sl]) for hd, sl in enumerate(heads)], axis=-1)

    o_inter = [[_mm_nt(qt[bi][:, sl], st_ref[bi, hd]) for hd, sl in enumerate(heads)] for bi in streams]
    kv_new = [[_mm_tn(v[bi][:, sl], khat[bi][:, sl]) for sl in heads] for bi in streams]
    decay = jnp.exp(g_tot)
    for bi in streams:
        oi_scr[bi] = jnp.concatenate(o_inter[bi], axis=-1)
        for hd, sl in enumerate(heads):
            st_ref[bi, hd] = decay[bi][:, sl] * st_ref[bi, hd] + kv_new[bi][hd]

    @pl.when(any_out_of_range)
    def _():
        for bi in streams:
            @pl.when(jnp.max(-g_tot[bi]) > HGRN_FAST_RANGE)
            def _(bi=bi):
                g_scr[...] = g[bi]
                q_scr[...] = q[bi]
                k_scr[...] = k[bi]
                v_scr[...] = v[bi]
                o_scr[bi] = jnp.zeros((cs, HG_WIDTH), F32)
                t_col = lax.broadcasted_iota(jnp.int32, (cs, 1), 0)

                def cols(i, carry):
                    base = pl.multiple_of(i * 8, 8)
                    g8, k8, v8 = (ref[pl.ds(base, 8), :] for ref in (g_scr, k_scr, v_scr))
                    for j in range(8):
                        dec = jnp.exp(jnp.where(t_col >= base + j, g_scr[...] - g8[j:j + 1], -jnp.inf))
                        pr = q_scr[...] * dec * k8[j:j + 1]
                        for sl in heads:
                            o_scr[bi, :, sl] += jnp.sum(pr[:, sl], axis=-1, keepdims=True) * v8[j:j + 1, sl]
                    return carry

                lax.fori_loop(0, cs // 8, cols, 0)

    for bi in streams:
        ob_ref[bi] = o_scr[bi] + oi_scr[bi]

    @pl.when(c == nc - 1)
    def _():
        for bi in streams:
            for hd in range(HG_HEADS):
                sout_ref[bi, hd] = st_ref[bi, hd].T


def _hgrn(zh, s0, *, n_rows, cs):
    b, t, zw = zh.shape
    assert n_rows % cs == 0 and n_rows <= t
    bb = HGRN_STREAMS if b % HGRN_STREAMS == 0 else 1
    return pl.pallas_call(
        functools.partial(_hgrn_body, bb=bb, cs=cs),
        grid=(b // bb, n_rows // cs),
        in_specs=[pl.BlockSpec((bb, cs, zw - HG_WIDTH), lambda i, c: (i, c, 0)),
                  pl.BlockSpec((bb, HG_HEADS, HG_DK, HG_DV), lambda i, c: (i, 0, 0, 0))],
        out_specs=(pl.BlockSpec((bb, cs, HG_WIDTH), lambda i, c: (i, c, 0)),
                   pl.BlockSpec((bb, HG_HEADS, HG_DK, HG_DV), lambda i, c: (i, 0, 0, 0))),
        out_shape=(jax.ShapeDtypeStruct((b, t, HG_WIDTH), F32),
                   jax.ShapeDtypeStruct((b, HG_HEADS, HG_DK, HG_DV), F32)),
        scratch_shapes=[pltpu.VMEM((bb, HG_HEADS, HG_DV, HG_DK), F32)]
        + [pltpu.VMEM((cs, HG_KW), F32) for _ in range(4)] + [pltpu.VMEM((bb, cs, HG_KW), F32) for _ in range(2)],
        compiler_params=pltpu.CompilerParams(dimension_semantics=("parallel", "arbitrary"),
                                             vmem_limit_bytes=VMEM_LIMIT),
        name="hgrn",
    )(zh, s0)


def _out_ffn_body(x_ref, ol_ref, ob_ref, gate_ref, buf_ref, gain_ref, wuv_ref, wout_ref, npost_ref, nfpre_ref,
                  wup_ref, cw_ref, cb_ref, wdn_ref, nfpost_ref, fin_ref, xo_ref, bufo_ref, a_scr,
                  *, bb, tm, pad_front, last, flat):
    t = pl.program_id(1)
    nt = pl.num_programs(1)
    d_ff = cw_ref.shape[1]
    keep = CONV_W - 1
    streams = range(bb)

    @pl.when(t == 0)
    def _():
        for bi in streams:
            a_scr[bi, 0:8, :] = jnp.zeros((8, d_ff), F32)
            a_scr[bi, 8 - keep:8, :] = buf_ref[bi]

    def mm_streams(xs, w):
        if not flat:
            return [_mm(x, w) for x in xs]
        out = _mm(jnp.concatenate(xs, axis=0), w)
        return [out[tm * bi:tm * (bi + 1)] for bi in streams]

    ol = [ol_ref[bi] for bi in streams]
    o_a = [mm_streams([o[:, 2 * KV_LORA * j:2 * KV_LORA * (j + 1)] for o in ol], wuv_ref[j])
           for j in range(MLA_HEADS // 2)]
    mixed = []
    for bi in streams:
        ob = ob_ref[bi]
        o_b = [_rms(ob[:, HG_DV * hd:HG_DV * (hd + 1)], gain_ref[...]) for hd in range(HG_HEADS)]
        o_b = (jnp.concatenate(o_b, axis=-1) * _silu(gate_ref[bi])).astype(BF16)
        mixed.append(jnp.concatenate([o[bi].astype(BF16) for o in o_a] + [o_b], axis=-1))
    y = mm_streams(mixed, wout_ref[...])
    x1 = [x_ref[bi] + _rms(y[bi], npost_ref[...]) for bi in streams]
    u = mm_streams([_rms(x1[bi], nfpre_ref[...]).astype(BF16) for bi in streams], wup_ref[...])
    cw = cw_ref[...]
    if pad_front:
        live = (t * tm + lax.broadcasted_iota(jnp.int32, (tm, 1), 0)) >= pad_front
    gated = []
    for bi in streams:
        a = u[bi][:, :d_ff]
        if pad_front:
            a = jnp.where(live, a, 0.0)
        a_scr[bi, 8:8 + tm, :] = a
        conv = cb_ref[...] + a * cw[keep:keep + 1]
        for j in range(keep):
            conv = conv + a_scr[bi, 8 - keep + j:8 - keep + j + tm, :] * cw[j:j + 1]
        gated.append((_silu(conv) * u[bi][:, d_ff:]).astype(BF16))
    f = mm_streams(gated, wdn_ref[...])
    for bi in streams:
        x2 = x1[bi] + _rms(f[bi], nfpost_ref[...])
        xo_ref[bi] = _rms(x2, fin_ref[...]) if last else x2
        a_scr[bi, 0:8, :] = a_scr[bi, tm:tm + 8, :]

    @pl.when(t == nt - 1)
    def _():
        for bi in streams:
            bufo_ref[bi] = a_scr[bi, 8 - keep:8, :]


def _out_ffn(x, ol, ob, zh, buf, w, final_norm, *, n_rows, pad_front, last):
    b, t, d = x.shape
    d_ff = buf.shape[-1]
    tm = _row_tile(n_rows, FFN_ROWS)
    assert tm >= 8 and n_rows <= t and zh.shape[-1] % HG_WIDTH == 0
    flat = tm < LANES
    bb = b if flat else (FFN_STREAMS if b % FFN_STREAMS == 0 else 1)
    row = lambda width: pl.BlockSpec((bb, tm, width), lambda i, j: (i, j, 0))
    gate =pl.BlockSpec((bb, tm, HG_WIDTH), lambda i, j: (i, j, zh.shape[-1] // HG_WIDTH - 1))
    per_b = pl.BlockSpec((bb, CONV_W - 1, d_ff), lambda i, j: (i, 0, 0))
    weights = (w['hg_gain'], w['w_uv'], w['w_out'], w['norm_mix_post'], w['norm_ffn_pre'], w['w_ffn_up'],
               w['ffn_conv_w'], w['ffn_conv_b'], w['w_ffn_down'], w['norm_ffn_post'], final_norm)
    return pl.pallas_call(
        functools.partial(_out_ffn_body, bb=bb, tm=tm, pad_front=pad_front, last=last, flat=flat),
        grid=(b // bb, n_rows // tm),
        in_specs=[row(d), row(MLA_HEADS * KV_LORA), row(HG_WIDTH), gate, per_b]
        + [_const_spec(a.shape) for a in weights],
        out_specs=(row(d), per_b),
        out_shape=(jax.ShapeDtypeStruct((b, t, d), F32), jax.ShapeDtypeStruct((b, CONV_W - 1, d_ff), F32)),
        scratch_shapes=[pltpu.VMEM((bb, tm + 8, d_ff), F32)],
        compiler_params=pltpu.CompilerParams(dimension_semantics=("parallel", "arbitrary"),
                                             vmem_limit_bytes=VMEM_LIMIT),
        name="out_ffn",
    )(x, ol, ob, zh, buf, *weights)


def _rope_tables(pos):
    half = QK_ROPE // 2
    inv = ROPE_THETA ** (-jnp.arange(half, dtype=F32) / half)
    ang = pos.astype(F32)[:, None] * inv
    cos, sin = jnp.cos(ang), jnp.sin(ang)
    zero = jnp.zeros_like(cos)
    pad = jnp.zeros((pos.shape[0], LANES - QK_ROPE), F32)
    return (jnp.concatenate([cos, cos, pad], axis=1),
            jnp.concatenate([-sin, zero, pad], axis=1),
            jnp.concatenate([zero, sin, pad], axis=1),
            jnp.tile(cos, (1, MLA_HEADS)), jnp.tile(sin, (1, MLA_HEADS)))


def _layer_weights(l, w_in, norm_mix_pre, norm_mix_post, q_norm, kv_norm, w_uq, w_uk, w_uv, hg_out_norm, w_out,
                   norm_ffn_pre, norm_ffn_post, w_ffn_up, ffn_conv_w, ffn_conv_b, w_ffn_down):
    d = w_in.shape[1]
    head_w = Q_LORA + KV_LORA + QK_ROPE
    w_in_r = jnp.concatenate([w_in[l][:, :head_w], jnp.zeros((d, LANES - QK_ROPE), F32), w_in[l][:, head_w:]], axis=1)
    uq = w_uq[l].reshape(Q_LORA, MLA_HEADS, QK_NOPE + QK_ROPE)
    half = QK_ROPE // 2
    w_uq_r = jnp.concatenate([uq[:, :, :QK_NOPE].reshape(Q_LORA, -1),
                              uq[:, :, QK_NOPE:QK_NOPE + half].reshape(Q_LORA, -1),
                              uq[:, :, QK_NOPE + half:].reshape(Q_LORA, -1)], axis=1)
    src = jnp.arange(2 * MLA_HEADS * half)
    dst = (src % (MLA_HEADS * half)) // half * LANES + src // (MLA_HEADS * half) * half + src % half
    pe_spread = jnp.zeros((2 * MLA_HEADS * half, MLA_HEADS * LANES), F32).at[src, dst].set(1.0)
    uk = jnp.transpose(w_uk[l], (1, 2, 0))
    uk_pair = jnp.zeros((MLA_HEADS // 2, 2 * QK_NOPE, 2 * KV_LORA), F32)
    uk_pair = uk_pair.at[:, :QK_NOPE, :KV_LORA].set(uk[0::2]).at[:, QK_NOPE:, KV_LORA:].set(uk[1::2])
    uv = jnp.transpose(w_uv[l], (1, 0, 2))
    uv_pair = jnp.zeros((MLA_HEADS // 2, 2 * KV_LORA, 2 * V_HEAD), F32)
    uv_pair = uv_pair.at[:, :KV_LORA, :V_HEAD].set(uv[0::2]).at[:, KV_LORA:, V_HEAD:].set(uv[1::2])
    row = lambda a: a.reshape(1, -1)
    return {
        'w_in': w_in_r.astype(BF16), 'norm_mix_pre': row(norm_mix_pre[l]), 'norm_mix_post': row(norm_mix_post[l]),
        'q_norm': row(q_norm[l]), 'kv_norm': row(kv_norm[l]), 'w_uq': w_uq_r.astype(BF16),
        'w_uk': uk_pair.astype(BF16), 'w_uv': uv_pair.astype(BF16), 'pe_spread': pe_spread.astype(BF16),
        'hg_gain': row(hg_out_norm[l]), 'w_out': w_out[l].astype(BF16),
        'norm_ffn_pre': row(norm_ffn_pre[l]), 'norm_ffn_post': row(norm_ffn_post[l]),
        'w_ffn_up': w_ffn_up[l].astype(BF16), 'ffn_conv_w': ffn_conv_w[l], 'ffn_conv_b': row(ffn_conv_b[l]),
        'w_ffn_down': w_ffn_down[l].astype(BF16),
    }


def kernel(x_prompt, x_sample, cache_kv_latent, cache_k_rope, state_hgrn, state_ffn_conv, meta_tokens, w_in, norm_mix_pre, norm_mix_post, q_norm, kv_norm, w_uq, w_uk, w_uv, hg_lower_bounds, hg_out_norm, w_out, norm_ffn_pre, norm_ffn_post, w_ffn_up, ffn_conv_w, ffn_conv_b, w_ffn_down, final_norm):
    depth = w_in.shape[0]
    b, seq, d = x_prompt.shape
    bs, ts, _ = x_sample.shape
    past = cache_kv_latent.shape[2]
    d_ff = ffn_conv_w.shape[-1]
    assert seq % CHUNK == 0

    lb_sm = jax.nn.softmax(hg_lower_bounds.astype(F32), axis=0)
    lb_all = jnp.cumsum(lb_sm, axis=0) - lb_sm[0]

    unit = IN_PROJ_ROWS * KEY_BLOCK // 128
    lm = PAD_FRONT + N_META
    lm_alloc = -(-lm // unit) * unit
    xm = jnp.concatenate([jnp.zeros((1, PAD_FRONT, d), F32), meta_tokens[None],
                          jnp.zeros((1, lm_alloc - lm, d), F32)], axis=1)
    tabs_m = _rope_tables(jnp.arange(lm_alloc, dtype=jnp.int32) - lm)
    seq_alloc = -(-seq // KEY_BLOCK) * KEY_BLOCK
    xp = x_prompt if seq_alloc == seq else jnp.pad(x_prompt, ((0, 0), (0, seq_alloc - seq), (0, 0)))
    tabs_p = _rope_tables(jnp.arange(seq_alloc, dtype=jnp.int32))
    xs = x_sample
    tabs_s = tuple(jnp.tile(a, (bs, 1)) for a in _rope_tables(past + jnp.arange(ts, dtype=jnp.int32)))
    fin = final_norm.reshape(1, -1)
    s0_m = jnp.zeros((1, HG_HEADS, HG_DK, HG_DV), F32)
    buf0_m = jnp.zeros((1, CONV_W - 1, d_ff), F32)
    meta_rows = slice(PAD_FRONT, lm)

    outs = [[] for _ in range(8)]
    for l in range(depth):
        w = _layer_weights(l, w_in, norm_mix_pre, norm_mix_post, q_norm, kv_norm, w_uq, w_uk, w_uv, hg_out_norm,
                           w_out, norm_ffn_pre, norm_ffn_post, w_ffn_up, ffn_conv_w, ffn_conv_b, w_ffn_down)
        w['hg_lb'] = lb_all[l].reshape(1, -1)
        last = l == depth - 1

        qcat, kcat, kvt, kv_m, pe_m, zh = _in_proj(xm, tabs_m, w, n_valid=lm, pad_front=PAD_FRONT)
        ol = _mla_prompt(qcat, kcat, kvt, None, lp=lm, dummy=PAD_FRONT)
        ob, s_m = _hgrn(zh, s0_m, n_rows=lm, cs=CHUNK)
        xm, buf_m = _out_ffn(xm, ol, ob, zh, buf0_m, w, fin, n_rows=lm, pad_front=PAD_FRONT, last=last)
        xm = jnp.pad(xm[:, :lm], ((0, 0), (0, lm_alloc - lm), (0, 0)))
        prefix = (kcat[0, meta_rows], kvt[0, :, meta_rows])

        qcat, kcat, kvt, kv, pe, zh = _in_proj(xp, tabs_p, w, n_valid=seq, pad_front=0)
        ol = _mla_prompt(qcat, kcat, kvt, prefix, lp=seq, dummy=0)
        ob, s_new = _hgrn(zh, jnp.broadcast_to(s_m, (b,) + s_m.shape[1:]), n_rows=seq, cs=CHUNK)
        xp, buf = _out_ffn(xp, ol, ob, zh, jnp.broadcast_to(buf_m, (b,) + buf_m.shape[1:]), w, fin, n_rows=seq,
                           pad_front=0, last=last)
        with_meta = lambda m, p: jnp.concatenate(
            [jnp.broadcast_to(m[:, meta_rows], (b, N_META, m.shape[-1])), p[:, :seq]], axis=1)
        for lst, a in zip(outs[:4], (with_meta(kv_m, kv), with_meta(pe_m, pe), s_new, buf)):
            lst.append(a)

        qcat, kcat, _, kv, pe, zh = _in_proj(xs.reshape(1, bs * ts, d), tabs_s, w, n_valid=bs * ts, pad_front=0)
        qcat = jnp.swapaxes(qcat.reshape(MLA_HEADS, bs, ts, QCAT), 0, 1)
        kcat, kv, pe, zh = (a.reshape(bs, ts, -1) for a in (kcat, kv, pe, zh))
        ol = _mla_sample(qcat, kcat, cache_kv_latent[l], cache_k_rope[l])
        ob, s_new = _hgrn(zh, state_hgrn[l], n_rows=ts, cs=ts)
        xs, buf = _out_ffn(xs, ol, ob, zh, state_ffn_conv[l], w, fin, n_rows=ts, pad_front=0, last=last)
        for lst, a in zip(outs[4:], (kv, pe, s_new, buf)):
            lst.append(a)

    y_prompt = xp if seq_alloc == seq else xp[:, :seq]
    return (y_prompt, xs) + tuple(jnp.stack(o) for o in outs)
```

```python
import functools

import jax
import jax.numpy as jnp
from jax import lax
from jax.experimental import pallas as pl
from jax.experimental.pallas import tpu as pltpu

F32 = jnp.float32
BF16 = jnp.bfloat16

CHUNK = 64
N_META = 16
PAD_FRONT = CHUNK - N_META
EPS = 1e-6
MLA_HEADS = 8
QK_NOPE = 64
QK_ROPE = 32
V_HEAD = 64
Q_LORA = 256
KV_LORA = 128
ROPE_THETA = 10000.0
HG_HEADS = 4
HG_DK = 128
HG_DV = 128
CONV_W = 3

HG_KW = HG_HEADS * HG_DK
HG_WIDTH = HG_HEADS * HG_DV
LANES = 128
KEY_BLOCK = 256
QCAT = KV_LORA + LANES
VT_ROWS = KV_LORA + 16
SOFTMAX_FAST_RANGE = 100.0
PREFIX_KEYS = N_META
IN_PROJ_ROWS = 256
IN_PROJ_STREAMS = 2
Q_SCALE = (QK_NOPE + QK_ROPE) ** -0.5 * 1.4426950408889634
HGRN_FAST_RANGE = 80.0
FFN_ROWS = 256
FFN_STREAMS = 2
HGRN_STREAMS = 8
V7X_VMEM_BYTES = 64 * 1024 * 1024
VMEM_LIMIT = V7X_VMEM_BYTES * 7 // 8


def _row_tile(n, target):
    if n <= target:
        return n
    best = None
    for t in range(16, target + 1, 16):
        if n % t == 0:
            best = t
    assert best is not None, (n, target)
    return best


def _const_spec(shape):
    nd = len(shape)
    return pl.BlockSpec(shape, lambda *_: (0,) * nd, pipeline_mode=pl.Buffered(1))


def _rms(x, w):
    return x * lax.rsqrt(jnp.mean(x * x, axis=-1, keepdims=True) + EPS) * w


def _silu(x):
    return x * jax.nn.sigmoid(x)


def _mm(a, b):
    return jnp.dot(a.astype(BF16), b.astype(BF16), preferred_element_type=F32)


def _mm_nt(a, b):
    return lax.dot_general(a.astype(BF16), b.astype(BF16), (((1,), (1,)), ((), ())),
                           preferred_element_type=F32)


def _mm_tn(a, b):
    return lax.dot_general(a.astype(BF16), b.astype(BF16), (((0,), (0,)), ((), ())),
                           preferred_element_type=F32)


def _in_proj_body(x_ref, cos_ref, sn_ref, sp_ref, cosq_ref, sinq_ref, npre_ref, win_ref, qn_ref, kvn_ref, wuq_ref,
                  wuk_ref, spread_ref, lb_ref, qcat_ref, kcat_ref, kvt_ref, kv_ref, pe_ref, zh_ref,
                  *, bb, tm, n_valid, pad_front):
    cos, sn, sp = cos_ref[...], sn_ref[...], sp_ref[...]
    lb = lb_ref[...]

    def rope(v):
        return v * cos + pltpu.roll(v, LANES - QK_ROPE // 2, 1) * sn + pltpu.roll(v, QK_ROPE // 2, 1) * sp

    row = pl.program_id(1) * tm + lax.broadcasted_iota(jnp.int32, (tm, 1), 0)
    live = row < n_valid
    real = row >= pad_front
    h = [_rms(jnp.where(live, x_ref[bi], 0.0), npre_ref[...]) for bi in range(bb)]
    zs = [_mm(h[bi], win_ref[...]) for bi in range(bb)]
    nope_w = MLA_HEADS * QK_NOPE
    for bi, z in enumerate(zs):
        off = Q_LORA + KV_LORA + LANES
        hf = z[:, off + HG_KW:off + 2 * HG_KW]
        e = jnp.exp(-jnp.abs(hf))
        r = 1.0 / (1.0 + e)
        f = lb + (1.0 - lb) * jnp.where(hf >= 0.0, r, e * r)
        zh_ref[bi, :, :HG_KW] = jnp.where(real, _silu(z[:, off:off + HG_KW]), 0.0)
        zh_ref[bi, :, HG_KW:2 * HG_KW] = jnp.where(real, jnp.log(f), 0.0)
        zh_ref[bi, :, 2 * HG_KW:3 * HG_KW] = jnp.where(real, (1.0 - lb) * jnp.where(hf >= 0.0, e * r, r), 0.0)
        zh_ref[bi, :, 3 * HG_KW:] = z[:, off + 2 * HG_KW:]
        kv_lat = _rms(z[:, Q_LORA:Q_LORA + KV_LORA], kvn_ref[...])
        k_pe = rope(z[:, Q_LORA + KV_LORA:Q_LORA + KV_LORA + LANES])
        kv_ref[bi] = kv_lat
        pe_ref[bi] = k_pe[:, :QK_ROPE]
        kcat_ref[bi, :, :KV_LORA] = kv_lat.astype(BF16)
        kcat_ref[bi, :, KV_LORA:] = k_pe.astype(BF16)
        kvt_ref[bi, :KV_LORA, :] = kv_lat.T.astype(BF16)
        kvt_ref[bi, KV_LORA:, :] = jnp.ones((VT_ROWS - KV_LORA, tm), BF16)

        cqn = _rms(z[:, :Q_LORA], qn_ref[...])
        q = _mm(cqn, wuq_ref[...]) * Q_SCALE
        x1, x2 = q[:, nope_w:nope_w + LANES], q[:, nope_w + LANES:]
        cq, sq = cosq_ref[...], sinq_ref[...]
        roped = jnp.concatenate([x1 * cq - x2 * sq, x1 * sq + x2 * cq], axis=1)
        pe_heads = _mm(roped, spread_ref[...])
        for j in range(MLA_HEADS // 2):
            ql = _mm(q[:, 2 * QK_NOPE * j:2 * QK_NOPE * (j + 1)], wuk_ref[j])
            for hh in range(2):
                hd = 2 * j + hh
                qcat_ref[bi, hd, :, :KV_LORA] = ql[:, KV_LORA * hh:KV_LORA * (hh + 1)].astype(BF16)
                qcat_ref[bi, hd, :, KV_LORA:] = pe_heads[:, LANES * hd:LANES * (hd + 1)].astype(BF16)


def _in_proj(x, tabs, w, *, n_valid, pad_front):
    bx, t, d = x.shape
    tm = _row_tile(t, IN_PROJ_ROWS)
    assert tm % LANES == 0 or tm == t
    nt = t // tm
    bb = IN_PROJ_STREAMS if bx % IN_PROJ_STREAMS == 0 else 1
    row = lambda shape: pl.BlockSpec((bb, tm) + shape, lambda b, i: (b, i) + (0,) * len(shape))
    tab = pl.BlockSpec((tm, LANES), lambda b, i: (i, 0))
    zw = 3 * HG_KW + 2 * HG_WIDTH
    out_shape = (
        jax.ShapeDtypeStruct((bx, MLA_HEADS, t, QCAT), BF16),
        jax.ShapeDtypeStruct((bx, t, QCAT), BF16),
        jax.ShapeDtypeStruct((bx, VT_ROWS, t), BF16),
        jax.ShapeDtypeStruct((bx, t, KV_LORA), F32),
        jax.ShapeDtypeStruct((bx, t, QK_ROPE), F32),
        jax.ShapeDtypeStruct((bx, t, zw), F32),
    )
    out_specs = (
        pl.BlockSpec((bb, MLA_HEADS, tm, QCAT), lambda b, i: (b, 0, i, 0)),
        row((QCAT,)),
        pl.BlockSpec((bb, VT_ROWS, tm), lambda b, i: (b, 0, i)),
        row((KV_LORA,)), row((QK_ROPE,)), row((zw,)),
    )
    weights = (w['norm_mix_pre'], w['w_in'], w['q_norm'], w['kv_norm'], w['w_uq'], w['w_uk'], w['pe_spread'],
               w['hg_lb'])
    return pl.pallas_call(
        functools.partial(_in_proj_body, bb=bb, tm=tm, n_valid=n_valid, pad_front=pad_front),
        grid=(bx // bb, nt),
        in_specs=[row((d,))] + [tab] * len(tabs) + [_const_spec(a.shape) for a in weights],
        out_specs=out_specs,
        out_shape=out_shape,
        compiler_params=pltpu.CompilerParams(dimension_semantics=("parallel", "parallel"),
                                             vmem_limit_bytes=VMEM_LIMIT),
        name="in_proj",
    )(x, *tabs, *weights)


def _mla_prompt_body(*refs, qt, j0, aliased, dummy, prefix):
    q_ref, k_ref, vt_ref = refs[:3]
    kpre_ref, vtpre_ref = refs[3:5] if prefix else (None, None)
    o_ref, m_scr, top_scr, acc_scr = refs[-4:]
    j = pl.program_id(1) + j0
    width = MLA_HEADS * qt
    pre = PREFIX_KEYS if prefix else 0
    lower_half = lax.broadcasted_iota(jnp.int32, (CHUNK, LANES), 1) < CHUNK

    def hidden(rows, cols=LANES):
        return jnp.full((rows, cols), -jnp.inf, F32)

    def mask_dummy(s):
        return jnp.concatenate([hidden(dummy, s.shape[1]), s[dummy:]], axis=0) if dummy else s

    def mask_diagonal(s):
        strips = []
        for c0 in range(0, width, LANES):
            rows = [s[:pre, c0:c0 + LANES]] if prefix else []
            t = s[pre:, c0:c0 + LANES]
            if (c0 % qt) // LANES == 0:
                rows += [t[:CHUNK], jnp.where(lower_half, -jnp.inf, t[CHUNK:2 * CHUNK]), hidden(2 * CHUNK)]
            else:
                rows += [t[:3 * CHUNK], jnp.where(lower_half, -jnp.inf, t[3 * CHUNK:])]
            strips.append(jnp.concatenate(rows, axis=0))
        return jnp.concatenate(strips, axis=1)

    def keys_of(kb):
        start = kb * KEY_BLOCK
        return pl.ds(start if isinstance(kb, int) else pl.multiple_of(start, KEY_BLOCK), KEY_BLOCK)

    def scores(kb, diagonal=False):
        keys = k_ref[0, keys_of(kb), :]
        if diagonal and prefix:
            keys = jnp.concatenate([kpre_ref[...], keys], axis=0)
        return _mm_nt(keys, q_ref[0].reshape(width, QCAT))

    def weighted_values(p, kb, diagonal):
        if diagonal and prefix:
            return _mm(vtpre_ref[...], p[:pre]) + _mm(vt_ref[0, :, keys_of(kb)], p[pre:])
        return _mm(vt_ref[0, :, keys_of(kb)], p)

    def absorb(s, kb, diagonal=False):
        if diagonal:
            s = mask_diagonal(s)
        if isinstance(kb, int) and kb == 0:
            s = mask_dummy(s)
        for hd in range(MLA_HEADS):
            cols = slice(qt * hd, qt * (hd + 1))
            m_new = jnp.max(s[:, cols], axis=0, keepdims=True)
            if not diagonal:
                m_old = m_scr[:, cols]
                m_new = jnp.maximum(m_old, m_new)
            pv = weighted_values(jnp.exp2(s[:, cols] - m_new), kb, diagonal)
            acc_scr[:, cols] = pv if diagonal else jnp.exp2(m_old - m_new) * acc_scr[:, cols] + pv
            m_scr[:, cols] = m_new

    def absorb_all(blocks):
        for s, kb, diagonal in blocks:
            absorb(s, kb, diagonal)

    def absorb_unscaled(blocks):
        blocks = [(mask_diagonal(s) if diagonal else s, kb, diagonal) for s, kb, diagonal in blocks]
        blocks = [(mask_dummy(s) if isinstance(kb, int) and kb == 0 else s, kb, d) for s, kb, d in blocks]
        first = blocks[0][2]
        for hd in range(MLA_HEADS):
            cols = slice(qt * hd, qt * (hd + 1))
            tops = [jnp.max(s[:, cols], axis=0, keepdims=True) for s, _, _ in blocks]
            if first:
                shift = jnp.max(blocks[0][0][:PREFIX_KEYS, cols], axis=0, keepdims=True) if prefix else tops[0]
                m_scr[:, cols] = shift
                top_scr[:, cols] = functools.reduce(jnp.maximum, tops)
            else:
                shift = m_scr[:, cols]
                top_scr[:, cols] = functools.reduce(jnp.maximum, tops, top_scr[:, cols])
            pv = None
            for s, kb, diagonal in blocks:
                term = weighted_values(jnp.exp2(s[:, cols] - shift), kb, diagonal)
                pv = term if pv is None else pv + term
            acc_scr[:, cols] = pv if first else acc_scr[:, cols] + pv

    def visit_blocks(absorb_blocks):
        def pair(a, b, diagonal=False):
            s_a, s_b = scores(a, diagonal), scores(b)
            absorb_blocks([(s_a, a, diagonal), (s_b, b, False)])

        def absorb_block(s, kb, diagonal=False):
            absorb_blocks([(s, kb, diagonal)])

        @pl.when(j == 0)
        def _():
            absorb_block(scores(0, True), 0, True)

        n_blocks = k_ref.shape[1] // KEY_BLOCK
        if n_blocks < 2:
            return

        @pl.when(j == 1)
        def _():
            pair(1, 0, True)

        if n_blocks < 3:
            return

        @pl.when(j > 1)
        def _():
            pair(j, j - 1, True)
            n_inner = (j - 2) // 2

            def inner(p, carry):
                pair(j - 2 * p, j - 2 * p - 1)
                return carry

            lax.fori_loop(1, n_inner + 1, inner, 0)

            @pl.when(j % 2 == 0)
            def _():
                absorb_block(scores(0), 0)

            @pl.when(j % 2 == 1)
            def _():
                pair(1, 0)

    def finish():
        for hd in range(MLA_HEADS):
            acc = acc_scr[:, qt * hd:qt * (hd + 1)]
            o_t = acc[:KV_LORA] * (1.0 / acc[KV_LORA:KV_LORA + 1])
            o_ref[0, :, KV_LORA * hd:KV_LORA * (hd + 1)] = o_t.T.astype(BF16)

    visit_blocks(absorb_unscaled)
    runaway = jnp.max(top_scr[...] - m_scr[...]) > SOFTMAX_FAST_RANGE
    finish()

    @pl.when(runaway)
    def _():
        visit_blocks(absorb_all)
        finish()


def _mla_prompt_call(qcat, kcat, kvt, prefix, prev, *, qt, j0, steps, dummy, rows_out):
    b, _, t, _ = qcat.shape
    per_block = KEY_BLOCK // qt
    in_specs = [pl.BlockSpec((1, MLA_HEADS, qt, QCAT), lambda i, j: (i, 0, (j + j0) * per_block, 0)),
                pl.BlockSpec((1, t, QCAT), lambda i, j: (i, 0, 0)),
                pl.BlockSpec((1, VT_ROWS, t), lambda i, j: (i, 0, 0))]
    args = (qcat, kcat, kvt)
    if prefix is not None:
        in_specs += [_const_spec(a.shape) for a in prefix]
        args += tuple(prefix)
    if prev is not None:
        in_specs.append(pl.BlockSpec(memory_space=pl.ANY))
        args += (prev,)
    return pl.pallas_call(
        functools.partial(_mla_prompt_body, qt=qt, j0=j0, aliased=prev is not None, dummy=dummy,
                          prefix=prefix is not None),
        grid=(b, steps),
        in_specs=in_specs,
        out_specs=pl.BlockSpec((1, qt, MLA_HEADS * KV_LORA), lambda i, j: (i, (j + j0) * per_block, 0)),
        out_shape=jax.ShapeDtypeStruct((b, rows_out, MLA_HEADS * KV_LORA), BF16),
        input_output_aliases={len(args) - 1: 0} if prev is not None else {},
        scratch_shapes=[pltpu.VMEM((1, MLA_HEADS * qt), F32), pltpu.VMEM((1, MLA_HEADS * qt), F32),
                        pltpu.VMEM((VT_ROWS, MLA_HEADS * qt), F32)],
        compiler_params=pltpu.CompilerParams(dimension_semantics=("parallel", "parallel"),
                                             vmem_limit_bytes=VMEM_LIMIT),
        name="mla_prompt",
    )(*args)


def _mla_prompt(qcat, kcat, kvt, prefix, *, lp, dummy):
    t = qcat.shape[2]
    assert t % KEY_BLOCK == 0 and lp <= t
    full, rest = divmod(lp, KEY_BLOCK)
    if rest > LANES:
        full, rest = full + 1, 0
    out = None
    rows = full * KEY_BLOCK + (LANES if rest else 0)
    if full:
        out = _mla_prompt_call(qcat, kcat, kvt, prefix, None, qt=KEY_BLOCK, j0=0, steps=full, dummy=dummy,
                               rows_out=rows)
    if rest:
        out = _mla_prompt_call(qcat, kcat, kvt, prefix, out, qt=LANES, j0=full, steps=1, dummy=dummy,
                               rows_out=rows)
    return out


def _mla_sample_body(q_ref, k_ref, ckv_ref, cpe_ref, o_ref, *, ts):
    rows = MLA_HEADS * ts
    q = q_ref[0].reshape(rows, QCAT)
    ckv = ckv_ref[0].astype(BF16)
    knew = k_ref[0]
    s_c = _mm_nt(q[:, :KV_LORA], ckv) + _mm_nt(q[:, KV_LORA:KV_LORA + QK_ROPE], cpe_ref[0])
    s_n = _mm_nt(q, knew)
    m = jnp.maximum(jnp.max(s_c, axis=-1, keepdims=True), jnp.max(s_n, axis=-1, keepdims=True))
    p_c = jnp.exp2(s_c - m)
    p_n = jnp.exp2(s_n - m)
    l = jnp.sum(p_c, axis=-1, keepdims=True) + jnp.sum(p_n, axis=-1, keepdims=True)
    o = (_mm(p_c, ckv) + _mm(p_n, knew[:, :KV_LORA])) / l
    for hd in range(MLA_HEADS):
        o_ref[0, :, KV_LORA * hd:KV_LORA * (hd + 1)] = o[ts * hd:ts * (hd + 1)].astype(BF16)


def _mla_sample(qcat, kcat, cache_kv, cache_pe):
    b, _, ts, _ = qcat.shape
    past = cache_kv.shape[1]
    return pl.pallas_call(
        functools.partial(_mla_sample_body, ts=ts),
        grid=(b,),
        in_specs=[pl.BlockSpec((1, MLA_HEADS, ts, QCAT), lambda i: (i, 0, 0, 0)),
                  pl.BlockSpec((1, ts, QCAT), lambda i: (i, 0, 0)),
                  pl.BlockSpec((1, past, KV_LORA), lambda i: (i, 0, 0)),
                  pl.BlockSpec((1, past, QK_ROPE), lambda i: (i, 0, 0))],
        out_specs=pl.BlockSpec((1, ts, MLA_HEADS * KV_LORA), lambda i: (i, 0, 0)),
        out_shape=jax.ShapeDtypeStruct((b, ts, MLA_HEADS * KV_LORA), BF16),
        compiler_params=pltpu.CompilerParams(dimension_semantics=("parallel",),
                                             vmem_limit_bytes=VMEM_LIMIT),
        name="mla_sample",
    )(qcat, kcat, cache_kv, cache_pe)


def _hgrn_body(zh_ref, s0_ref, ob_ref, sout_ref, st_ref, g_scr, q_scr, k_scr, v_scr, o_scr, oi_scr, *, bb, cs):
    c = pl.program_id(1)
    nc = pl.num_programs(1)
    streams = range(bb)
    heads = [slice(HG_DK * hd, HG_DK * (hd + 1)) for hd in range(HG_HEADS)]

    @pl.when(c == 0)
    def _():
        for bi in streams:
            for hd in range(HG_HEADS):
                st_ref[bi, hd] = s0_ref[bi, hd].T

    q = zh_ref[:, :, :HG_KW]
    gl = zh_ref[:, :, HG_KW:2 * HG_KW]
    k = zh_ref[:, :, 2 * HG_KW:3 * HG_KW]
    v = zh_ref[:, :, 3 * HG_KW:3 * HG_KW + HG_WIDTH]
    t_idx = lax.broadcasted_iota(jnp.int32, (cs, cs), 0)
    s_idx = lax.broadcasted_iota(jnp.int32, (cs, cs), 1)
    causal = t_idx >= s_idx
    row = lax.broadcasted_iota(jnp.int32, (cs, 1), 0)
    g = [gl[bi] for bi in streams]
    shift = 1
    while shift < cs:
        g = [x + jnp.where(row >= shift, pltpu.roll(x, shift, 0), 0.0) for x in g]
        shift *= 2
    g = jnp.stack(g)
    g_tot = g[:, cs - 1:cs, :]
    any_out_of_range = jnp.max(-g_tot) > HGRN_FAST_RANGE
    qt = q * jnp.exp(g)
    khat = k * jnp.exp(g_tot - g)
    kt = k * jnp.exp(-g)
    a = [[jnp.where(causal, _mm_nt(qt[bi][:, sl], kt[bi][:, sl]), 0.0) for sl in heads] for bi in streams]
    for bi in streams:
        o_scr[bi] = jnp.concatenate([_mm(a[bi][hd], v[bi][:, sl]) for hd, sl in enumerate(heads)], axis=-1)

    o_inter = [[_mm_nt(qt[bi][:, sl], st_ref[bi, hd]) for hd, sl in enumerate(heads)] for bi in streams]
    kv_new = [[_mm_tn(v[bi][:, sl], khat[bi][:, sl]) for sl in heads] for bi in streams]
    decay = jnp.exp(g_tot)
    for bi in streams:
        oi_scr[bi] = jnp.concatenate(o_inter[bi], axis=-1)
        for hd, sl in enumerate(heads):
            st_ref[bi, hd] = decay[bi][:, sl] * st_ref[bi, hd] + kv_new[bi][hd]

    @pl.when(any_out_of_range)
    def _():
        for bi in streams:
            @pl.when(jnp.max(-g_tot[bi]) > HGRN_FAST_RANGE)
            def _(bi=bi):
                g_scr[...] = g[bi]
                q_scr[...] = q[bi]
                k_scr[...] = k[bi]
                v_scr[...] = v[bi]
                o_scr[bi] = jnp.zeros((cs, HG_WIDTH), F32)
                t_col = lax.broadcasted_iota(jnp.int32, (cs, 1), 0)

                def cols(i, carry):
                    base = pl.multiple_of(i * 8, 8)
                    g8, k8, v8 = (ref[pl.ds(base, 8), :] for ref in (g_scr, k_scr, v_scr))
                    for j in range(8):
                        dec = jnp.exp(jnp.where(t_col >= base + j, g_scr[...] - g8[j:j + 1], -jnp.inf))
                        pr = q_scr[...] * dec * k8[j:j + 1]
                        for sl in heads:
                            o_scr[bi, :, sl] += jnp.sum(pr[:, sl], axis=-1, keepdims=True) * v8[j:j + 1, sl]
                    return carry

                lax.fori_loop(0, cs // 8, cols, 0)

    for bi in streams:
        ob_ref[bi] = o_scr[bi] + oi_scr[bi]

    @pl.when(c == nc - 1)
    def _():
        for bi in streams:
            for hd in range(HG_HEADS):
                sout_ref[bi, hd] = st_ref[bi, hd].T


def _hgrn(zh, s0, *, n_rows, cs):
    b, t, zw = zh.shape
    assert n_rows % cs == 0 and n_rows <= t
    bb = HGRN_STREAMS if b % HGRN_STREAMS == 0 else 1
    return pl.pallas_call(
        functools.partial(_hgrn_body, bb=bb, cs=cs),
        grid=(b // bb, n_rows // cs),
        in_specs=[pl.BlockSpec((bb, cs, zw - HG_WIDTH), lambda i, c: (i, c, 0)),
                  pl.BlockSpec((bb, HG_HEADS, HG_DK, HG_DV), lambda i, c: (i, 0, 0, 0))],
        out_specs=(pl.BlockSpec((bb, cs, HG_WIDTH), lambda i, c: (i, c, 0)),
                   pl.BlockSpec((bb, HG_HEADS, HG_DK, HG_DV), lambda i, c: (i, 0, 0, 0))),
        out_shape=(jax.ShapeDtypeStruct((b, n_rows, HG_WIDTH), F32),
                   jax.ShapeDtypeStruct((b, HG_HEADS, HG_DK, HG_DV), F32)),
        scratch_shapes=[pltpu.VMEM((bb, HG_HEADS, HG_DV, HG_DK), F32)]
        + [pltpu.VMEM((cs, HG_KW), F32) for _ in range(4)] + [pltpu.VMEM((bb, cs, HG_KW), F32) for _ in range(2)],
        compiler_params=pltpu.CompilerParams(dimension_semantics=("parallel", "arbitrary"),
                                             vmem_limit_bytes=VMEM_LIMIT),
        name="hgrn",
    )(zh, s0)


def _out_ffn_body(x_ref, ol_ref, ob_ref, gate_ref, buf_ref, gain_ref, wuv_ref, wout_ref, npost_ref, nfpre_ref,
                  wup_ref, cw_ref, cb_ref, wdn_ref, nfpost_ref, fin_ref, xo_ref, bufo_ref, a_scr,
                  *, bb, tm, pad_front, last, flat):
    t = pl.program_id(1)
    nt = pl.num_programs(1)
    d_ff = cw_ref.shape[1]
    keep = CONV_W - 1
    streams = range(bb)

    @pl.when(t == 0)
    def _():
        for bi in streams:
            a_scr[bi, 0:8, :] = jnp.zeros((8, d_ff), F32)
            a_scr[bi, 8 - keep:8, :] = buf_ref[bi]

    def mm_streams(xs, w):
        if not flat:
            return [_mm(x, w) for x in xs]
        out = _mm(jnp.concatenate(xs, axis=0), w)
        return [out[tm * bi:tm * (bi + 1)] for bi in streams]

    ol = [ol_ref[bi] for bi in streams]
    o_a = [mm_streams([o[:, 2 * KV_LORA * j:2 * KV_LORA * (j + 1)] for o in ol], wuv_ref[j])
           for j in range(MLA_HEADS // 2)]
    mixed = []
    for bi in streams:
        ob = ob_ref[bi]
        o_b = [_rms(ob[:, HG_DV * hd:HG_DV * (hd + 1)], gain_ref[...]) for hd in range(HG_HEADS)]
        o_b = (jnp.concatenate(o_b, axis=-1) * _silu(gate_ref[bi])).astype(BF16)
        mixed.append(jnp.concatenate([o[bi].astype(BF16) for o in o_a] + [o_b], axis=-1))
    y = mm_streams(mixed, wout_ref[...])
    x1 = [x_ref[bi] + _rms(y[bi], npost_ref[...]) for bi in streams]
    u = mm_streams([_rms(x1[bi], nfpre_ref[...]).astype(BF16) for bi in streams], wup_ref[...])
    cw = cw_ref[...]
    if pad_front:
        live = (t * tm + lax.broadcasted_iota(jnp.int32, (tm, 1), 0)) >= pad_front
    gated = []
    for bi in streams:
        a = u[bi][:, :d_ff]
        if pad_front:
            a = jnp.where(live, a, 0.0)
        a_scr[bi, 8:8 + tm, :] = a
        conv = cb_ref[...] + a * cw[keep:keep + 1]
        for j in range(keep):
            conv = conv + a_scr[bi, 8 - keep + j:8 - keep + j + tm, :] * cw[j:j + 1]
        gated.append((_silu(conv) * u[bi][:, d_ff:]).astype(BF16))
    f = mm_streams(gated, wdn_ref[...])
    for bi in streams:
        x2 = x1[bi] + _rms(f[bi], nfpost_ref[...])
        xo_ref[bi] = _rms(x2, fin_ref[...]) if last else x2
        a_scr[bi, 0:8, :] = a_scr[bi, tm:tm + 8, :]

    @pl.when(t == nt - 1)
    def _():
        for bi in streams:
            bufo_ref[bi] = a_scr[bi, 8 - keep:8, :]


def _out_ffn(x, ol, ob, zh, buf, w, final_norm, *, n_rows, pad_front, last):
    b, t, d = x.shape
    d_ff = buf.shape[-1]
    tm = _row_tile(n_rows, FFN_ROWS)
    assert tm >= 8 and n_rows <= t and zh.shape[-1] % HG_WIDTH == 0
    flat = tm < LANES
    bb = b if flat else (FFN_STREAMS if b % FFN_STREAMS == 0 else 1)
    row = lambda width: pl.BlockSpec((bb, tm, width), lambda i, j: (i, j, 0))
    gate =pl.BlockSpec((bb, tm, HG_WIDTH), lambda i, j: (i, j, zh.shape[-1] // HG_WIDTH - 1))
    per_b = pl.BlockSpec((bb, CONV_W - 1, d_ff), lambda i, j: (i, 0, 0))
    weights = (w['hg_gain'], w['w_uv'], w['w_out'], w['norm_mix_post'], w['norm_ffn_pre'], w['w_ffn_up'],
               w['ffn_conv_w'], w['ffn_conv_b'], w['w_ffn_down'], w['norm_ffn_post'], final_norm)
    return pl.pallas_call(
        functools.partial(_out_ffn_body, bb=bb, tm=tm, pad_front=pad_front, last=last, flat=flat),
        grid=(b // bb, n_rows // tm),
        in_specs=[row(d), row(MLA_HEADS * KV_LORA), row(HG_WIDTH), gate, per_b]
        + [_const_spec(a.shape) for a in weights],
        out_specs=(row(d), per_b),
        out_shape=(jax.ShapeDtypeStruct((b, n_rows, d), F32), jax.ShapeDtypeStruct((b, CONV_W - 1, d_ff), F32)),
        scratch_shapes=[pltpu.VMEM((bb, tm + 8, d_ff), F32)],
        compiler_params=pltpu.CompilerParams(dimension_semantics=("parallel", "arbitrary"),
                                             vmem_limit_bytes=VMEM_LIMIT),
        name="out_ffn",
    )(x, ol, ob, zh, buf, *weights)


def _rope_tables(pos):
    half = QK_ROPE // 2
    inv = ROPE_THETA ** (-jnp.arange(half, dtype=F32) / half)
    ang = pos.astype(F32)[:, None] * inv
    cos, sin = jnp.cos(ang), jnp.sin(ang)
    zero = jnp.zeros_like(cos)
    pad = jnp.zeros((pos.shape[0], LANES - QK_ROPE), F32)
    return (jnp.concatenate([cos, cos, pad], axis=1),
            jnp.concatenate([-sin, zero, pad], axis=1),
            jnp.concatenate([zero, sin, pad], axis=1),
            jnp.tile(cos, (1, MLA_HEADS)), jnp.tile(sin, (1, MLA_HEADS)))


def _layer_weights(l, w_in, norm_mix_pre, norm_mix_post, q_norm, kv_norm, w_uq, w_uk, w_uv, hg_out_norm, w_out,
                   norm_ffn_pre, norm_ffn_post, w_ffn_up, ffn_conv_w, ffn_conv_b, w_ffn_down):
    d = w_in.shape[1]
    head_w = Q_LORA + KV_LORA + QK_ROPE
    w_in_r = jnp.concatenate([w_in[l][:, :head_w], jnp.zeros((d, LANES - QK_ROPE), F32), w_in[l][:, head_w:]], axis=1)
    uq = w_uq[l].reshape(Q_LORA, MLA_HEADS, QK_NOPE + QK_ROPE)
    half = QK_ROPE // 2
    w_uq_r = jnp.concatenate([uq[:, :, :QK_NOPE].reshape(Q_LORA, -1),
                              uq[:, :, QK_NOPE:QK_NOPE + half].reshape(Q_LORA, -1),
                              uq[:, :, QK_NOPE + half:].reshape(Q_LORA, -1)], axis=1)
    src = jnp.arange(2 * MLA_HEADS * half)
    dst = (src % (MLA_HEADS * half)) // half * LANES + src // (MLA_HEADS * half) * half + src % half
    pe_spread = jnp.zeros((2 * MLA_HEADS * half, MLA_HEADS * LANES), F32).at[src, dst].set(1.0)
    uk = jnp.transpose(w_uk[l], (1, 2, 0))
    uk_pair = jnp.zeros((MLA_HEADS // 2, 2 * QK_NOPE, 2 * KV_LORA), F32)
    uk_pair = uk_pair.at[:, :QK_NOPE, :KV_LORA].set(uk[0::2]).at[:, QK_NOPE:, KV_LORA:].set(uk[1::2])
    uv = jnp.transpose(w_uv[l], (1, 0, 2))
    uv_pair = jnp.zeros((MLA_HEADS // 2, 2 * KV_LORA, 2 * V_HEAD), F32)
    uv_pair = uv_pair.at[:, :KV_LORA, :V_HEAD].set(uv[0::2]).at[:, KV_LORA:, V_HEAD:].set(uv[1::2])
    row = lambda a: a.reshape(1, -1)
    return {
        'w_in': w_in_r.astype(BF16), 'norm_mix_pre': row(norm_mix_pre[l]), 'norm_mix_post': row(norm_mix_post[l]),
        'q_norm': row(q_norm[l]), 'kv_norm': row(kv_norm[l]), 'w_uq': w_uq_r.astype(BF16),
        'w_uk': uk_pair.astype(BF16), 'w_uv': uv_pair.astype(BF16), 'pe_spread': pe_spread.astype(BF16),
        'hg_gain': row(hg_out_norm[l]), 'w_out': w_out[l].astype(BF16),
        'norm_ffn_pre': row(norm_ffn_pre[l]), 'norm_ffn_post': row(norm_ffn_post[l]),
        'w_ffn_up': w_ffn_up[l].astype(BF16), 'ffn_conv_w': ffn_conv_w[l], 'ffn_conv_b': row(ffn_conv_b[l]),
        'w_ffn_down': w_ffn_down[l].astype(BF16),
    }


def kernel(x_prompt, x_sample, cache_kv_latent, cache_k_rope, state_hgrn, state_ffn_conv, meta_tokens, w_in, norm_mix_pre, norm_mix_post, q_norm, kv_norm, w_uq, w_uk, w_uv, hg_lower_bounds, hg_out_norm, w_out, norm_ffn_pre, norm_ffn_post, w_ffn_up, ffn_conv_w, ffn_conv_b, w_ffn_down, final_norm):
    depth = w_in.shape[0]
    b, seq, d = x_prompt.shape
    bs, ts, _ = x_sample.shape
    past = cache_kv_latent.shape[2]
    d_ff = ffn_conv_w.shape[-1]
    assert seq % CHUNK == 0

    lb_sm = jax.nn.softmax(hg_lower_bounds.astype(F32), axis=0)
    lb_all = jnp.cumsum(lb_sm, axis=0) - lb_sm[0]

    assert KEY_BLOCK % IN_PROJ_ROWS == 0
    lm = PAD_FRONT + N_META
    lm_alloc = -(-lm // KEY_BLOCK) * KEY_BLOCK
    seq_alloc = -(-seq // KEY_BLOCK) * KEY_BLOCK
    allocated = lambda a, rows: a if a.shape[1] == rows else jnp.pad(a, ((0, 0), (0, rows - a.shape[1]), (0, 0)))
    xm = allocated(jnp.concatenate([jnp.zeros((1, PAD_FRONT, d), F32), meta_tokens[None]], axis=1), lm_alloc)
    tabs_m = _rope_tables(jnp.arange(lm_alloc, dtype=jnp.int32) - lm)
    xp = allocated(x_prompt, seq_alloc)
    tabs_p = _rope_tables(jnp.arange(seq_alloc, dtype=jnp.int32))
    xs = x_sample
    tabs_s = tuple(jnp.tile(a, (bs, 1)) for a in _rope_tables(past + jnp.arange(ts, dtype=jnp.int32)))
    fin = final_norm.reshape(1, -1)
    s0_m = jnp.zeros((1, HG_HEADS, HG_DK, HG_DV), F32)
    buf0_m = jnp.zeros((1, CONV_W - 1, d_ff), F32)
    meta_rows = slice(PAD_FRONT, lm)

    outs = [[] for _ in range(8)]
    for l in range(depth):
        w = _layer_weights(l, w_in, norm_mix_pre, norm_mix_post, q_norm, kv_norm, w_uq, w_uk, w_uv, hg_out_norm,
                           w_out, norm_ffn_pre, norm_ffn_post, w_ffn_up, ffn_conv_w, ffn_conv_b, w_ffn_down)
        w['hg_lb'] = lb_all[l].reshape(1, -1)
        last = l == depth - 1

        qcat, kcat, kvt, kv_m, pe_m, zh = _in_proj(xm, tabs_m, w, n_valid=lm, pad_front=PAD_FRONT)
        ol = _mla_prompt(qcat, kcat, kvt, None, lp=lm, dummy=PAD_FRONT)
        ob, s_m = _hgrn(zh, s0_m, n_rows=lm, cs=CHUNK)
        xm, buf_m = _out_ffn(xm, ol, ob, zh, buf0_m, w, fin, n_rows=lm, pad_front=PAD_FRONT, last=last)
        xm = allocated(xm, lm_alloc)
        prefix = (kcat[0, meta_rows], kvt[0, :, meta_rows])

        qcat, kcat, kvt, kv, pe, zh = _in_proj(allocated(xp, seq_alloc), tabs_p, w, n_valid=seq, pad_front=0)
        ol = _mla_prompt(qcat, kcat, kvt, prefix, lp=seq, dummy=0)
        ob, s_new = _hgrn(zh, jnp.broadcast_to(s_m, (b,) + s_m.shape[1:]), n_rows=seq, cs=CHUNK)
        xp, buf = _out_ffn(xp, ol, ob, zh, jnp.broadcast_to(buf_m, (b,) + buf_m.shape[1:]), w, fin, n_rows=seq,
                           pad_front=0, last=last)
        with_meta = lambda m, p: jnp.concatenate(
            [jnp.broadcast_to(m[:, meta_rows], (b, N_META, m.shape[-1])), p[:, :seq]], axis=1)
        for lst, a in zip(outs[:4], (with_meta(kv_m, kv), with_meta(pe_m, pe), s_new, buf)):
            lst.append(a)

        qcat, kcat, _, kv, pe, zh = _in_proj(xs.reshape(1, bs * ts, d), tabs_s, w, n_valid=bs * ts, pad_front=0)
        qcat = jnp.swapaxes(qcat.reshape(MLA_HEADS, bs, ts, QCAT), 0, 1)
        kcat, kv, pe, zh = (a.reshape(bs, ts, -1) for a in (kcat, kv, pe, zh))
        ol = _mla_sample(qcat, kcat, cache_kv_latent[l], cache_k_rope[l])
        ob, s_new = _hgrn(zh, state_hgrn[l], n_rows=ts, cs=ts)
        xs, buf = _out_ffn(xs, ol, ob, zh, state_ffn_conv[l], w, fin, n_rows=ts, pad_front=0, last=last)
        for lst, a in zip(outs[4:], (kv, pe, s_new, buf)):
            lst.append(a)

    return (xp, xs) + tuple(jnp.stack(o) for o in outs)
```

```python
import functools

import jax
import jax.numpy as jnp
from jax import lax
from jax.experimental import pallas as pl
from jax.experimental.pallas import tpu as pltpu

F32 = jnp.float32
BF16 = jnp.bfloat16

CHUNK = 64
N_META = 16
PAD_FRONT = CHUNK - N_META
EPS = 1e-6
MLA_HEADS = 8
QK_NOPE = 64
QK_ROPE = 32
V_HEAD = 64
Q_LORA = 256
KV_LORA = 128
ROPE_THETA = 10000.0
HG_HEADS = 4
HG_DK = 128
HG_DV = 128
CONV_W = 3

HG_KW = HG_HEADS * HG_DK
HG_WIDTH = HG_HEADS * HG_DV
LANES = 128
KEY_BLOCK = 256
QCAT = KV_LORA + LANES
VT_ROWS = KV_LORA + 16
SOFTMAX_FAST_RANGE = 100.0
PREFIX_KEYS = N_META
IN_PROJ_ROWS = 256
IN_PROJ_STREAMS = 2
Q_SCALE = (QK_NOPE + QK_ROPE) ** -0.5 * 1.4426950408889634
HGRN_FAST_RANGE = 80.0
FFN_ROWS = 256
FFN_STREAMS = 2
HGRN_STREAMS = 8
V7X_VMEM_BYTES = 64 * 1024 * 1024
VMEM_LIMIT = V7X_VMEM_BYTES * 7 // 8


def _row_tile(n, target):
    if n <= target:
        return n
    best = None
    for t in range(16, target + 1, 16):
        if n % t == 0:
            best = t
    assert best is not None, (n, target)
    return best


def _const_spec(shape):
    nd = len(shape)
    return pl.BlockSpec(shape, lambda *_: (0,) * nd, pipeline_mode=pl.Buffered(1))


def _rms(x, w):
    return x * lax.rsqrt(jnp.mean(x * x, axis=-1, keepdims=True) + EPS) * w


def _silu(x):
    return x * jax.nn.sigmoid(x)


def _mm(a, b):
    return jnp.dot(a.astype(BF16), b.astype(BF16), preferred_element_type=F32)


def _mm_nt(a, b):
    return lax.dot_general(a.astype(BF16), b.astype(BF16), (((1,), (1,)), ((), ())),
                           preferred_element_type=F32)


def _mm_tn(a, b):
    return lax.dot_general(a.astype(BF16), b.astype(BF16), (((0,), (0,)), ((), ())),
                           preferred_element_type=F32)


def _in_proj_body(x_ref, cos_ref, sn_ref, sp_ref, cosq_ref, sinq_ref, npre_ref, win_ref, qn_ref, kvn_ref, wuq_ref,
                  wuk_ref, spread_ref, lb_ref, qcat_ref, kcat_ref, kvt_ref, kv_ref, pe_ref, zh_ref,
                  *, bb, tm, n_valid, pad_front):
    cos, sn, sp = cos_ref[...], sn_ref[...], sp_ref[...]
    lb = lb_ref[...]

    def rope(v):
        return v * cos + pltpu.roll(v, LANES - QK_ROPE // 2, 1) * sn + pltpu.roll(v, QK_ROPE // 2, 1) * sp

    row = pl.program_id(1) * tm + lax.broadcasted_iota(jnp.int32, (tm, 1), 0)
    live = row < n_valid
    real = row >= pad_front
    h = [_rms(jnp.where(live, x_ref[bi], 0.0), npre_ref[...]) for bi in range(bb)]
    zs = [_mm(h[bi], win_ref[...]) for bi in range(bb)]
    nope_w = MLA_HEADS * QK_NOPE
    for bi, z in enumerate(zs):
        off = Q_LORA + KV_LORA + LANES
        hf = z[:, off + HG_KW:off + 2 * HG_KW]
        e = jnp.exp(-jnp.abs(hf))
        r = 1.0 / (1.0 + e)
        f = lb + (1.0 - lb) * jnp.where(hf >= 0.0, r, e * r)
        zh_ref[bi, :, :HG_KW] = jnp.where(real, _silu(z[:, off:off + HG_KW]), 0.0)
        zh_ref[bi, :, HG_KW:2 * HG_KW] = jnp.where(real, jnp.log(f), 0.0)
        zh_ref[bi, :, 2 * HG_KW:3 * HG_KW] = jnp.where(real, (1.0 - lb) * jnp.where(hf >= 0.0, e * r, r), 0.0)
        zh_ref[bi, :, 3 * HG_KW:] = z[:, off + 2 * HG_KW:]
        kv_lat = _rms(z[:, Q_LORA:Q_LORA + KV_LORA], kvn_ref[...])
        k_pe = rope(z[:, Q_LORA + KV_LORA:Q_LORA + KV_LORA + LANES])
        kv_ref[bi] = kv_lat
        pe_ref[bi] = k_pe[:, :QK_ROPE]
        kcat_ref[bi, :, :KV_LORA] = kv_lat.astype(BF16)
        kcat_ref[bi, :, KV_LORA:] = k_pe.astype(BF16)
        kvt_ref[bi, :KV_LORA, :] = kv_lat.T.astype(BF16)
        kvt_ref[bi, KV_LORA:, :] = jnp.ones((VT_ROWS - KV_LORA, tm), BF16)

        cqn = _rms(z[:, :Q_LORA], qn_ref[...])
        q = _mm(cqn, wuq_ref[...]) * Q_SCALE
        x1, x2 = q[:, nope_w:nope_w + LANES], q[:, nope_w + LANES:]
        cq, sq = cosq_ref[...], sinq_ref[...]
        roped = jnp.concatenate([x1 * cq - x2 * sq, x1 * sq + x2 * cq], axis=1)
        pe_heads = _mm(roped, spread_ref[...])
        for j in range(MLA_HEADS // 2):
            ql = _mm(q[:, 2 * QK_NOPE * j:2 * QK_NOPE * (j + 1)], wuk_ref[j])
            for hh in range(2):
                hd = 2 * j + hh
                qcat_ref[bi, hd, :, :KV_LORA] = ql[:, KV_LORA * hh:KV_LORA * (hh + 1)].astype(BF16)
                qcat_ref[bi, hd, :, KV_LORA:] = pe_heads[:, LANES * hd:LANES * (hd + 1)].astype(BF16)


def _in_proj(x, tabs, w, *, n_valid, pad_front):
    bx, t, d = x.shape
    tm = _row_tile(t, IN_PROJ_ROWS)
    assert tm % LANES == 0 or tm == t
    nt = t // tm
    bb = IN_PROJ_STREAMS if bx % IN_PROJ_STREAMS == 0 else 1
    row = lambda shape: pl.BlockSpec((bb, tm) + shape, lambda b, i: (b, i) + (0,) * len(shape))
    tab = pl.BlockSpec((tm, LANES), lambda b, i: (i, 0))
    zw = 3 * HG_KW + 2 * HG_WIDTH
    out_shape = (
        jax.ShapeDtypeStruct((bx, MLA_HEADS, t, QCAT), BF16),
        jax.ShapeDtypeStruct((bx, t, QCAT), BF16),
        jax.ShapeDtypeStruct((bx, VT_ROWS, t), BF16),
        jax.ShapeDtypeStruct((bx, t, KV_LORA), F32),
        jax.ShapeDtypeStruct((bx, t, QK_ROPE), F32),
        jax.ShapeDtypeStruct((bx, t, zw), F32),
    )
    out_specs = (
        pl.BlockSpec((bb, MLA_HEADS, tm, QCAT), lambda b, i: (b, 0, i, 0)),
        row((QCAT,)),
        pl.BlockSpec((bb, VT_ROWS, tm), lambda b, i: (b, 0, i)),
        row((KV_LORA,)), row((QK_ROPE,)), row((zw,)),
    )
    weights = (w['norm_mix_pre'], w['w_in'], w['q_norm'], w['kv_norm'], w['w_uq'], w['w_uk'], w['pe_spread'],
               w['hg_lb'])
    return pl.pallas_call(
        functools.partial(_in_proj_body, bb=bb, tm=tm, n_valid=n_valid, pad_front=pad_front),
        grid=(bx // bb, nt),
        in_specs=[row((d,))] + [tab] * len(tabs) + [_const_spec(a.shape) for a in weights],
        out_specs=out_specs,
        out_shape=out_shape,
        compiler_params=pltpu.CompilerParams(dimension_semantics=("parallel", "parallel"),
                                             vmem_limit_bytes=VMEM_LIMIT),
        name="in_proj",
    )(x, *tabs, *weights)


def _mla_prompt_body(*refs, qt, j0, aliased, dummy, prefix):
    q_ref, k_ref, vt_ref = refs[:3]
    kpre_ref, vtpre_ref = refs[3:5] if prefix else (None, None)
    o_ref, m_scr, top_scr, acc_scr = refs[-4:]
    j = pl.program_id(1) + j0
    width = MLA_HEADS * qt
    pre = PREFIX_KEYS if prefix else 0
    lower_half = lax.broadcasted_iota(jnp.int32, (CHUNK, LANES), 1) < CHUNK

    def hidden(rows, cols=LANES):
        return jnp.full((rows, cols), -jnp.inf, F32)

    def mask_dummy(s):
        return jnp.concatenate([hidden(dummy, s.shape[1]), s[dummy:]], axis=0) if dummy else s

    def mask_diagonal(s):
        strips = []
        for c0 in range(0, width, LANES):
            rows = [s[:pre, c0:c0 + LANES]] if prefix else []
            t = s[pre:, c0:c0 + LANES]
            if (c0 % qt) // LANES == 0:
                rows += [t[:CHUNK], jnp.where(lower_half, -jnp.inf, t[CHUNK:2 * CHUNK]), hidden(2 * CHUNK)]
            else:
                rows += [t[:3 * CHUNK], jnp.where(lower_half, -jnp.inf, t[3 * CHUNK:])]
            strips.append(jnp.concatenate(rows, axis=0))
        return jnp.concatenate(strips, axis=1)

    def keys_of(kb):
        start = kb * KEY_BLOCK
        return pl.ds(start if isinstance(kb, int) else pl.multiple_of(start, KEY_BLOCK), KEY_BLOCK)

    def scores(kb, diagonal=False):
        keys = k_ref[0, keys_of(kb), :]
        if diagonal and prefix:
            keys = jnp.concatenate([kpre_ref[...], keys], axis=0)
        return _mm_nt(keys, q_ref[0].reshape(width, QCAT))

    def weighted_values(p, kb, diagonal):
        if diagonal and prefix:
            return _mm(vtpre_ref[...], p[:pre]) + _mm(vt_ref[0, :, keys_of(kb)], p[pre:])
        return _mm(vt_ref[0, :, keys_of(kb)], p)

    def absorb(s, kb, diagonal=False):
        if diagonal:
            s = mask_diagonal(s)
        if isinstance(kb, int) and kb == 0:
            s = mask_dummy(s)
        for hd in range(MLA_HEADS):
            cols = slice(qt * hd, qt * (hd + 1))
            m_new = jnp.max(s[:, cols], axis=0, keepdims=True)
            if not diagonal:
                m_old = m_scr[:, cols]
                m_new = jnp.maximum(m_old, m_new)
            pv = weighted_values(jnp.exp2(s[:, cols] - m_new), kb, diagonal)
            acc_scr[:, cols] = pv if diagonal else jnp.exp2(m_old - m_new) * acc_scr[:, cols] + pv
            m_scr[:, cols] = m_new

    def absorb_all(blocks):
        for s, kb, diagonal in blocks:
            absorb(s, kb, diagonal)

    def absorb_unscaled(blocks):
        blocks = [(mask_diagonal(s) if diagonal else s, kb, diagonal) for s, kb, diagonal in blocks]
        blocks = [(mask_dummy(s) if isinstance(kb, int) and kb == 0 else s, kb, d) for s, kb, d in blocks]
        first = blocks[0][2]
        for hd in range(MLA_HEADS):
            cols = slice(qt * hd, qt * (hd + 1))
            tops = [jnp.max(s[:, cols], axis=0, keepdims=True) for s, _, _ in blocks]
            if first:
                shift = jnp.max(blocks[0][0][:PREFIX_KEYS, cols], axis=0, keepdims=True) if prefix else tops[0]
                m_scr[:, cols] = shift
                top_scr[:, cols] = functools.reduce(jnp.maximum, tops)
            else:
                shift = m_scr[:, cols]
                top_scr[:, cols] = functools.reduce(jnp.maximum, tops, top_scr[:, cols])
            pv = None
            for s, kb, diagonal in blocks:
                term = weighted_values(jnp.exp2(s[:, cols] - shift), kb, diagonal)
                pv = term if pv is None else pv + term
            acc_scr[:, cols] = pv if first else acc_scr[:, cols] + pv

    def visit_blocks(absorb_blocks):
        def pair(a, b, diagonal=False):
            s_a, s_b = scores(a, diagonal), scores(b)
            absorb_blocks([(s_a, a, diagonal), (s_b, b, False)])

        def absorb_block(s, kb, diagonal=False):
            absorb_blocks([(s, kb, diagonal)])

        @pl.when(j == 0)
        def _():
            absorb_block(scores(0, True), 0, True)

        n_blocks = k_ref.shape[1] // KEY_BLOCK
        if n_blocks < 2:
            return

        @pl.when(j == 1)
        def _():
            pair(1, 0, True)

        if n_blocks < 3:
            return

        @pl.when(j > 1)
        def _():
            pair(j, j - 1, True)
            n_inner = (j - 2) // 2

            def inner(p, carry):
                pair(j - 2 * p, j - 2 * p - 1)
                return carry

            lax.fori_loop(1, n_inner + 1, inner, 0)

            @pl.when(j % 2 == 0)
            def _():
                absorb_block(scores(0), 0)

            @pl.when(j % 2 == 1)
            def _():
                pair(1, 0)

    def finish():
        for hd in range(MLA_HEADS):
            acc = acc_scr[:, qt * hd:qt * (hd + 1)]
            o_t = acc[:KV_LORA] * (1.0 / acc[KV_LORA:KV_LORA + 1])
            o_ref[0, :, KV_LORA * hd:KV_LORA * (hd + 1)] = o_t.T.astype(BF16)

    visit_blocks(absorb_unscaled)
    runaway = jnp.max(top_scr[...] - m_scr[...]) > SOFTMAX_FAST_RANGE
    finish()

    @pl.when(runaway)
    def _():
        visit_blocks(absorb_all)
        finish()


def _mla_prompt_call(qcat, kcat, kvt, prefix, prev, *, qt, j0, steps, dummy, rows_out):
    b, _, t, _ = qcat.shape
    per_block = KEY_BLOCK // qt
    in_specs = [pl.BlockSpec((1, MLA_HEADS, qt, QCAT), lambda i, j: (i, 0, (j + j0) * per_block, 0)),
                pl.BlockSpec((1, t, QCAT), lambda i, j: (i, 0, 0)),
                pl.BlockSpec((1, VT_ROWS, t), lambda i, j: (i, 0, 0))]
    args = (qcat, kcat, kvt)
    if prefix is not None:
        in_specs += [_const_spec(a.shape) for a in prefix]
        args += tuple(prefix)
    if prev is not None:
        in_specs.append(pl.BlockSpec(memory_space=pl.ANY))
        args += (prev,)
    return pl.pallas_call(
        functools.partial(_mla_prompt_body, qt=qt, j0=j0, aliased=prev is not None, dummy=dummy,
                          prefix=prefix is not None),
        grid=(b, steps),
        in_specs=in_specs,
        out_specs=pl.BlockSpec((1, qt, MLA_HEADS * KV_LORA), lambda i, j: (i, (j + j0) * per_block, 0)),
        out_shape=jax.ShapeDtypeStruct((b, rows_out, MLA_HEADS * KV_LORA), BF16),
        input_output_aliases={len(args) - 1: 0} if prev is not None else {},
        scratch_shapes=[pltpu.VMEM((1, MLA_HEADS * qt), F32), pltpu.VMEM((1, MLA_HEADS * qt), F32),
                        pltpu.VMEM((VT_ROWS, MLA_HEADS * qt), F32)],
        compiler_params=pltpu.CompilerParams(dimension_semantics=("parallel", "parallel"),
                                             vmem_limit_bytes=VMEM_LIMIT),
        name="mla_prompt",
    )(*args)


def _mla_prompt(qcat, kcat, kvt, prefix, *, lp, dummy):
    t = qcat.shape[2]
    assert t % KEY_BLOCK == 0 and lp <= t
    full, rest = divmod(lp, KEY_BLOCK)
    if rest > LANES:
        full, rest = full + 1, 0
    out = None
    rows = full * KEY_BLOCK + (LANES if rest else 0)
    if full:
        out = _mla_prompt_call(qcat, kcat, kvt, prefix, None, qt=KEY_BLOCK, j0=0, steps=full, dummy=dummy,
                               rows_out=rows)
    if rest:
        out = _mla_prompt_call(qcat, kcat, kvt, prefix, out, qt=LANES, j0=full, steps=1, dummy=dummy,
                               rows_out=rows)
    return out


def _mla_sample_body(q_ref, k_ref, ckv_ref, cpe_ref, o_ref, *, ts):
    rows = MLA_HEADS * ts
    q = q_ref[0].reshape(rows, QCAT)
    ckv = ckv_ref[0].astype(BF16)
    knew = k_ref[0]
    s_c = _mm_nt(q[:, :KV_LORA], ckv) + _mm_nt(q[:, KV_LORA:KV_LORA + QK_ROPE], cpe_ref[0])
    s_n = _mm_nt(q, knew)
    m = jnp.maximum(jnp.max(s_c, axis=-1, keepdims=True), jnp.max(s_n, axis=-1, keepdims=True))
    p_c = jnp.exp2(s_c - m)
    p_n = jnp.exp2(s_n - m)
    l = jnp.sum(p_c, axis=-1, keepdims=True) + jnp.sum(p_n, axis=-1, keepdims=True)
    o = (_mm(p_c, ckv) + _mm(p_n, knew[:, :KV_LORA])) / l
    for hd in range(MLA_HEADS):
        o_ref[0, :, KV_LORA * hd:KV_LORA * (hd + 1)] = o[ts * hd:ts * (hd + 1)].astype(BF16)


def _mla_sample(qcat, kcat, cache_kv, cache_pe):
    b, _, ts, _ = qcat.shape
    past = cache_kv.shape[1]
    return pl.pallas_call(
        functools.partial(_mla_sample_body, ts=ts),
        grid=(b,),
        in_specs=[pl.BlockSpec((1, MLA_HEADS, ts, QCAT), lambda i: (i, 0, 0, 0)),
                  pl.BlockSpec((1, ts, QCAT), lambda i: (i, 0, 0)),
                  pl.BlockSpec((1, past, KV_LORA), lambda i: (i, 0, 0)),
                  pl.BlockSpec((1, past, QK_ROPE), lambda i: (i, 0, 0))],
        out_specs=pl.BlockSpec((1, ts, MLA_HEADS * KV_LORA), lambda i: (i, 0, 0)),
        out_shape=jax.ShapeDtypeStruct((b, ts, MLA_HEADS * KV_LORA), BF16),
        compiler_params=pltpu.CompilerParams(dimension_semantics=("parallel",),
                                             vmem_limit_bytes=VMEM_LIMIT),
        name="mla_sample",
    )(qcat, kcat, cache_kv, cache_pe)


def _hgrn_body(zh_ref, s0_ref, ob_ref, sout_ref, st_ref, g_scr, q_scr, k_scr, v_scr, o_scr, oi_scr, *, bb, cs):
    c = pl.program_id(1)
    nc = pl.num_programs(1)
    streams = range(bb)
    heads = [slice(HG_DK * hd, HG_DK * (hd + 1)) for hd in range(HG_HEADS)]

    @pl.when(c == 0)
    def _():
        for bi in streams:
            for hd in range(HG_HEADS):
                st_ref[bi, hd] = s0_ref[bi, hd].T

    q = zh_ref[:, :, :HG_KW]
    gl = zh_ref[:, :, HG_KW:2 * HG_KW]
    k = zh_ref[:, :, 2 * HG_KW:3 * HG_KW]
    v = zh_ref[:, :, 3 * HG_KW:3 * HG_KW + HG_WIDTH]
    t_idx = lax.broadcasted_iota(jnp.int32, (cs, cs), 0)
    s_idx = lax.broadcasted_iota(jnp.int32, (cs, cs), 1)
    causal = t_idx >= s_idx
    row = lax.broadcasted_iota(jnp.int32, (cs, 1), 0)
    g = [gl[bi] for bi in streams]
    shift = 1
    while shift < cs:
        g = [x + jnp.where(row >= shift, pltpu.roll(x, shift, 0), 0.0) for x in g]
        shift *= 2
    g = jnp.stack(g)
    g_tot = g[:, cs - 1:cs, :]
    any_out_of_range = jnp.max(-g_tot) > HGRN_FAST_RANGE
    qt = q * jnp.exp(g)
    khat = k * jnp.exp(g_tot - g)
    kt = k * jnp.exp(-g)
    a = [[jnp.where(causal, _mm_nt(qt[bi][:, sl], kt[bi][:, sl]), 0.0) for sl in heads] for bi in streams]
    for bi in streams:
        o_scr[bi] = jnp.concatenate([_mm(a[bi][hd], v[bi][:, sl]) for hd, sl in enumerate(heads)], axis=-1)

    o_inter = [[_mm_nt(qt[bi][:, sl], st_ref[bi, hd]) for hd, sl in enumerate(heads)] for bi in streams]
    kv_new = [[_mm_tn(v[bi][:, sl], khat[bi][:, sl]) for sl in heads] for bi in streams]
    decay = jnp.exp(g_tot)
    for bi in streams:
        oi_scr[bi] = jnp.concatenate(o_inter[bi], axis=-1)
        for hd, sl in enumerate(heads):
            st_ref[bi, hd] = decay[bi][:, sl] * st_ref[bi, hd] + kv_new[bi][hd]

    @pl.when(any_out_of_range)
    def _():
        for bi in streams:
            @pl.when(jnp.max(-g_tot[bi]) > HGRN_FAST_RANGE)
            def _(bi=bi):
                g_scr[...] = g[bi]
                q_scr[...] = q[bi]
                k_scr[...] = k[bi]
                v_scr[...] = v[bi]
                o_scr[bi] = jnp.zeros((cs, HG_WIDTH), F32)
                t_col = lax.broadcasted_iota(jnp.int32, (cs, 1), 0)

                def cols(i, carry):
                    base = pl.multiple_of(i * 8, 8)
                    g8, k8, v8 = (ref[pl.ds(base, 8), :] for ref in (g_scr, k_scr, v_scr))
                    for j in range(8):
                        dec = jnp.exp(jnp.where(t_col >= base + j, g_scr[...] - g8[j:j + 1], -jnp.inf))
                        pr = q_scr[...] * dec * k8[j:j + 1]
                        for sl in heads:
                            o_scr[bi, :, sl] += jnp.sum(pr[:, sl], axis=-1, keepdims=True) * v8[j:j + 1, sl]
                    return carry

                lax.fori_loop(0, cs // 8, cols, 0)

    for bi in streams:
        ob_ref[bi] = o_scr[bi] + oi_scr[bi]

    @pl.when(c == nc - 1)
    def _():
        for bi in streams:
            for hd in range(HG_HEADS):
                sout_ref[bi, hd] = st_ref[bi, hd].T


def _hgrn(zh, s0, *, n_rows, cs):
    b, t, zw = zh.shape
    assert n_rows % cs == 0 and n_rows <= t
    bb = HGRN_STREAMS if b % HGRN_STREAMS == 0 else 1
    return pl.pallas_call(
        functools.partial(_hgrn_body, bb=bb, cs=cs),
        grid=(b // bb, n_rows // cs),
        in_specs=[pl.BlockSpec((bb, cs, zw - HG_WIDTH), lambda i, c: (i, c, 0)),
                  pl.BlockSpec((bb, HG_HEADS, HG_DK, HG_DV), lambda i, c: (i, 0, 0, 0))],
        out_specs=(pl.BlockSpec((bb, cs, HG_WIDTH), lambda i, c: (i, c, 0)),
                   pl.BlockSpec((bb, HG_HEADS, HG_DK, HG_DV), lambda i, c: (i, 0, 0, 0))),
        out_shape=(jax.ShapeDtypeStruct((b, n_rows, HG_WIDTH), F32),
                   jax.ShapeDtypeStruct((b, HG_HEADS, HG_DK, HG_DV), F32)),
        scratch_shapes=[pltpu.VMEM((bb, HG_HEADS, HG_DV, HG_DK), F32)]
        + [pltpu.VMEM((cs, HG_KW), F32) for _ in range(4)] + [pltpu.VMEM((bb, cs, HG_KW), F32) for _ in range(2)],
        compiler_params=pltpu.CompilerParams(dimension_semantics=("parallel", "arbitrary"),
                                             vmem_limit_bytes=VMEM_LIMIT),
        name="hgrn",
    )(zh, s0)


def _out_ffn_body(x_ref, ol_ref, ob_ref, gate_ref, buf_ref, gain_ref, wuv_ref, wout_ref, npost_ref, nfpre_ref,
                  wup_ref, cw_ref, cb_ref, wdn_ref, nfpost_ref, fin_ref, xo_ref, bufo_ref, a_scr,
                  *, bb, tm, pad_front, last, flat):
    t = pl.program_id(1)
    nt = pl.num_programs(1)
    d_ff = cw_ref.shape[1]
    keep = CONV_W - 1
    streams = range(bb)

    @pl.when(t == 0)
    def _():
        for bi in streams:
            a_scr[bi, 0:8, :] = jnp.zeros((8, d_ff), F32)
            a_scr[bi, 8 - keep:8, :] = buf_ref[bi]

    def mm_streams(xs, w):
        if not flat:
            return [_mm(x, w) for x in xs]
        out = _mm(jnp.concatenate(xs, axis=0), w)
        return [out[tm * bi:tm * (bi + 1)] for bi in streams]

    ol = [ol_ref[bi] for bi in streams]
    o_a = [mm_streams([o[:, 2 * KV_LORA * j:2 * KV_LORA * (j + 1)] for o in ol], wuv_ref[j])
           for j in range(MLA_HEADS // 2)]
    mixed = []
    for bi in streams:
        ob = ob_ref[bi]
        o_b = [_rms(ob[:, HG_DV * hd:HG_DV * (hd + 1)], gain_ref[...]) for hd in range(HG_HEADS)]
        o_b = (jnp.concatenate(o_b, axis=-1) * _silu(gate_ref[bi])).astype(BF16)
        mixed.append(jnp.concatenate([o[bi].astype(BF16) for o in o_a] + [o_b], axis=-1))
    y = mm_streams(mixed, wout_ref[...])
    x1 = [x_ref[bi] + _rms(y[bi], npost_ref[...]) for bi in streams]
    u = mm_streams([_rms(x1[bi], nfpre_ref[...]).astype(BF16) for bi in streams], wup_ref[...])
    cw = cw_ref[...]
    if pad_front:
        live = (t * tm + lax.broadcasted_iota(jnp.int32, (tm, 1), 0)) >= pad_front
    gated = []
    for bi in streams:
        a = u[bi][:, :d_ff]
        if pad_front:
            a = jnp.where(live, a, 0.0)
        a_scr[bi, 8:8 + tm, :] = a
        conv = cb_ref[...] + a * cw[keep:keep + 1]
        for j in range(keep):
            conv = conv + a_scr[bi, 8 - keep + j:8 - keep + j + tm, :] * cw[j:j + 1]
        gated.append((_silu(conv) * u[bi][:, d_ff:]).astype(BF16))
    f = mm_streams(gated, wdn_ref[...])
    for bi in streams:
        x2 = x1[bi] + _rms(f[bi], nfpost_ref[...])
        xo_ref[bi] = _rms(x2, fin_ref[...]) if last else x2
        a_scr[bi, 0:8, :] = a_scr[bi, tm:tm + 8, :]

    @pl.when(t == nt - 1)
    def _():
        for bi in streams:
            bufo_ref[bi] = a_scr[bi, 8 - keep:8, :]


def _out_ffn(x, ol, ob, zh, buf, w, final_norm, *, n_rows, pad_front, last):
    b, t, d = x.shape
    d_ff = buf.shape[-1]
    tm = _row_tile(n_rows, FFN_ROWS)
    assert tm >= 8 and n_rows <= t and zh.shape[-1] % HG_WIDTH == 0
    flat = tm < LANES
    bb = b if flat else (FFN_STREAMS if b % FFN_STREAMS == 0 else 1)
    row = lambda width: pl.BlockSpec((bb, tm, width), lambda i, j: (i, j, 0))
    gate =pl.BlockSpec((bb, tm, HG_WIDTH), lambda i, j: (i, j, zh.shape[-1] // HG_WIDTH - 1))
    per_b = pl.BlockSpec((bb, CONV_W - 1, d_ff), lambda i, j: (i, 0, 0))
    weights = (w['hg_gain'], w['w_uv'], w['w_out'], w['norm_mix_post'], w['norm_ffn_pre'], w['w_ffn_up'],
               w['ffn_conv_w'], w['ffn_conv_b'], w['w_ffn_down'], w['norm_ffn_post'], final_norm)
    return pl.pallas_call(
        functools.partial(_out_ffn_body, bb=bb, tm=tm, pad_front=pad_front, last=last, flat=flat),
        grid=(b // bb, n_rows // tm),
        in_specs=[row(d), row(MLA_HEADS * KV_LORA), row(HG_WIDTH), gate, per_b]
        + [_const_spec(a.shape) for a in weights],
        out_specs=(row(d), per_b),
        out_shape=(jax.ShapeDtypeStruct((b, n_rows, d), F32), jax.ShapeDtypeStruct((b, CONV_W - 1, d_ff), F32)),
        scratch_shapes=[pltpu.VMEM((bb, tm + 8, d_ff), F32)],
        compiler_params=pltpu.CompilerParams(dimension_semantics=("parallel", "arbitrary"),
                                             vmem_limit_bytes=VMEM_LIMIT),
        name="out_ffn",
    )(x, ol, ob, zh, buf, *weights)


def _rope_tables(pos):
    half = QK_ROPE // 2
    inv = ROPE_THETA ** (-jnp.arange(half, dtype=F32) / half)
    ang = pos.astype(F32)[:, None] * inv
    cos, sin = jnp.cos(ang), jnp.sin(ang)
    zero = jnp.zeros_like(cos)
    pad = jnp.zeros((pos.shape[0], LANES - QK_ROPE), F32)
    return (jnp.concatenate([cos, cos, pad], axis=1),
            jnp.concatenate([-sin, zero, pad], axis=1),
            jnp.concatenate([zero, sin, pad], axis=1),
            jnp.tile(cos, (1, MLA_HEADS)), jnp.tile(sin, (1, MLA_HEADS)))


def _layer_weights(l, w_in, norm_mix_pre, norm_mix_post, q_norm, kv_norm, w_uq, w_uk, w_uv, hg_out_norm, w_out,
                   norm_ffn_pre, norm_ffn_post, w_ffn_up, ffn_conv_w, ffn_conv_b, w_ffn_down):
    d = w_in.shape[1]
    head_w = Q_LORA + KV_LORA + QK_ROPE
    w_in_r = jnp.concatenate([w_in[l][:, :head_w], jnp.zeros((d, LANES - QK_ROPE), F32), w_in[l][:, head_w:]], axis=1)
    uq = w_uq[l].reshape(Q_LORA, MLA_HEADS, QK_NOPE + QK_ROPE)
    half = QK_ROPE // 2
    w_uq_r = jnp.concatenate([uq[:, :, :QK_NOPE].reshape(Q_LORA, -1),
                              uq[:, :, QK_NOPE:QK_NOPE + half].reshape(Q_LORA, -1),
                              uq[:, :, QK_NOPE + half:].reshape(Q_LORA, -1)], axis=1)
    src = jnp.arange(2 * MLA_HEADS * half)
    dst = (src % (MLA_HEADS * half)) // half * LANES + src // (MLA_HEADS * half) * half + src % half
    pe_spread = (dst[:, None] == jnp.arange(MLA_HEADS * LANES)[None, :]).astype(F32)

    def block_diagonal_pairs(m):
        zero = jnp.zeros_like(m[0::2])
        return jnp.concatenate([jnp.concatenate([m[0::2], zero], axis=2),
                                jnp.concatenate([zero, m[1::2]], axis=2)], axis=1)

    uk_pair = block_diagonal_pairs(jnp.transpose(w_uk[l], (1, 2, 0)))
    uv_pair = block_diagonal_pairs(jnp.transpose(w_uv[l], (1, 0, 2)))
    row = lambda a: a.reshape(1, -1)
    return {
        'w_in': w_in_r.astype(BF16), 'norm_mix_pre': row(norm_mix_pre[l]), 'norm_mix_post': row(norm_mix_post[l]),
        'q_norm': row(q_norm[l]), 'kv_norm': row(kv_norm[l]), 'w_uq': w_uq_r.astype(BF16),
        'w_uk': uk_pair.astype(BF16), 'w_uv': uv_pair.astype(BF16), 'pe_spread': pe_spread.astype(BF16),
        'hg_gain': row(hg_out_norm[l]), 'w_out': w_out[l].astype(BF16),
        'norm_ffn_pre': row(norm_ffn_pre[l]), 'norm_ffn_post': row(norm_ffn_post[l]),
        'w_ffn_up': w_ffn_up[l].astype(BF16), 'ffn_conv_w': ffn_conv_w[l], 'ffn_conv_b': row(ffn_conv_b[l]),
        'w_ffn_down': w_ffn_down[l].astype(BF16),
    }


def kernel(x_prompt, x_sample, cache_kv_latent, cache_k_rope, state_hgrn, state_ffn_conv, meta_tokens, w_in, norm_mix_pre, norm_mix_post, q_norm, kv_norm, w_uq, w_uk, w_uv, hg_lower_bounds, hg_out_norm, w_out, norm_ffn_pre, norm_ffn_post, w_ffn_up, ffn_conv_w, ffn_conv_b, w_ffn_down, final_norm):
    depth = w_in.shape[0]
    b, seq, d = x_prompt.shape
    bs, ts, _ = x_sample.shape
    past = cache_kv_latent.shape[2]
    d_ff = ffn_conv_w.shape[-1]
    assert seq % CHUNK == 0

    lb_sm = jax.nn.softmax(hg_lower_bounds.astype(F32), axis=0)
    lb_all = jnp.cumsum(lb_sm, axis=0) - lb_sm[0]

    assert KEY_BLOCK % IN_PROJ_ROWS == 0
    lm = PAD_FRONT + N_META
    lm_alloc = -(-lm // KEY_BLOCK) * KEY_BLOCK
    seq_alloc = -(-seq // KEY_BLOCK) * KEY_BLOCK
    allocated = lambda a, rows: a if a.shape[1] == rows else jnp.pad(a, ((0, 0), (0, rows - a.shape[1]), (0, 0)))
    xm = allocated(jnp.concatenate([jnp.zeros((1, PAD_FRONT, d), F32), meta_tokens[None]], axis=1), lm_alloc)
    tabs_m = _rope_tables(jnp.arange(lm_alloc, dtype=jnp.int32) - lm)
    xp = allocated(x_prompt, seq_alloc)
    tabs_p = _rope_tables(jnp.arange(seq_alloc, dtype=jnp.int32))
    xs = x_sample
    tabs_s = tuple(jnp.tile(a, (bs, 1)) for a in _rope_tables(past + jnp.arange(ts, dtype=jnp.int32)))
    fin = final_norm.reshape(1, -1)
    s0_m = jnp.zeros((1, HG_HEADS, HG_DK, HG_DV), F32)
    buf0_m = jnp.zeros((1, CONV_W - 1, d_ff), F32)
    meta_rows = slice(PAD_FRONT, lm)

    outs = [[] for _ in range(8)]
    for l in range(depth):
        w = _layer_weights(l, w_in, norm_mix_pre, norm_mix_post, q_norm, kv_norm, w_uq, w_uk, w_uv, hg_out_norm,
                           w_out, norm_ffn_pre, norm_ffn_post, w_ffn_up, ffn_conv_w, ffn_conv_b, w_ffn_down)
        w['hg_lb'] = lb_all[l].reshape(1, -1)
        last = l == depth - 1

        qcat, kcat, kvt, kv_m, pe_m, zh = _in_proj(xm, tabs_m, w, n_valid=lm, pad_front=PAD_FRONT)
        ol = _mla_prompt(qcat, kcat, kvt, None, lp=lm, dummy=PAD_FRONT)
        ob, s_m = _hgrn(zh, s0_m, n_rows=lm, cs=CHUNK)
        xm, buf_m = _out_ffn(xm, ol, ob, zh, buf0_m, w, fin, n_rows=lm, pad_front=PAD_FRONT, last=last)
        xm = allocated(xm, lm_alloc)
        prefix = (kcat[0, meta_rows], kvt[0, :, meta_rows])

        qcat, kcat, kvt, kv, pe, zh = _in_proj(allocated(xp, seq_alloc), tabs_p, w, n_valid=seq, pad_front=0)
        ol = _mla_prompt(qcat, kcat, kvt, prefix, lp=seq, dummy=0)
        ob, s_new = _hgrn(zh, jnp.broadcast_to(s_m, (b,) + s_m.shape[1:]), n_rows=seq, cs=CHUNK)
        xp, buf = _out_ffn(xp, ol, ob, zh, jnp.broadcast_to(buf_m, (b,) + buf_m.shape[1:]), w, fin, n_rows=seq,
                           pad_front=0, last=last)
        with_meta = lambda m, p: jnp.concatenate(
            [jnp.broadcast_to(m[:, meta_rows], (b, N_META, m.shape[-1])), p[:, :seq]], axis=1)
        for lst, a in zip(outs[:4], (with_meta(kv_m, kv), with_meta(pe_m, pe), s_new, buf)):
            lst.append(a)

        qcat, kcat, _, kv, pe, zh = _in_proj(xs.reshape(1, bs * ts, d), tabs_s, w, n_valid=bs * ts, pad_front=0)
        qcat = jnp.swapaxes(qcat.reshape(MLA_HEADS, bs, ts, QCAT), 0, 1)
        kcat, kv, pe, zh = (a.reshape(bs, ts, -1) for a in (kcat, kv, pe, zh))
        ol = _mla_sample(qcat, kcat, cache_kv_latent[l], cache_k_rope[l])
        ob, s_new = _hgrn(zh, state_hgrn[l], n_rows=ts, cs=ts)
        xs, buf = _out_ffn(xs, ol, ob, zh, state_ffn_conv[l], w, fin, n_rows=ts, pad_front=0, last=last)
        for lst, a in zip(outs[4:], (kv, pe, s_new, buf)):
            lst.append(a)

    return (xp, xs) + tuple(jnp.stack(o) for o in outs)
```
